```python
import math
import jax, jax.numpy as jnp
from jax import lax
import numpy as np

D_MODEL = 1024
BATCH = 16
SEQ = 2048
DEPTH = 2
DEC_BATCH = 32
DEC_SEQ = 1
PAST_LEN = 16384
PAGE_SIZE = 128

HEAD_DIM = 64
MIX_WIDTH = D_MODEL
X_HEADS = 4
X_WIDTH = X_HEADS * HEAD_DIM
TOK_WIDTH = MIX_WIDTH - X_WIDTH
N_MEM = 256
M_HEADS = 8
M_HEAD_DIM = TOK_WIDTH // M_HEADS
CONV_W = 4
MLSTM_CHUNK = 64
N_Q = TOK_WIDTH // HEAD_DIM
N_KV = 4
GROUP = N_Q // N_KV
KV_WIDTH = N_KV * HEAD_DIM
CMP_STRIDE = 16
CMP_LEN = 2 * CMP_STRIDE
SLC_BLOCK = 64
N_SEL = 16
WINDOW = 512
Q_BLOCK = 128
D_FF = 2816
N_EXPERTS = 8
TOP_K = 2
E_FF = 7 * D_MODEL // 4

N_A = (DEPTH + 1) // 2
N_B = DEPTH // 2
A_IN = 4 * TOK_WIDTH + 2 * M_HEADS + X_WIDTH
B_IN = TOK_WIDTH + 6 * KV_WIDTH + 3 * N_Q + X_WIDTH
RMS_EPS = 1e-6
NEG = -1e30
BIG = 1e9

kernel_name = 'hybrid_mlstm_nsa_memory_decoder_step'


def rmsnorm(x, g):
    xf = x.astype(jnp.float32)
    return xf * lax.rsqrt(jnp.mean(xf * xf, axis=-1, keepdims=True) + RMS_EPS) * g.astype(jnp.float32)


def masked_softmax(s, mask):
    s = jnp.where(mask, s, NEG)
    e = jnp.where(mask, jnp.exp(s - jnp.max(s, axis=-1, keepdims=True)), 0.0)
    return e / jnp.maximum(jnp.sum(e, axis=-1, keepdims=True), 1e-30)


def alibi_slopes():
    return jnp.exp2(-8.0 * jnp.arange(1, N_Q + 1, dtype=jnp.float32) / N_Q)


def swiglu(u, w_gu, w_d):
    g, v = jnp.split(u @ w_gu, 2, axis=-1)
    return (jax.nn.silu(g) * v) @ w_d


def moe_swiglu(u, w_router, w_gu, w_d):
    logits = (u @ w_router).astype(jnp.float32)
    top_v, top_i = lax.top_k(logits, TOP_K)
    gates = jax.nn.softmax(top_v, axis=-1)
    comb = jnp.sum(jax.nn.one_hot(top_i, N_EXPERTS, dtype=jnp.float32) * gates[..., None], axis=-2)
    out = 0.0
    for e in range(N_EXPERTS):
        out = out + comb[..., e:e + 1] * swiglu(u, w_gu[e], w_d[e])
    return out


def mem_kv(mem, g, w):
    b, n, _ = mem.shape
    kv = (rmsnorm(mem, g) @ w).reshape(b, n, 2, X_HEADS, HEAD_DIM)
    return kv[:, :, 0], kv[:, :, 1]


def mem_attention(q, mk, mv):
    b, t = q.shape[:2]
    s = jnp.einsum('bthd,bnhd->bhtn', q, mk.astype(jnp.float32)) * HEAD_DIM ** -0.5
    p = jax.nn.softmax(s, axis=-1)
    return jnp.einsum('bhtn,bnhd->bthd', p, mv.astype(jnp.float32)).reshape(b, t, X_WIDTH)


def causal_conv(hist, w):
    t = hist.shape[1] - (CONV_W - 1)
    return sum(hist[:, j:j + t] * w[j] for j in range(CONV_W))


def mlstm_chunked(q, k, v, ig, lf, C0, n0, m0):
    b, t, h, d = q.shape
    L = MLSTM_CHUNK if t % MLSTM_CHUNK == 0 else t
    nc = t // L

    def chunks(a):
        return jnp.moveaxis(a.reshape((b, nc, L) + a.shape[2:]), 1, 0)

    causal = jnp.tril(jnp.ones((L, L), dtype=bool))[None, :, :, None]

    def step(carry, inp):
        C, n, m = carry
        qc, kc, vc, ic, fc = inp
        F = jnp.cumsum(fc, axis=1)
        a = F + m[:, None, :]
        D = jnp.where(causal, F[:, :, None, :] - F[:, None, :, :] + ic[:, None, :, :], NEG)
        mt = jnp.maximum(a, jnp.max(D, axis=2))
        W = jnp.exp(D - mt[:, :, None, :])
        g = jnp.exp(a - mt)
        qk = jnp.einsum('bthd,bshd->btsh', qc, kc) * W
        num = jnp.einsum('btsh,bshd->bthd', qk, vc) + g[..., None] * jnp.einsum('bhvk,bthk->bthv', C, qc)
        den = jnp.sum(qk, axis=2) + g * jnp.einsum('bhk,bthk->bth', n, qc)
        hc = num / jnp.maximum(jnp.abs(den), jnp.exp(-mt))[..., None]
        wl, gl = W[:, -1], g[:, -1]
        C = gl[..., None, None] * C + jnp.einsum('bsh,bshv,bshk->bhvk', wl, vc, kc)
        n = gl[..., None] * n + jnp.einsum('bsh,bshk->bhk', wl, kc)
        return (C, n, mt[:, -1]), hc

    (C, n, m), hs = lax.scan(step, (C0, n0, m0), (chunks(q), chunks(k), chunks(v), chunks(ig), chunks(lf)))
    return jnp.moveaxis(hs, 0, 1).reshape(b, t, h, d), C, n, m


def mlstm_mixer(z, conv_hist, C0, n0, m0, conv_w, b_gate, head_g):
    b, t, _ = z.shape
    qk_pre, v, o_pre, g_pre, xq = jnp.split(
        z, [2 * TOK_WIDTH, 3 * TOK_WIDTH, 4 * TOK_WIDTH, 4 * TOK_WIDTH + 2 * M_HEADS], axis=-1)
    hist = jnp.concatenate([conv_hist.astype(jnp.float32), qk_pre], axis=1)
    q, k = jnp.split(jax.nn.silu(causal_conv(hist, conv_w.astype(jnp.float32))), 2, axis=-1)
    shp = (b, t, M_HEADS, M_HEAD_DIM)
    ig, fg = jnp.split(g_pre + b_gate, 2, axis=-1)
    h, C, n, m = mlstm_chunked(q.reshape(shp), k.reshape(shp) * M_HEAD_DIM ** -0.5, v.reshape(shp),
                               ig, jax.nn.log_sigmoid(fg), C0.astype(jnp.float32),
                               n0.astype(jnp.float32), m0.astype(jnp.float32))
    h = jax.nn.sigmoid(o_pre).reshape(shp) * h
    h = h * lax.rsqrt(jnp.mean(h * h, axis=-1, keepdims=True) + RMS_EPS) * head_g.reshape(M_HEADS, M_HEAD_DIM)
    new_state = (C, n, m, hist[:, t:])
    return h.reshape(b, t, TOK_WIDTH), xq.reshape(b, t, X_HEADS, HEAD_DIM), new_state


def nsa_split(z, b_gate):
    b, t, _ = z.shape
    q, kv, g, xq = jnp.split(z, [TOK_WIDTH, TOK_WIDTH + 6 * KV_WIDTH, TOK_WIDTH + 6 * KV_WIDTH + 3 * N_Q], axis=-1)
    return (q.reshape(b, t, N_Q, HEAD_DIM), kv.reshape(b, t, 6, N_KV, HEAD_DIM),
            jax.nn.sigmoid(g + b_gate).reshape(b, t, 3, N_Q), xq.reshape(b, t, X_HEADS, HEAD_DIM))


def nsa_merge(gates, o):
    b, t = gates.shape[:2]
    return jnp.einsum('btcn,btcnd->btnd', gates, o).reshape(b, t, TOK_WIDTH)


def compress_rows(a, pe, w1, w2):
    n = a.shape[0] // CMP_STRIDE
    seg = a.reshape(n, CMP_STRIDE, N_KV, HEAD_DIM)
    nxt = jnp.concatenate([seg[1:], jnp.zeros_like(seg[:1])], axis=0)
    blk = jnp.concatenate([seg, nxt], axis=1) + pe[None, :, None, :]
    flat = blk.transpose(0, 2, 1, 3).reshape(n, N_KV, CMP_LEN * HEAD_DIM)
    return jax.nn.gelu(flat @ w1) @ w2


def nsa_sequence(q, q_start, rows, win, w_c1, w_c2, pe, slopes):
    tk, tq = rows.shape[0], q.shape[0]
    k_cmp = compress_rows(rows[:, 0], pe[0], w_c1[0], w_c2[0])
    v_cmp = compress_rows(rows[:, 1], pe[1], w_c1[1], w_c2[1])
    n_cmp, n_slc = tk // CMP_STRIDE, tk // SLC_BLOCK
    n_top = min(N_SEL, n_slc)
    cmp_start = jnp.arange(n_cmp) * CMP_STRIDE
    cmp_end = cmp_start + CMP_LEN - 1
    slc_start = jnp.arange(n_slc) * SLC_BLOCK
    overlap = ((cmp_start[:, None] < slc_start[None, :] + SLC_BLOCK)
               & (cmp_end[:, None] >= slc_start[None, :])).astype(jnp.float32)
    k_blk = rows[:, 2].reshape(n_slc, SLC_BLOCK, N_KV, HEAD_DIM).transpose(2, 0, 1, 3)
    v_blk = rows[:, 3].reshape(n_slc, SLC_BLOCK, N_KV, HEAD_DIM).transpose(2, 0, 1, 3)
    sl = slopes.reshape(N_KV, GROUP)[None, :, :, None]
    scale = HEAD_DIM ** -0.5
    qb = Q_BLOCK if tq % Q_BLOCK == 0 else tq
    n_qb = tq // qb
    qg = q.reshape(n_qb, qb, N_KV, GROUP, HEAD_DIM)
    kv_idx = jnp.arange(N_KV)[None, :, None]
    blk_id = jnp.arange(n_slc)[None, :]

    def block(args):
        j, qj = args
        t = q_start + j * qb + jnp.arange(qb)
        tb = t[:, None, None, None]
        s_c = jnp.einsum('tgrd,ngd->tgrn', qj, k_cmp) * scale - sl * (tb - cmp_end)
        p_c = masked_softmax(s_c, (cmp_end[None, :] <= t[:, None])[:, None, None, :])
        o_c = jnp.einsum('tgrn,ngd->tgrd', p_c, v_cmp)
        imp = jnp.einsum('tgrn,nj->tgj', p_c, overlap)
        cur = (t // SLC_BLOCK)[:, None]
        forced = (blk_id == 0) | (blk_id == cur) | (blk_id == cur - 1)
        future = slc_start[None, :] > t[:, None]
        imp = jnp.where(forced[:, None], BIG, jnp.where(future[:, None], -BIG, imp))
        _, idx = lax.top_k(imp, n_top)
        k_sel = k_blk[kv_idx, idx].reshape(qb, N_KV, n_top * SLC_BLOCK, HEAD_DIM)
        v_sel = v_blk[kv_idx, idx].reshape(qb, N_KV, n_top * SLC_BLOCK, HEAD_DIM)
        pos = (idx[..., None] * SLC_BLOCK + jnp.arange(SLC_BLOCK)).reshape(qb, N_KV, 1, n_top * SLC_BLOCK)
        s_s = jnp.einsum('tgrd,tgmd->tgrm', qj, k_sel) * scale - sl * (tb - pos)
        p_s = masked_softmax(s_s, pos <= tb)
        o_s = jnp.einsum('tgrm,tgmd->tgrd', p_s, v_sel)
        w_rows = lax.dynamic_slice_in_dim(win, j * qb, WINDOW + qb, axis=0)
        wpos = q_start - WINDOW + j * qb + jnp.arange(WINDOW + qb)
        dist = t[:, None] - wpos[None, :]
        s_w = jnp.einsum('tgrd,wgd->tgrw', qj, w_rows[:, 0]) * scale - sl * dist[:, None, None, :]
        mask_w = (dist >= 0) & (dist <= WINDOW) & (wpos >= 0)[None, :]
        p_w = masked_softmax(s_w, mask_w[:, None, None, :])
        o_w = jnp.einsum('tgrw,wgd->tgrd', p_w, w_rows[:, 1])
        return jnp.stack([o_c, o_s, o_w], axis=1)

    out = lax.map(block, (jnp.arange(n_qb), qg))
    return out.reshape(tq, 3, N_Q, HEAD_DIM)


def nsa_prompt(q, kv, w_c1, w_c2, pe, slopes):
    def one(args):
        qs, kvs = args
        win = jnp.concatenate([jnp.zeros((WINDOW, 2, N_KV, HEAD_DIM), kvs.dtype), kvs[:, 4:]], axis=0)
        return nsa_sequence(qs, 0, kvs[:, :4], win, w_c1, w_c2, pe, slopes)
    return lax.map(one, (q, kv))


def nsa_sample(q, kv, pool, layer, page_table, win_buf, past_len, w_c1, w_c2, pe, slopes):
    tq = q.shape[1]
    total = past_len + tq
    tk = -(-total // SLC_BLOCK) * SLC_BLOCK
    wbuf = win_buf.shape[1]

    def one(args):
        qs, kvs, pages, buf = args
        past = pool[layer, pages].reshape(past_len, 4, N_KV, HEAD_DIM).astype(jnp.float32)
        rows = jnp.concatenate([past, kvs[:, :4], jnp.zeros((tk - total, 4, N_KV, HEAD_DIM), jnp.float32)], axis=0)
        win = jnp.concatenate([jnp.zeros((WINDOW - wbuf, 2, N_KV, HEAD_DIM), jnp.float32),
                               buf.astype(jnp.float32), kvs[:, 4:]], axis=0)
        return nsa_sequence(qs, past_len, rows, win, w_c1, w_c2, pe, slopes)
    return lax.map(one, (q, kv, page_table, win_buf))


def setup_inputs(seed: int = 0) -> dict:
    key = jax.random.key(seed)
    keys = iter(jax.random.split(key, 48))

    def nrm(shape, scale):
        return scale * jax.random.normal(next(keys), shape, jnp.float32)

    def gain(shape):
        return 1.0 + nrm(shape, 0.02)

    n_pages = PAST_LEN // PAGE_SIZE
    n_used = DEC_BATCH * n_pages
    n_pool = n_used + n_used // 4
    wbuf = min(WINDOW, PAST_LEN)
    inp = {}
    inp['x_prompt'] = nrm((BATCH, SEQ, D_MODEL), 1.0)
    inp['x_sample'] = nrm((DEC_BATCH, DEC_SEQ, D_MODEL), 1.0)
    inp['cache_mem_k'] = nrm((DEPTH, DEC_BATCH, N_MEM, X_HEADS, HEAD_DIM), 1.0)
    inp['cache_mem_v'] = nrm((DEPTH, DEC_BATCH, N_MEM, X_HEADS, HEAD_DIM), 1.0)
    inp['state_mlstm_C'] = nrm((N_A, DEC_BATCH, M_HEADS, M_HEAD_DIM, M_HEAD_DIM), 0.1)
    inp['state_mlstm_n'] = nrm((N_A, DEC_BATCH, M_HEADS, M_HEAD_DIM), 0.1)
    inp['state_mlstm_m'] = nrm((N_A, DEC_BATCH, M_HEADS), 1.0)
    inp['state_mlstm_conv'] = nrm((N_A, DEC_BATCH, CONV_W - 1, 2 * TOK_WIDTH), 1.0)
    inp['cache_nsa'] = nrm((N_B, n_pool, PAGE_SIZE, 4, N_KV, HEAD_DIM), 1.0)
    inp['cache_nsa_win'] = nrm((N_B, DEC_BATCH, wbuf, 2, N_KV, HEAD_DIM), 1.0)
    inp['page_table'] = jax.random.permutation(next(keys), n_pool)[:n_used].reshape(DEC_BATCH, n_pages).astype(jnp.int32)
    inp['mem_prompt'] = nrm((BATCH, N_MEM, D_MODEL), 1.0)
    inp['norm_mix'] = gain((DEPTH, D_MODEL))
    inp['norm_mem'] = gain((DEPTH, D_MODEL))
    inp['norm_ffn'] = gain((DEPTH, D_MODEL))
    inp['norm_final'] = gain((D_MODEL,))
    inp['w_mem_kv'] = nrm((DEPTH, D_MODEL, 2 * X_WIDTH), D_MODEL ** -0.5)
    inp['w_in_a'] = nrm((N_A, D_MODEL, A_IN), D_MODEL ** -0.5)
    inp['b_gate_a'] = jnp.concatenate([nrm((N_A, M_HEADS), 0.1), 3.0 + nrm((N_A, M_HEADS), 0.5)], axis=-1)
    inp['conv_a'] = nrm((N_A, CONV_W, 2 * TOK_WIDTH), CONV_W ** -0.5)
    inp['head_norm_a'] = gain((N_A, TOK_WIDTH))
    inp['w_in_b'] = nrm((N_B, D_MODEL, B_IN), D_MODEL ** -0.5)
    inp['b_gate_b'] = nrm((N_B, 3 * N_Q), 0.1)
    inp['w_cmp1'] = nrm((N_B, 2, CMP_LEN * HEAD_DIM, HEAD_DIM), (CMP_LEN * HEAD_DIM) ** -0.5)
    inp['w_cmp2'] = nrm((N_B, 2, HEAD_DIM, HEAD_DIM), HEAD_DIM ** -0.5)
    inp['pe_cmp'] = nrm((N_B, 2, CMP_LEN, HEAD_DIM), 0.1)
    inp['w_out'] = nrm((DEPTH, MIX_WIDTH, D_MODEL), MIX_WIDTH ** -0.5)
    inp['w_ffn_gu'] = nrm((N_A, D_MODEL, 2 * D_FF), D_MODEL ** -0.5)
    inp['w_ffn_d'] = nrm((N_A, D_FF, D_MODEL), D_FF ** -0.5)
    inp['w_router'] = nrm((N_B, D_MODEL, N_EXPERTS), D_MODEL ** -0.5)
    inp['w_moe_gu'] = nrm((N_B, N_EXPERTS, D_MODEL, 2 * E_FF), D_MODEL ** -0.5)
    inp['w_moe_d'] = nrm((N_B, N_EXPERTS, E_FF, D_MODEL), E_FF ** -0.5)
    return inp


def reference(x_prompt, x_sample, cache_mem_k, cache_mem_v, state_mlstm_C, state_mlstm_n,
              state_mlstm_m, state_mlstm_conv, cache_nsa, cache_nsa_win, page_table, mem_prompt,
              norm_mix, norm_mem, norm_ffn, norm_final, w_mem_kv, w_in_a, b_gate_a, conv_a,
              head_norm_a, w_in_b, b_gate_b, w_cmp1, w_cmp2, pe_cmp, w_out, w_ffn_gu, w_ffn_d,
              w_router, w_moe_gu, w_moe_d):
    f32 = jnp.float32
    bp, seq = x_prompt.shape[0], x_prompt.shape[1]
    past_len = page_table.shape[1] * cache_nsa.shape[2]
    wbuf = cache_nsa_win.shape[2]
    slopes = alibi_slopes()
    xp = x_prompt.astype(f32)
    xs = x_sample.astype(f32)
    mem_k_p, mem_v_p = [], []
    C_p, n_p, m_p, conv_p, rows_p, win_p = [], [], [], [], [], []
    C_s, n_s, m_s, conv_s, rows_s, win_s = [], [], [], [], [], []
    for i in range(DEPTH):
        li = i // 2
        up = rmsnorm(xp, norm_mix[i])
        us = rmsnorm(xs, norm_mix[i])
        mk, mv = mem_kv(mem_prompt, norm_mem[i], w_mem_kv[i])
        mem_k_p.append(mk)
        mem_v_p.append(mv)
        if i % 2 == 0:
            par = (conv_a[li], b_gate_a[li], head_norm_a[li])
            hp, qxp, st_p = mlstm_mixer(
                up @ w_in_a[li], jnp.zeros((bp, CONV_W - 1, 2 * TOK_WIDTH), f32),
                jnp.zeros((bp, M_HEADS, M_HEAD_DIM, M_HEAD_DIM), f32),
                jnp.zeros((bp, M_HEADS, M_HEAD_DIM), f32), jnp.zeros((bp, M_HEADS), f32), *par)
            hs, qxs, st_s = mlstm_mixer(
                us @ w_in_a[li], state_mlstm_conv[li], state_mlstm_C[li], state_mlstm_n[li],
                state_mlstm_m[li], *par)
            C_p.append(st_p[0]); n_p.append(st_p[1]); m_p.append(st_p[2]); conv_p.append(st_p[3])
            C_s.append(st_s[0]); n_s.append(st_s[1]); m_s.append(st_s[2]); conv_s.append(st_s[3])
        else:
            par = (w_cmp1[li], w_cmp2[li], pe_cmp[li], slopes)
            qp, kvp, gp, qxp = nsa_split(up @ w_in_b[li], b_gate_b[li])
            qs, kvs, gs, qxs = nsa_split(us @ w_in_b[li], b_gate_b[li])
            hp = nsa_merge(gp, nsa_prompt(qp, kvp, *par))
            hs = nsa_merge(gs, nsa_sample(qs, kvs, cache_nsa, li, page_table, cache_nsa_win[li], past_len, *par))
            rows_p.append(kvp[:, :, :4])
            win_p.append(kvp[:, seq - min(WINDOW, seq):, 4:])
            rows_s.append(kvs[:, :, :4])
            win_s.append(jnp.concatenate([cache_nsa_win[li].astype(f32), kvs[:, :, 4:]], axis=1)[:, -wbuf:])
        xp = xp + jnp.concatenate([hp, mem_attention(qxp, mk, mv)], axis=-1) @ w_out[i]
        xs = xs + jnp.concatenate([hs, mem_attention(qxs, cache_mem_k[i], cache_mem_v[i])], axis=-1) @ w_out[i]
        if i % 2 == 0:
            xp = xp + swiglu(rmsnorm(xp, norm_ffn[i]), w_ffn_gu[li], w_ffn_d[li])
            xs = xs + swiglu(rmsnorm(xs, norm_ffn[i]), w_ffn_gu[li], w_ffn_d[li])
        else:
            xp = xp + moe_swiglu(rmsnorm(xp, norm_ffn[i]), w_router[li], w_moe_gu[li], w_moe_d[li])
            xs = xs + moe_swiglu(rmsnorm(xs, norm_ffn[i]), w_router[li], w_moe_gu[li], w_moe_d[li])
    y_prompt = rmsnorm(xp, norm_final)
    y_sample = rmsnorm(xs, norm_final)
    return (y_prompt, y_sample,
            jnp.stack(mem_k_p), jnp.stack(mem_v_p),
            jnp.stack(C_p), jnp.stack(n_p), jnp.stack(m_p), jnp.stack(conv_p),
            jnp.stack(rows_p), jnp.stack(win_p),
            jnp.stack(C_s), jnp.stack(n_s), jnp.stack(m_s), jnp.stack(conv_s),
            jnp.stack(rows_s), jnp.stack(win_s))
```

```python
import functools

import jax
import jax.numpy as jnp
from jax import lax
from jax.experimental import pallas as pl
from jax.experimental.pallas import tpu as pltpu

F32 = jnp.float32
BF16 = jnp.bfloat16
HIGHEST = lax.Precision.HIGHEST

D_MODEL = 1024
HEAD_DIM = 64
X_HEADS = 4
X_WIDTH = X_HEADS * HEAD_DIM
TOK_WIDTH = D_MODEL - X_WIDTH
M_HEADS = 8
M_HEAD_DIM = TOK_WIDTH // M_HEADS
M_HEAD_PAD = 128
M_PAD_WIDTH = M_HEADS * M_HEAD_PAD
CONV_W = 4
N_Q = TOK_WIDTH // HEAD_DIM
N_KV = 4
GROUP = N_Q // N_KV
KV_WIDTH = N_KV * HEAD_DIM
CMP_STRIDE = 16
CMP_LEN = 2 * CMP_STRIDE
SLC_BLOCK = 64
N_SEL = 16
WINDOW = 512
Q_BLOCK = 128
D_FF = 2816
N_EXPERTS = 8
TOP_K = 2
E_FF = 7 * D_MODEL // 4
RMS_EPS = 1e-6
NEG = -1e30
BIG = 1e9

MLSTM_CHUNK = 128
VMEM_LIMIT = 56 * 1024 * 1024


def _cparams(sem):
    return pltpu.CompilerParams(dimension_semantics=sem, vmem_limit_bytes=VMEM_LIMIT)


def _rms(x, g):
    return x * lax.rsqrt(jnp.mean(x * x, axis=-1, keepdims=True) + RMS_EPS) * g


def _dot(a, b):
    return jnp.dot(a, b, preferred_element_type=F32)


def _dot_nt(a, b):
    return lax.dot_general(a, b, (((1,), (1,)), ((), ())), preferred_element_type=F32)


def _dot_tn(a, b):
    return lax.dot_general(a, b, (((0,), (0,)), ((), ())), preferred_element_type=F32)


def _norm_proj_body(x_ref, g_ref, *refs, n_out, hi):
    w_refs, o_refs = refs[:n_out], refs[n_out:]
    u = _rms(x_ref[...], g_ref[...])
    ub = u.astype(BF16)
    for w_ref, o_ref, h in zip(w_refs, o_refs, hi):
        if h:
            o_ref[...] = jnp.dot(u, w_ref[...], precision=HIGHEST, preferred_element_type=F32)
        else:
            o_ref[...] = _dot(ub, w_ref[...])


def norm_proj(x, gamma, weights, hi):
    m, d = x.shape
    tm = min(512, m)
    n_out = len(weights)
    in_specs = [pl.BlockSpec((tm, d), lambda i: (i, 0)), pl.BlockSpec((1, d), lambda i: (0, 0))]
    in_specs += [pl.BlockSpec(w.shape, lambda i: (0, 0)) for w in weights]
    out_specs = [pl.BlockSpec((tm, w.shape[1]), lambda i: (i, 0)) for w in weights]
    out_shape = [jax.ShapeDtypeStruct((m, w.shape[1]), F32) for w in weights]
    return pl.pallas_call(
        functools.partial(_norm_proj_body, n_out=n_out, hi=tuple(hi)),
        grid=(m // tm,), in_specs=in_specs, out_specs=out_specs, out_shape=out_shape,
        compiler_params=_cparams(("parallel",)), name="norm_proj",
    )(x, gamma.reshape(1, d), *weights)


def _mem_attn_body(q_ref, k_ref, v_ref, o_ref):
    q = q_ref[...] * (HEAD_DIM ** -0.5)
    rows = q.shape[0]
    if rows < 8:
        q = jnp.broadcast_to(q[0:1], (8, X_WIDTH))
    kb = k_ref[...].astype(BF16)
    vb = v_ref[...].astype(BF16)
    head = lax.broadcasted_iota(jnp.int32, (1, X_WIDTH), 1) // HEAD_DIM
    out = jnp.zeros(q.shape, F32)
    for h in range(X_HEADS):
        qh = jnp.where(head == h, q, 0.0).astype(BF16)
        s = _dot_nt(qh, kb)
        p = jnp.exp(s - jnp.max(s, axis=-1, keepdims=True))
        o = _dot(p.astype(BF16), vb) / jnp.sum(p, axis=-1, keepdims=True)
        out = jnp.where(head == h, o, out)
    o_ref[...] = out[:rows]


def mem_attention(q, mk, mv):
    b, t, w = q.shape
    n = mk.shape[1]
    tq = min(512, t)
    return pl.pallas_call(
        _mem_attn_body, grid=(b, t // tq),
        in_specs=[pl.BlockSpec((None, tq, w), lambda i, j: (i, j, 0)),
                  pl.BlockSpec((None, n, w), lambda i, j: (i, 0, 0)),
                  pl.BlockSpec((None, n, w), lambda i, j: (i, 0, 0))],
        out_specs=pl.BlockSpec((None, tq, w), lambda i, j: (i, j, 0)),
        out_shape=jax.ShapeDtypeStruct((b, t, w), F32),
        compiler_params=_cparams(("parallel", "parallel")), name="mem_attention",
    )(q, mk, mv)


def _out_proj_body(x_ref, h_ref, a_ref, wh_ref, wa_ref, o_ref):
    o_ref[...] = (x_ref[...] + _dot(h_ref[...].astype(BF16), wh_ref[...])
                  + _dot(a_ref[...].astype(BF16), wa_ref[...]))


def out_proj(x, h, a, wh, wa):
    m, d = x.shape
    tm = min(512, m)
    kh, ka = h.shape[1], a.shape[1]
    return pl.pallas_call(
        _out_proj_body, grid=(m // tm,),
        in_specs=[pl.BlockSpec((tm, d), lambda i: (i, 0)),
                  pl.BlockSpec((tm, kh), lambda i: (i, 0)),
                  pl.BlockSpec((tm, ka), lambda i: (i, 0)),
                  pl.BlockSpec((kh, d), lambda i: (0, 0)),
                  pl.BlockSpec((ka, d), lambda i: (0, 0))],
        out_specs=pl.BlockSpec((tm, d), lambda i: (i, 0)),
        out_shape=jax.ShapeDtypeStruct((m, d), F32),
        compiler_params=_cparams(("parallel",)), name="out_proj",
    )(x, h, a, wh, wa)


def _ffn_body(x_ref, g_ref, wg_ref, wu_ref, wd_ref, o_ref, u_sc):
    @pl.when(pl.program_id(1) == 0)
    def _():
        x = x_ref[...]
        u_sc[...] = _rms(x, g_ref[...]).astype(BF16)
        o_ref[...] = x

    u = u_sc[...]
    hg = _dot(u, wg_ref[...])
    hu = _dot(u, wu_ref[...])
    hm = (hg * jax.nn.sigmoid(hg) * hu).astype(BF16)
    o_ref[...] += _dot(hm, wd_ref[...])


def ffn(x, gamma, w_gu, w_d):
    m, d = x.shape
    f = w_d.shape[0]
    tm = min(1024, m)
    tf = 256
    nf = f // tf
    return pl.pallas_call(
        _ffn_body, grid=(m // tm, nf),
        in_specs=[pl.BlockSpec((tm, d), lambda i, j: (i, 0)),
                  pl.BlockSpec((1, d), lambda i, j: (0, 0)),
                  pl.BlockSpec((d, tf), lambda i, j: (0, j)),
                  pl.BlockSpec((d, tf), lambda i, j: (0, nf + j)),
                  pl.BlockSpec((tf, d), lambda i, j: (j, 0))],
        out_specs=pl.BlockSpec((tm, d), lambda i, j: (i, 0)),
        out_shape=jax.ShapeDtypeStruct((m, d), F32),
        scratch_shapes=[pltpu.VMEM((tm, d), BF16)],
        compiler_params=_cparams(("parallel", "arbitrary")), name="ffn",
    )(x, gamma.reshape(1, d), w_gu, w_gu, w_d)


def _router_body(x_ref, g_ref, wr_ref, u_ref, comb_ref, rank_ref, cnt_ref):
    tm = x_ref.shape[0]
    u = _rms(x_ref[...], g_ref[...])
    u_ref[...] = u.astype(BF16)
    logits = jnp.dot(u, wr_ref[...], precision=HIGHEST, preferred_element_type=F32)
    lane = lax.broadcasted_iota(jnp.int32, (tm, 128), 1)
    logits = jnp.where(lane < N_EXPERTS, logits, -jnp.inf)
    m1 = jnp.max(logits, axis=-1, keepdims=True)
    i1 = jnp.min(jnp.where(logits == m1, lane, 128), axis=-1, keepdims=True)
    rest = jnp.where(lane == i1, -jnp.inf, logits)
    m2 = jnp.max(rest, axis=-1, keepdims=True)
    i2 = jnp.min(jnp.where(rest == m2, lane, 128), axis=-1, keepdims=True)
    e2 = jnp.exp(m2 - m1)
    g1 = 1.0 / (1.0 + e2)
    g2 = e2 / (1.0 + e2)
    comb_ref[...] = jnp.where(lane == i1, g1, 0.0) + jnp.where(lane == i2, g2, 0.0)
    sel = ((lane == i1) | (lane == i2)).astype(F32)
    r = lax.broadcasted_iota(jnp.int32, (tm, tm), 0)
    c = lax.broadcasted_iota(jnp.int32, (tm, tm), 1)
    tril = (c < r).astype(BF16)
    rank = _dot(tril, sel.astype(BF16))
    rank_ref[...] = jnp.where(sel > 0, rank, -1.0)
    cnt_ref[...] = jnp.sum(sel, axis=0, keepdims=True).astype(jnp.int32)


def router(x, gamma, w_router_pad, tm):
    m, d = x.shape
    nt = m // tm
    return pl.pallas_call(
        _router_body, grid=(nt,),
        in_specs=[pl.BlockSpec((tm, d), lambda i: (i, 0)),
                  pl.BlockSpec((1, d), lambda i: (0, 0)),
                  pl.BlockSpec((d, 128), lambda i: (0, 0))],
        out_specs=[pl.BlockSpec((tm, d), lambda i: (i, 0)),
                   pl.BlockSpec((tm, 128), lambda i: (i, 0)),
                   pl.BlockSpec((tm, 128), lambda i: (i, 0)),
                   pl.BlockSpec((None, 1, 128), lambda i: (i, 0, 0))],
        out_shape=[jax.ShapeDtypeStruct((m, d), BF16),
                   jax.ShapeDtypeStruct((m, 128), F32),
                   jax.ShapeDtypeStruct((m, 128), F32),
                   jax.ShapeDtypeStruct((nt, 1, 128), jnp.int32)],
        compiler_params=_cparams(("parallel",)), name="router",
    )(x, gamma.reshape(1, d), w_router_pad)


def _moe_body(cnt_ref, u_ref, rcol_ref, rrow_ref, gcol_ref, wg_ref, wu_ref, wd_ref, o_ref, *, cap):
    i = pl.program_id(0)
    e = pl.program_id(1)
    tm = u_ref.shape[0]

    @pl.when(e == 0)
    def _():
        o_ref[...] = jnp.zeros(o_ref.shape, F32)

    cnt = cnt_ref[i * N_EXPERTS + e]
    n_chunks = (cnt + cap - 1) // cap

    def chunk(c, carry):
        base = (c * cap).astype(F32)
        slot_r = lax.broadcasted_iota(jnp.int32, (cap, tm), 0).astype(F32) + base
        gather = (rrow_ref[...] == slot_r).astype(BF16)
        xg = _dot(gather, u_ref[...]).astype(BF16)
        hg = _dot(xg, wg_ref[...])
        hu = _dot(xg, wu_ref[...])
        hm = (hg * jax.nn.sigmoid(hg) * hu).astype(BF16)
        y = _dot(hm, wd_ref[...]).astype(BF16)
        slot_c = lax.broadcasted_iota(jnp.int32, (tm, cap), 1).astype(F32) + base
        scatter = (rcol_ref[...] == slot_c).astype(BF16)
        o_ref[...] += gcol_ref[...] * _dot(scatter, y)
        return carry

    lax.fori_loop(0, n_chunks, chunk, 0)


def moe(u, comb, rank, counts, w_gu, w_d, tm):
    m, d = u.shape
    nt = m // tm
    cap = min(256, tm)
    ef = w_d.shape[1]
    comb_e = comb[:, :N_EXPERTS].T
    rank_e = rank[:, :N_EXPERTS].T
    rcol = rank_e.reshape(N_EXPERTS, m, 1)
    rrow = rank_e.reshape(N_EXPERTS, 1, m)
    gcol = comb_e.reshape(N_EXPERTS, m, 1)
    cnt = counts[:, 0, :N_EXPERTS].reshape(nt * N_EXPERTS)
    grid_spec = pltpu.PrefetchScalarGridSpec(
        num_scalar_prefetch=1, grid=(nt, N_EXPERTS),
        in_specs=[pl.BlockSpec((tm, d), lambda i, e, c: (i, 0)),
                  pl.BlockSpec((None, tm, 1), lambda i, e, c: (e, i, 0)),
                  pl.BlockSpec((None, 1, tm), lambda i, e, c: (e, 0, i)),
                  pl.BlockSpec((None, tm, 1), lambda i, e, c: (e, i, 0)),
                  pl.BlockSpec((None, d, ef), lambda i, e, c: (e, 0, 0)),
                  pl.BlockSpec((None, d, ef), lambda i, e, c: (e, 0, 1)),
                  pl.BlockSpec((None, ef, d), lambda i, e, c: (e, 0, 0))],
        out_specs=pl.BlockSpec((tm, d), lambda i, e, c: (i, 0)))
    return pl.pallas_call(
        functools.partial(_moe_body, cap=cap), grid_spec=grid_spec,
        out_shape=jax.ShapeDtypeStruct((m, d), F32),
        compiler_params=_cparams(("parallel", "arbitrary")), name="moe",
    )(cnt, u, rcol, rrow, gcol, w_gu, w_gu, w_d)


def _final_body(x_ref, y_ref, g_ref, o_ref):
    o_ref[...] = _rms(x_ref[...] + y_ref[...], g_ref[...])


def final_norm(x, y, gamma):
    m, d = x.shape
    tm = min(1024, m)
    return pl.pallas_call(
        _final_body, grid=(m // tm,),
        in_specs=[pl.BlockSpec((tm, d), lambda i: (i, 0)),
                  pl.BlockSpec((tm, d), lambda i: (i, 0)),
                  pl.BlockSpec((1, d), lambda i: (0, 0))],
        out_specs=pl.BlockSpec((tm, d), lambda i: (i, 0)),
        out_shape=jax.ShapeDtypeStruct((m, d), F32),
        compiler_params=_cparams(("parallel",)), name="final_norm",
    )(x, y, gamma.reshape(1, d))


def _log_sigmoid(x):
    return jnp.minimum(x, 0.0) - jnp.log(1.0 + jnp.exp(-jnp.abs(x)))


def _silu(x):
    return x * jax.nn.sigmoid(x)


def _mlstm_body(qp_ref, kp_ref, v_ref, o_ref, g_ref, gt_ref, cwq_ref, cwk_ref, bgr_ref, bgc_ref, hg_ref,
                h_ref, c_out, n_out, m_out, c_sc, n_sc, m_sc, tq_sc, tk_sc):
    j = pl.program_id(1)
    L = qp_ref.shape[0]
    P = M_HEAD_PAD

    @pl.when(j == 0)
    def _():
        c_sc[...] = jnp.zeros(c_sc.shape, F32)
        n_sc[...] = jnp.zeros(n_sc.shape, F32)
        m_sc[...] = jnp.zeros(m_sc.shape, F32)
        tq_sc[...] = jnp.zeros(tq_sc.shape, F32)
        tk_sc[...] = jnp.zeros(tk_sc.shape, F32)

    def conv(x_ref, tail_sc, cw_ref):
        x = x_ref[...]
        ext = jnp.concatenate([tail_sc[...], x], axis=0)
        y = cw_ref[3:4, :] * x
        for tap in range(CONV_W - 1):
            lo = 8 - (CONV_W - 1) + tap
            y = y + cw_ref[tap:tap + 1, :] * ext[lo:lo + L]
        tail_sc[...] = x[L - 8:L]
        return _silu(y)

    qc = conv(qp_ref, tq_sc, cwq_ref)
    kc = conv(kp_ref, tk_sc, cwk_ref) * (M_HEAD_DIM ** -0.5)

    g = g_ref[...] + bgr_ref[...]
    gt = gt_ref[...] + bgc_ref[...]
    ig_c, lf_c = g[:, :M_HEADS], _log_sigmoid(g[:, M_HEADS:])
    ig_r, lf_r = gt[:M_HEADS], _log_sigmoid(gt[M_HEADS:])
    row = lax.broadcasted_iota(jnp.int32, (L, L), 0)
    col = lax.broadcasted_iota(jnp.int32, (L, L), 1)
    causal = col <= row
    f_c = jnp.dot(causal.astype(F32), lf_c, precision=HIGHEST, preferred_element_type=F32)
    f_r = jnp.dot(lf_r, (row <= col).astype(F32), precision=HIGHEST, preferred_element_type=F32)

    for h in range(M_HEADS):
        sl = slice(h * P, (h + 1) * P)
        q, k, v = qc[:, sl], kc[:, sl], v_ref[:, sl]
        fc, fr = f_c[:, h:h + 1], f_r[h:h + 1, :]
        m_prev = m_sc[h:h + 1, 0:1]
        d = jnp.where(causal, fc - fr + ig_r[h:h + 1, :], NEG)
        a = fc + m_prev
        mt = jnp.maximum(a, jnp.max(d, axis=1, keepdims=True))
        wm = jnp.exp(d - mt)
        gcol = jnp.exp(a - mt)
        qb, kb, vb = q.astype(BF16), k.astype(BF16), v.astype(BF16)
        s = _dot_nt(qb, kb) * wm
        c_old = c_sc[h]
        n_old = n_sc[h:h + 1, :]
        num = _dot(s.astype(BF16), vb) + gcol * _dot_nt(qb, c_old.astype(BF16))
        den = jnp.sum(s, axis=1, keepdims=True) + gcol * jnp.sum(q * n_old, axis=1, keepdims=True)
        hc = num / jnp.maximum(jnp.abs(den), jnp.exp(-mt))
        og = jax.nn.sigmoid(o_ref[:, sl]) * hc
        ms = jnp.sum(og * og, axis=1, keepdims=True) * (1.0 / M_HEAD_DIM)
        h_ref[:, sl] = og * lax.rsqrt(ms + RMS_EPS) * hg_ref[:, sl]
        mt_last, gl = mt[L - 1:L], gcol[L - 1:L]
        wl = jnp.exp(fc[L - 1:L] - fc + ig_c[:, h:h + 1] - mt_last)
        c_sc[h] = gl * c_old + _dot_tn((v * wl).astype(BF16), kb)
        n_sc[h:h + 1, :] = gl * n_old + jnp.sum(wl * k, axis=0, keepdims=True)
        m_sc[h:h + 1, :] = jnp.broadcast_to(mt_last, (1, P))

    @pl.when(j == pl.num_programs(1) - 1)
    def _():
        for h in range(M_HEADS):
            c_out[h] = c_sc[h][:M_HEAD_DIM, :M_HEAD_DIM]
        n_out[...] = n_sc[:, :M_HEAD_DIM]
        m_out[...] = m_sc[...]


def mlstm_prompt(qp, kp, v, o, g, cwq, cwk, b_gate, head_g):
    b, t, w = qp.shape
    L = min(MLSTM_CHUNK, t)
    gt = jnp.swapaxes(g, 1, 2)
    seq = pl.BlockSpec((None, L, w), lambda i, j: (i, j, 0))
    const = lambda shape: pl.BlockSpec(shape, lambda i, j: (0,) * len(shape))
    return pl.pallas_call(
        _mlstm_body, grid=(b, t // L),
        in_specs=[seq, seq, seq, seq,
                  pl.BlockSpec((None, L, 2 * M_HEADS), lambda i, j: (i, j, 0)),
                  pl.BlockSpec((None, 2 * M_HEADS, L), lambda i, j: (i, 0, j)),
                  const((CONV_W, w)), const((CONV_W, w)), const((1, 2 * M_HEADS)), const((2 * M_HEADS, 1)),
                  const((1, w))],
        out_specs=[seq,
                   pl.BlockSpec((None, M_HEADS, M_HEAD_DIM, M_HEAD_DIM), lambda i, j: (i, 0, 0, 0)),
                   pl.BlockSpec((None, M_HEADS, M_HEAD_DIM), lambda i, j: (i, 0, 0)),
                   pl.BlockSpec((None, M_HEADS, M_HEAD_PAD), lambda i, j: (i, 0, 0))],
        out_shape=[jax.ShapeDtypeStruct((b, t, w), F32),
                   jax.ShapeDtypeStruct((b, M_HEADS, M_HEAD_DIM, M_HEAD_DIM), F32),
                   jax.ShapeDtypeStruct((b, M_HEADS, M_HEAD_DIM), F32),
                   jax.ShapeDtypeStruct((b, M_HEADS, M_HEAD_PAD), F32)],
        scratch_shapes=[pltpu.VMEM((M_HEADS, M_HEAD_PAD, M_HEAD_PAD), F32),
                        pltpu.VMEM((M_HEADS, M_HEAD_PAD), F32),
                        pltpu.VMEM((M_HEADS, M_HEAD_PAD), F32),
                        pltpu.VMEM((8, w), F32), pltpu.VMEM((8, w), F32)],
        compiler_params=_cparams(("parallel", "arbitrary")), name="mlstm_prompt",
    )(qp, kp, v, o, g, gt, cwq, cwk, b_gate.reshape(1, -1), b_gate.reshape(-1, 1), head_g)


def _mlstm_step_body(qp_ref, kp_ref, v_ref, o_ref, g_ref, hq_ref, hk_ref, c0_ref, n0_ref, m0_ref,
                     cwq_ref, cwk_ref, bgr_ref, hg_ref, h_ref, c_out, n_out, m_out):
    P, dh = M_HEAD_PAD, M_HEAD_DIM

    def conv(x_ref, hist_ref, cw_ref):
        y = cw_ref[3:4, :] * x_ref[...]
        for tap in range(CONV_W - 1):
            y = y + cw_ref[tap:tap + 1, :] * hist_ref[tap:tap + 1, :]
        return _silu(y)

    qc = conv(qp_ref, hq_ref, cwq_ref)
    kc = conv(kp_ref, hk_ref, cwk_ref) * (dh ** -0.5)
    g = g_ref[...] + bgr_ref[...]
    m0 = m0_ref[...]
    for h in range(M_HEADS):
        lo = h * P
        q, k, v = qc[:, lo:lo + dh], kc[:, lo:lo + dh], v_ref[:, lo:lo + dh]
        ig = g[:, h:h + 1]
        lf = _log_sigmoid(g[:, M_HEADS + h:M_HEADS + h + 1])
        a = lf + m0[h:h + 1, :]
        mt = jnp.maximum(a, ig)
        wv = jnp.exp(ig - mt)
        gv = jnp.exp(a - mt)
        qf, kf = q.astype(BF16).astype(F32), k.astype(BF16).astype(F32)
        qk = jnp.sum(qf * kf, axis=1, keepdims=True) * wv
        c_old, n_old = c0_ref[h], n0_ref[h:h + 1, :]
        q8 = jnp.broadcast_to(q, (8, dh)).astype(BF16)
        cq = _dot_nt(q8, c_old.astype(BF16))[0:1]
        num = qk * v + gv * cq
        den = qk + gv * jnp.sum(q * n_old, axis=1, keepdims=True)
        hc = num / jnp.maximum(jnp.abs(den), jnp.exp(-mt))
        og = jax.nn.sigmoid(o_ref[:, lo:lo + dh]) * hc
        ms = jnp.sum(og * og, axis=1, keepdims=True) * (1.0 / dh)
        hn = og * lax.rsqrt(ms + RMS_EPS) * hg_ref[:, lo:lo + dh]
        h_ref[:, lo:lo + P] = jnp.concatenate([hn, jnp.zeros((1, P - dh), F32)], axis=1)
        v_col = jnp.transpose(jnp.broadcast_to(v_ref[:, lo:lo + P], (8, P)))[:dh, 0:1]
        c_out[h] = gv * c_old + wv * (v_col * k)
        n_out[h:h + 1, :] = gv * n_old + wv * k
        m_out[h:h + 1, :] = jnp.broadcast_to(mt, (1, P))


def mlstm_step(qp, kp, v, o, g, hist_q, hist_k, c0, n0, m0, cwq, cwk, b_gate, head_g):
    b, _, w = qp.shape
    row = pl.BlockSpec((None, 1, w), lambda i: (i, 0, 0))
    const = lambda shape: pl.BlockSpec(shape, lambda i: (0,) * len(shape))
    cspec = pl.BlockSpec((None, M_HEADS, M_HEAD_DIM, M_HEAD_DIM), lambda i: (i, 0, 0, 0))
    nspec = pl.BlockSpec((None, M_HEADS, M_HEAD_DIM), lambda i: (i, 0, 0))
    return pl.pallas_call(
        _mlstm_step_body, grid=(b,),
        in_specs=[row, row, row, row,
                  pl.BlockSpec((None, 1, 2 * M_HEADS), lambda i: (i, 0, 0)),
                  pl.BlockSpec((None, CONV_W - 1, w), lambda i: (i, 0, 0)),
                  pl.BlockSpec((None, CONV_W - 1, w), lambda i: (i, 0, 0)),
                  cspec, nspec,
                  pl.BlockSpec((None, M_HEADS, 1), lambda i: (i, 0, 0)),
                  const((CONV_W, w)), const((CONV_W, w)), const((1, 2 * M_HEADS)), const((1, w))],
        out_specs=[row, cspec, nspec,
                   pl.BlockSpec((None, M_HEADS, M_HEAD_PAD), lambda i: (i, 0, 0))],
        out_shape=[jax.ShapeDtypeStruct((b, 1, w), F32),
                   jax.ShapeDtypeStruct(c0.shape, F32),
                   jax.ShapeDtypeStruct(n0.shape, F32),
                   jax.ShapeDtypeStruct((b, M_HEADS, M_HEAD_PAD), F32)],
        compiler_params=_cparams(("parallel",)), name="mlstm_step",
    )(qp, kp, v, o, g, hist_q, hist_k, c0, n0, m0.reshape(b, M_HEADS, 1), cwq, cwk,
      b_gate.reshape(1, -1), head_g)


def _pad_heads(a):
    lead = a.shape[:-1]
    a = a.reshape(lead + (M_HEADS, M_HEAD_DIM))
    a = jnp.pad(a, [(0, 0)] * len(lead) + [(0, 0), (0, M_HEAD_PAD - M_HEAD_DIM)])
    return a.reshape(lead + (M_PAD_WIDTH,))


def _unpad_heads(a):
    lead = a.shape[:-1]
    return a.reshape(lead + (M_HEADS, M_HEAD_PAD))[..., :M_HEAD_DIM].reshape(lead + (TOK_WIDTH,))


def _gelu_tanh(x):
    return x * (0.5 * (1.0 + jnp.tanh(0.7978845608028654 * (x + 0.044715 * (x * x * x)))))


def _compress_tail(u, v_next, pe_ref, w1_ref, w2_ref):
    bias = jnp.zeros((8, KV_WIDTH), F32)
    for p in range(CMP_LEN):
        pe8 = jnp.broadcast_to(pe_ref[p:p + 1, :], (8, KV_WIDTH)).astype(BF16)
        bias = bias + _dot(pe8, w1_ref[p])
    pre = u + v_next + bias[0:1]
    return _dot(_gelu_tanh(pre).astype(BF16), w2_ref[...])


PERM_ROWS = CMP_STRIDE * CMP_STRIDE


def _regroup_rows(x, ybuf, grp):
    r = lax.broadcasted_iota(jnp.int32, (PERM_ROWS, PERM_ROWS), 0)
    c = lax.broadcasted_iota(jnp.int32, (PERM_ROWS, PERM_ROWS), 1)
    perm = (c == (r % CMP_STRIDE) * CMP_STRIDE + r // CMP_STRIDE).astype(BF16)
    y = _dot(perm, x).astype(BF16)
    for p in range(CMP_STRIDE):
        ybuf[p, pl.ds(pl.multiple_of(grp * CMP_STRIDE, CMP_STRIDE), CMP_STRIDE), :] = (
            y[p * CMP_STRIDE:(p + 1) * CMP_STRIDE])


def _compress_from(ybuf, w1_ref, w2_ref, pe_ref, o_ref):
    nb = ybuf.shape[1]
    u = jnp.zeros((nb, KV_WIDTH), F32)
    v = jnp.zeros((nb, KV_WIDTH), F32)
    for p in range(CMP_STRIDE):
        x = ybuf[p]
        u = u + _dot(x, w1_ref[p])
        v = v + _dot(x, w1_ref[CMP_STRIDE + p])
    v_next = jnp.concatenate([v[1:], jnp.zeros((1, KV_WIDTH), F32)], axis=0)
    o_ref[...] = _compress_tail(u, v_next, pe_ref, w1_ref, w2_ref)


def _compress_body(rows_ref, w1_ref, w2_ref, pe_ref, o_ref, ybuf):
    for grp in range(rows_ref.shape[0] // PERM_ROWS):
        _regroup_rows(rows_ref[grp * PERM_ROWS:(grp + 1) * PERM_ROWS, :].astype(BF16), ybuf, grp)
    _compress_from(ybuf, w1_ref, w2_ref, pe_ref, o_ref)


def compress_prompt(kv, w1bd, w2bd, pe4):
    b, t, _ = kv.shape
    nb = t // CMP_STRIDE
    return pl.pallas_call(
        _compress_body, grid=(b, 2),
        in_specs=[pl.BlockSpec((None, t, KV_WIDTH), lambda i, c: (i, 0, c)),
                  pl.BlockSpec((None, CMP_LEN, KV_WIDTH, KV_WIDTH), lambda i, c: (c, 0, 0, 0)),
                  pl.BlockSpec((None, KV_WIDTH, KV_WIDTH), lambda i, c: (c, 0, 0)),
                  pl.BlockSpec((None, CMP_LEN, KV_WIDTH), lambda i, c: (c, 0, 0))],
        out_specs=pl.BlockSpec((None, None, nb, KV_WIDTH), lambda i, c: (i, c, 0, 0)),
        out_shape=jax.ShapeDtypeStruct((b, 2, nb, KV_WIDTH), F32),
        scratch_shapes=[pltpu.VMEM((CMP_STRIDE, nb, KV_WIDTH), BF16)],
        compiler_params=_cparams(("parallel", "parallel")), name="compress_prompt",
    )(kv, w1bd, w2bd, pe4)


def _alibi_slopes():
    return [2.0 ** (-8.0 * (n + 1) / N_Q) for n in range(N_Q)]


SEL_CHUNK = 256


def _nsa_body(q_ref, ks_ref, vs_ref, kw_ref, vw_ref, cmp_ref, gp_ref, bg_ref, o_ref,
              acc_sc, m_sc, l_sc, sel_sc, *, n_top, n_slc):
    qb = pl.program_id(1)
    QB, CK = Q_BLOCK, SEL_CHUNK
    rows = GROUP * QB
    n_cmp = cmp_ref.shape[1]
    slopes = _alibi_slopes()
    lane_head = lax.broadcasted_iota(jnp.int32, (1, KV_WIDTH), 1) // HEAD_DIM
    t1 = qb * QB + lax.broadcasted_iota(jnp.int32, (QB, 1), 0)
    t_col = jnp.concatenate([t1] * GROUP, axis=0)
    t_row = qb * QB + lax.broadcasted_iota(jnp.int32, (1, QB), 1)
    gates = jax.nn.sigmoid(gp_ref[...] + bg_ref[...])
    kc = cmp_ref[0].astype(BF16)
    vc = cmp_ref[1].astype(BF16)
    cmp_end = lax.broadcasted_iota(jnp.int32, (1, n_cmp), 1) * CMP_STRIDE + (CMP_LEN - 1)
    oj = lax.broadcasted_iota(jnp.int32, (128, n_cmp), 0)
    on = lax.broadcasted_iota(jnp.int32, (128, n_cmp), 1) * CMP_STRIDE
    overlap_t = ((on < oj * SLC_BLOCK + SLC_BLOCK) & (on + (CMP_LEN - 1) >= oj * SLC_BLOCK)
                 & (oj < n_slc)).astype(F32)
    blk = lax.broadcasted_iota(jnp.int32, (n_slc, 1), 0)
    cur = t_row // SLC_BLOCK
    forced = (blk == 0) | (blk == cur) | (blk == cur - 1)
    future = blk * SLC_BLOCK > t_row
    sel_sc[...] = jnp.zeros(sel_sc.shape, F32)

    def reset():
        m_sc[...] = jnp.full(m_sc.shape, NEG, F32)
        l_sc[...] = jnp.zeros(l_sc.shape, F32)
        acc_sc[...] = jnp.zeros(acc_sc.shape, F32)

    out_r = [jnp.zeros((QB, KV_WIDTH), F32) for _ in range(GROUP)]
    for g in range(N_KV):
        gm = lane_head == g
        qe = jnp.concatenate(
            [jnp.where(gm, q_ref[:, r * KV_WIDTH:(r + 1) * KV_WIDTH], 0.0) for r in range(GROUP)],
            axis=0) * (HEAD_DIM ** -0.5)
        qe = qe.astype(BF16)
        slope_col = jnp.concatenate(
            [jnp.full((QB, 1), slopes[GROUP * g + r], F32) for r in range(GROUP)], axis=0)

        def update(k, v, mask, pos_f):
            s = _dot_nt(qe, k.astype(BF16)) + slope_col * pos_f
            s = jnp.where(mask, s, NEG)
            m_old = m_sc[...]
            m_new = jnp.maximum(m_old, jnp.max(s, axis=-1, keepdims=True))
            alpha = jnp.exp(m_old - m_new)
            p = jnp.where(mask, jnp.exp(s - m_new), 0.0)
            l_sc[...] = alpha * l_sc[...] + jnp.sum(p, axis=-1, keepdims=True)
            acc_sc[...] = alpha * acc_sc[...] + _dot(p.astype(BF16), v.astype(BF16))
            m_sc[...] = m_new

        mask_c = cmp_end <= t_col
        s = _dot_nt(qe, kc) + slope_col * cmp_end.astype(F32)
        s = jnp.where(mask_c, s, NEG)
        e = jnp.where(mask_c, jnp.exp(s - jnp.max(s, axis=-1, keepdims=True)), 0.0)
        p_c = e / jnp.maximum(jnp.sum(e, axis=-1, keepdims=True), 1e-30)
        o_c = _dot(p_c.astype(BF16), vc)
        p_sum = p_c[0:QB] + p_c[QB:2 * QB] + p_c[2 * QB:3 * QB]

        imp = lax.dot_general(overlap_t, p_sum, (((1,), (1,)), ((), ())), precision=HIGHEST,
                              preferred_element_type=F32)[:n_slc]
        imp = jnp.where(forced, BIG, jnp.where(future, -BIG, imp))
        for j in range(n_slc):
            row_j = imp[j:j + 1, :]
            beats = (imp > row_j) | ((imp == row_j) & (blk < j))
            rank = jnp.sum(beats.astype(F32), axis=0, keepdims=True)
            sel_sc[j:j + 1, :] = (rank < n_top).astype(F32)
        sel = jnp.transpose(sel_sc[...]).astype(BF16)

        reset()

        def sel_chunk(c, carry):
            start = pl.multiple_of(c * CK, CK)
            pos = c * CK + lax.broadcasted_iota(jnp.int32, (1, CK), 1)
            ej = lax.broadcasted_iota(jnp.int32, (128, CK), 0)
            eb = c * (CK // SLC_BLOCK) + lax.broadcasted_iota(jnp.int32, (128, CK), 1) // SLC_BLOCK
            expand = (ej == eb).astype(BF16)
            picked = _dot(sel, expand)
            picked = jnp.concatenate([picked] * GROUP, axis=0)
            mask = (picked > 0.5) & (pos <= t_col)
            update(ks_ref[pl.ds(start, CK), :], vs_ref[pl.ds(start, CK), :], mask, pos.astype(F32))
            return carry

        lax.fori_loop(0, (qb * QB + QB + CK - 1) // CK, sel_chunk, 0)
        o_s = acc_sc[...] / jnp.maximum(l_sc[...], 1e-30)

        reset()
        for i in range(WINDOW // QB + 1):
            cw = qb - WINDOW // QB + i
            start = pl.multiple_of(jnp.maximum(cw, 0) * QB, QB)
            pos = cw * QB + lax.broadcasted_iota(jnp.int32, (1, QB), 1)
            dist = t_col - pos
            mask = (dist >= 0) & (dist <= WINDOW) & (pos >= 0)
            update(kw_ref[pl.ds(start, QB), :], vw_ref[pl.ds(start, QB), :], mask, pos.astype(F32))
        o_w = acc_sc[...] / jnp.maximum(l_sc[...], 1e-30)

        tot = jnp.zeros((rows, KV_WIDTH), F32)
        for c, o in enumerate((o_c, o_s, o_w)):
            gcol = jnp.concatenate(
                [gates[:, c * N_Q + GROUP * g + r:c * N_Q + GROUP * g + r + 1] for r in range(GROUP)], axis=0)
            tot = tot + gcol * o
        for r in range(GROUP):
            out_r[r] = jnp.where(gm, tot[r * QB:(r + 1) * QB], out_r[r])

    o_ref[...] = jnp.concatenate(out_r, axis=1)


def nsa_prompt(q, kv, cmp, g_pre, b_gate):
    b, t, _ = q.shape
    n_slc = t // SLC_BLOCK
    n_top = min(N_SEL, n_slc)
    full = lambda c: pl.BlockSpec((None, t, KV_WIDTH), lambda i, j: (i, 0, c))
    rows = GROUP * Q_BLOCK
    return pl.pallas_call(
        functools.partial(_nsa_body, n_top=n_top, n_slc=n_slc), grid=(b, t // Q_BLOCK),
        in_specs=[pl.BlockSpec((None, Q_BLOCK, TOK_WIDTH), lambda i, j: (i, j, 0)),
                  full(2), full(3), full(4), full(5),
                  pl.BlockSpec((None, 2, t // CMP_STRIDE, KV_WIDTH), lambda i, j: (i, 0, 0, 0)),
                  pl.BlockSpec((None, Q_BLOCK, 3 * N_Q), lambda i, j: (i, j, 0)),
                  pl.BlockSpec((1, 3 * N_Q), lambda i, j: (0, 0))],
        out_specs=pl.BlockSpec((None, Q_BLOCK, TOK_WIDTH), lambda i, j: (i, j, 0)),
        out_shape=jax.ShapeDtypeStruct((b, t, TOK_WIDTH), F32),
        scratch_shapes=[pltpu.VMEM((rows, KV_WIDTH), F32), pltpu.VMEM((rows, 1), F32),
                        pltpu.VMEM((rows, 1), F32), pltpu.VMEM((128, Q_BLOCK), F32)],
        compiler_params=_cparams(("parallel", "arbitrary")), name="nsa_prompt",
    )(q, kv, kv, kv, kv, cmp, g_pre, b_gate.reshape(1, -1))


def _compress_paged_body(pt_ref, pool_ref, w1_ref, w2_ref, pe_ref, o_ref, raw, ybuf, sems,
                         *, layer, n_pages):
    page = raw.shape[0] // n_pages
    pages_per_grp = PERM_ROWS // page
    n_grp = n_pages // pages_per_grp
    step = pl.program_id(0) * 2 + pl.program_id(1)
    n_steps = pl.num_programs(0) * 2

    def page_copy(stp, p):
        seq, typ = stp // 2, stp % 2
        return pltpu.make_async_copy(
            pool_ref.at[layer, pt_ref[seq * n_pages + p], :, pl.ds(typ * KV_WIDTH, KV_WIDTH)],
            raw.at[pl.ds(p * page, page), :], sems.at[p // pages_per_grp])

    def start_all(stp):
        def body(p, carry):
            page_copy(stp, p).start()
            return carry
        lax.fori_loop(0, n_pages, body, 0)

    @pl.when(step == 0)
    def _():
        start_all(step)

    def regroup(grp, carry):
        for k in range(pages_per_grp):
            page_copy(step, grp * pages_per_grp + k).wait()
        start = pl.multiple_of(grp * PERM_ROWS, PERM_ROWS)
        _regroup_rows(raw[pl.ds(start, PERM_ROWS), :].astype(BF16), ybuf, grp)
        return carry

    lax.fori_loop(0, n_grp, regroup, 0)

    @pl.when(step + 1 < n_steps)
    def _():
        start_all(step + 1)

    _compress_from(ybuf, w1_ref, w2_ref, pe_ref, o_ref)


def compress_paged(pool, page_table, layer, w1bd, w2bd, pe4):
    b, n_pages = page_table.shape
    page = pool.shape[2]
    assert PERM_ROWS % page == 0 and n_pages % (PERM_ROWS // page) == 0
    nb = n_pages * page // CMP_STRIDE
    grid_spec = pltpu.PrefetchScalarGridSpec(
        num_scalar_prefetch=1, grid=(b, 2),
        in_specs=[pl.BlockSpec(memory_space=pl.ANY),
                  pl.BlockSpec((None, CMP_LEN, KV_WIDTH, KV_WIDTH), lambda i, c, pt: (c, 0, 0, 0)),
                  pl.BlockSpec((None, KV_WIDTH, KV_WIDTH), lambda i, c, pt: (c, 0, 0)),
                  pl.BlockSpec((None, CMP_LEN, KV_WIDTH), lambda i, c, pt: (c, 0, 0))],
        out_specs=pl.BlockSpec((None, None, nb, KV_WIDTH), lambda i, c, pt: (i, c, 0, 0)),
        scratch_shapes=[pltpu.VMEM((n_pages * page, KV_WIDTH), F32),
                        pltpu.VMEM((CMP_STRIDE, nb, KV_WIDTH), BF16),
                        pltpu.SemaphoreType.DMA((n_pages * page // PERM_ROWS,))])
    return pl.pallas_call(
        functools.partial(_compress_paged_body, layer=layer, n_pages=n_pages), grid_spec=grid_spec,
        out_shape=jax.ShapeDtypeStruct((b, 2, nb, KV_WIDTH), F32),
        compiler_params=_cparams(("arbitrary", "arbitrary")), name="compress_paged",
    )(page_table.reshape(-1), pool, w1bd, w2bd, pe4)


def _decode_queries(q_ref):
    lane_head = lax.broadcasted_iota(jnp.int32, (1, KV_WIDTH), 1) // HEAD_DIM
    row4 = lax.broadcasted_iota(jnp.int32, (N_KV, 1), 0)
    parts = []
    for r in range(GROUP):
        q_r = jnp.broadcast_to(q_ref[:, r * KV_WIDTH:(r + 1) * KV_WIDTH], (N_KV, KV_WIDTH))
        parts.append(jnp.where(lane_head == row4, q_r, 0.0))
    parts.append(jnp.zeros((16 - N_Q, KV_WIDTH), F32))
    qe = (jnp.concatenate(parts, axis=0) * (HEAD_DIM ** -0.5)).astype(BF16)
    row = lax.broadcasted_iota(jnp.int32, (16, 1), 0)
    slopes = _alibi_slopes()
    slope_col = jnp.zeros((16, 1), F32)
    for r in range(GROUP):
        for g in range(N_KV):
            slope_col = jnp.where(row == N_KV * r + g, slopes[GROUP * g + r], slope_col)
    return qe, slope_col, row % N_KV, lane_head


def _nsa_select_body(q_ref, cmp_ref, oc_ref, idx_ref, *, t_pos, n_slc, n_top):
    nc = cmp_ref.shape[1]
    nj = idx_ref.shape[1]
    qe, slope_col, _, _ = _decode_queries(q_ref)
    kc = cmp_ref[0].astype(BF16)
    vc = cmp_ref[1].astype(BF16)
    cmp_end = lax.broadcasted_iota(jnp.int32, (1, nc), 1) * CMP_STRIDE + (CMP_LEN - 1)
    mask = cmp_end <= t_pos
    s = _dot_nt(qe, kc) + slope_col * cmp_end.astype(F32)
    s = jnp.where(mask, s, NEG)
    e = jnp.where(mask, jnp.exp(s - jnp.max(s, axis=-1, keepdims=True)), 0.0)
    p = e / jnp.maximum(jnp.sum(e, axis=-1, keepdims=True), 1e-30)
    oc_ref[...] = _dot(p.astype(BF16), vc)
    p_sum = p[0:N_KV] + p[N_KV:2 * N_KV] + p[2 * N_KV:3 * N_KV]
    p_sum = jnp.concatenate([p_sum, jnp.zeros((8 - N_KV, nc), F32)], axis=0)
    on = lax.broadcasted_iota(jnp.int32, (nc, nj), 0) * CMP_STRIDE
    oj = lax.broadcasted_iota(jnp.int32, (nc, nj), 1) * SLC_BLOCK
    overlap = ((on < oj + SLC_BLOCK) & (on + (CMP_LEN - 1) >= oj)).astype(F32)
    imp = jnp.dot(p_sum, overlap, precision=HIGHEST, preferred_element_type=F32)
    j = lax.broadcasted_iota(jnp.int32, (8, nj), 1)
    cur = t_pos // SLC_BLOCK
    forced = (j == 0) | (j == cur) | (j == cur - 1)
    imp = jnp.where(forced, BIG, jnp.where(j * SLC_BLOCK > t_pos, -BIG, imp))
    imp = jnp.where(j < n_slc, imp, -jnp.inf)
    picked = jnp.zeros((8, nj), jnp.int32)
    for i in range(n_top):
        mx = jnp.max(imp, axis=-1, keepdims=True)
        arg = jnp.min(jnp.where(imp == mx, j, nj), axis=-1, keepdims=True)
        picked = jnp.where(j == i, arg, picked)
        imp = jnp.where(j == arg, -jnp.inf, imp)
    idx_ref[...] = picked


def nsa_select(q, cmp, t_pos, n_slc, n_top):
    b = q.shape[0]
    nc = cmp.shape[2]
    nj = -(-n_slc // 128) * 128
    return pl.pallas_call(
        functools.partial(_nsa_select_body, t_pos=t_pos, n_slc=n_slc, n_top=n_top), grid=(b,),
        in_specs=[pl.BlockSpec((None, 1, TOK_WIDTH), lambda i: (i, 0, 0)),
                  pl.BlockSpec((None, 2, nc, KV_WIDTH), lambda i: (i, 0, 0, 0))],
        out_specs=[pl.BlockSpec((None, 16, KV_WIDTH), lambda i: (i, 0, 0)),
                   pl.BlockSpec((None, 8, nj), lambda i: (i, 0, 0))],
        out_shape=[jax.ShapeDtypeStruct((b, 16, KV_WIDTH), F32),
                   jax.ShapeDtypeStruct((b, 8, nj), jnp.int32)],
        compiler_params=_cparams(("parallel",)), name="nsa_select",
    )(q, cmp)


def _nsa_decode_body(pt_ref, idx_ref, q_ref, kvn_ref, win_ref, oc_ref, gp_ref, bg_ref, pool_ref, o_ref,
                     blkbuf, sem, *, layer, t_pos, n_top, n_pages, per_page):
    n_past_blocks = n_pages * per_page
    b = pl.program_id(0)

    def blk_copy(k):
        bid = jnp.minimum(idx_ref[b * N_KV * n_top + k], n_past_blocks - 1)
        return pltpu.make_async_copy(
            pool_ref.at[layer, pt_ref[b * n_pages + bid // per_page],
                        pl.ds((bid % per_page) * SLC_BLOCK, SLC_BLOCK), pl.ds(2 * KV_WIDTH, 2 * KV_WIDTH)],
            blkbuf.at[k], sem.at[0])

    for k in range(N_KV * n_top):
        blk_copy(k).start()

    qe, slope_col, row_group, lane_head = _decode_queries(q_ref)
    qf = qe.astype(F32)
    kvn = kvn_ref[...]
    new = lambda c: kvn[:, c * KV_WIDTH:(c + 1) * KV_WIDTH]
    t_f = float(t_pos)

    def attend(k, v, mask, pos_f, k_new, v_new):
        s = _dot_nt(qe, k.astype(BF16)) + slope_col * pos_f
        s = jnp.where(mask, s, NEG)
        s_new = (jnp.sum(qf * k_new.astype(BF16).astype(F32), axis=-1, keepdims=True) + slope_col * t_f)
        m = jnp.maximum(jnp.max(s, axis=-1, keepdims=True), s_new)
        p = jnp.where(mask, jnp.exp(s - m), 0.0)
        p_new = jnp.exp(s_new - m)
        l = jnp.sum(p, axis=-1, keepdims=True) + p_new
        return (_dot(p.astype(BF16), v.astype(BF16)) + p_new * v_new) / l

    wbuf = win_ref.shape[0]
    pos_w = (t_pos - wbuf) + lax.broadcasted_iota(jnp.int32, (1, wbuf), 1)
    mask_w = (pos_w >= 0) & (t_pos - pos_w <= WINDOW)
    o_w = attend(win_ref[:, :KV_WIDTH], win_ref[:, KV_WIDTH:], mask_w, pos_w.astype(F32), new(4), new(5))

    for k in range(N_KV * n_top):
        blk_copy(k).wait()

    o_s = jnp.zeros((16, KV_WIDTH), F32)
    key = lax.broadcasted_iota(jnp.int32, (1, n_top * SLC_BLOCK), 1)
    for g in range(N_KV):
        pos = jnp.zeros((1, n_top * SLC_BLOCK), jnp.int32)
        for i in range(n_top):
            bid = idx_ref[(b * N_KV + g) * n_top + i]
            base = jnp.where(bid < n_past_blocks, bid * SLC_BLOCK, t_pos + 1)
            pos = jnp.where(key // SLC_BLOCK == i, base + key % SLC_BLOCK, pos)
        blocks = blkbuf[g * n_top:(g + 1) * n_top].reshape(n_top * SLC_BLOCK, 2 * KV_WIDTH)
        o_g = attend(blocks[:, :KV_WIDTH], blocks[:, KV_WIDTH:], pos <= t_pos, pos.astype(F32), new(2), new(3))
        o_s = jnp.where(row_group == g, o_g, o_s)

    gates = jax.nn.sigmoid(gp_ref[...] + bg_ref[...])
    row = lax.broadcasted_iota(jnp.int32, (16, 1), 0)
    tot = jnp.zeros((16, KV_WIDTH), F32)
    for c, o in enumerate((oc_ref[...], o_s, o_w)):
        gcol = jnp.zeros((16, 1), F32)
        for r in range(GROUP):
            for g in range(N_KV):
                lane = c * N_Q + GROUP * g + r
                gcol = jnp.where(row == N_KV * r + g, gates[:, lane:lane + 1], gcol)
        tot = tot + gcol * o
    out = []
    for r in range(GROUP):
        h_r = jnp.zeros((1, KV_WIDTH), F32)
        for g in range(N_KV):
            h_r = jnp.where(lane_head == g, tot[N_KV * r + g:N_KV * r + g + 1], h_r)
        out.append(h_r)
    o_ref[...] = jnp.concatenate(out, axis=1)


def nsa_decode(pool, page_table, layer, idx, q, kv_new, win, o_c, g_pre, b_gate, t_pos, n_top):
    b, n_pages = page_table.shape
    page = pool.shape[2]
    per_page = page // SLC_BLOCK
    wbuf = win.shape[1]
    row = lambda w: pl.BlockSpec((None, 1, w), lambda s, pt, ix: (s, 0, 0))
    grid_spec = pltpu.PrefetchScalarGridSpec(
        num_scalar_prefetch=2, grid=(b,),
        in_specs=[row(TOK_WIDTH), row(6 * KV_WIDTH),
                  pl.BlockSpec((None, wbuf, 2 * KV_WIDTH), lambda s, pt, ix: (s, 0, 0)),
                  pl.BlockSpec((None, 16, KV_WIDTH), lambda s, pt, ix: (s, 0, 0)),
                  row(3 * N_Q),
                  pl.BlockSpec((1, 3 * N_Q), lambda s, pt, ix: (0, 0)),
                  pl.BlockSpec(memory_space=pl.ANY)],
        out_specs=row(TOK_WIDTH),
        scratch_shapes=[pltpu.VMEM((N_KV * n_top, SLC_BLOCK, 2 * KV_WIDTH), F32),
                        pltpu.SemaphoreType.DMA((1,))])
    idx_flat = idx[:, :N_KV, :n_top].reshape(-1)
    return pl.pallas_call(
        functools.partial(_nsa_decode_body, layer=layer, t_pos=t_pos, n_top=n_top, n_pages=n_pages,
                          per_page=per_page),
        grid_spec=grid_spec, out_shape=jax.ShapeDtypeStruct((b, 1, TOK_WIDTH), F32),
        compiler_params=_cparams(("arbitrary",)), name="nsa_decode",
    )(page_table.reshape(-1), idx_flat, q, kv_new, win, o_c, g_pre, b_gate.reshape(1, -1), pool)


def _rgd_perm():
    idx = []
    for r in range(GROUP):
        for g in range(N_KV):
            n = GROUP * g + r
            idx.extend(range(n * HEAD_DIM, (n + 1) * HEAD_DIM))
    return jnp.array(idx, jnp.int32)


def _cols_to_rgd(w):
    lead = w.shape[:-1]
    w = w.reshape(lead + (N_KV, GROUP, HEAD_DIM))
    return jnp.swapaxes(w, -3, -2).reshape(lead + (TOK_WIDTH,))


def _block_diag4(w):
    eye = jnp.eye(N_KV, dtype=w.dtype)
    out = jnp.einsum('ab,...ij->...aibj', eye, w)
    return out.reshape(w.shape[:-2] + (KV_WIDTH, KV_WIDTH))


MOE_TILE = 1024


def kernel(x_prompt, x_sample, cache_mem_k, cache_mem_v, state_mlstm_C, state_mlstm_n, state_mlstm_m,
           state_mlstm_conv, cache_nsa, cache_nsa_win, page_table, mem_prompt, norm_mix, norm_mem, norm_ffn,
           norm_final, w_mem_kv, w_in_a, b_gate_a, conv_a, head_norm_a, w_in_b, b_gate_b, w_cmp1, w_cmp2,
           pe_cmp, w_out, w_ffn_gu, w_ffn_d, w_router, w_moe_gu, w_moe_d):
    bp, seq, d = x_prompt.shape
    bs, dseq, _ = x_sample.shape
    assert dseq == 1 and norm_mix.shape[0] == 2 and w_in_a.shape[0] == 1 and w_in_b.shape[0] == 1
    n_mem = mem_prompt.shape[1]
    bf = lambda a: a.astype(BF16)
    xp = x_prompt.reshape(bp * seq, d).astype(F32)
    xs = x_sample.reshape(bs, d).astype(F32)
    mem = mem_prompt.reshape(bp * n_mem, d).astype(F32)

    def mem_kv(i):
        w = w_mem_kv[i]
        mk, mv = norm_proj(mem, norm_mem[i], [bf(w[:, :X_WIDTH]), bf(w[:, X_WIDTH:])], [False, False])
        return mk.reshape(bp, n_mem, X_WIDTH), mv.reshape(bp, n_mem, X_WIDTH)

    def mix_out(i, x, h, a, wh):
        return out_proj(x, h.reshape(x.shape[0], -1), a.reshape(x.shape[0], -1), wh, bf(w_out[i][TOK_WIDTH:]))

    wa = w_in_a[0]
    t3, t4 = 3 * TOK_WIDTH, 4 * TOK_WIDTH
    w0 = [bf(_pad_heads(wa[:, :TOK_WIDTH])), bf(_pad_heads(wa[:, TOK_WIDTH:2 * TOK_WIDTH])),
          bf(_pad_heads(wa[:, 2 * TOK_WIDTH:t3])), bf(_pad_heads(wa[:, t3:t4])),
          bf(wa[:, t4 + 2 * M_HEADS:]), wa[:, t4:t4 + 2 * M_HEADS]]
    hi0 = [False] * 5 + [True]
    cwq = _pad_heads(conv_a[0][:, :TOK_WIDTH])
    cwk = _pad_heads(conv_a[0][:, TOK_WIDTH:])
    head_g = _pad_heads(head_norm_a[0]).reshape(1, M_PAD_WIDTH)
    wh0 = w_out[0][:TOK_WIDTH].reshape(M_HEADS, M_HEAD_DIM, d)
    wh0 = bf(jnp.pad(wh0, ((0, 0), (0, M_HEAD_PAD - M_HEAD_DIM), (0, 0))).reshape(M_PAD_WIDTH, d))
    mk0, mv0 = mem_kv(0)

    qp, kp, vp, op, xqp, gp = norm_proj(xp, norm_mix[0], w0, hi0)
    r3 = lambda a: a.reshape(bp, seq, a.shape[-1])
    hp, c_p, n_p, m_p = mlstm_prompt(r3(qp), r3(kp), r3(vp), r3(op), r3(gp), cwq, cwk, b_gate_a[0], head_g)
    conv_p = jnp.concatenate([_unpad_heads(r3(qp)[:, seq - (CONV_W - 1):]),
                              _unpad_heads(r3(kp)[:, seq - (CONV_W - 1):])], axis=-1)
    ap = mem_attention(r3(xqp), mk0, mv0)
    xp = mix_out(0, xp, hp, ap, wh0)
    xp = ffn(xp, norm_ffn[0], bf(w_ffn_gu[0]), bf(w_ffn_d[0]))

    qs, ks_, vs, os_, xqs, gs = norm_proj(xs, norm_mix[0], w0, hi0)
    s3 = lambda a: a.reshape(bs, 1, a.shape[-1])
    hist = state_mlstm_conv[0].astype(F32)
    hs, c_s, n_s, m_s = mlstm_step(
        s3(qs), s3(ks_), s3(vs), s3(os_), s3(gs), _pad_heads(hist[..., :TOK_WIDTH]),
        _pad_heads(hist[..., TOK_WIDTH:]), state_mlstm_C[0].astype(F32), state_mlstm_n[0].astype(F32),
        state_mlstm_m[0].astype(F32), cwq, cwk, b_gate_a[0], head_g)
    qk_new = jnp.concatenate([_unpad_heads(s3(qs)), _unpad_heads(s3(ks_))], axis=-1)
    conv_s = jnp.concatenate([hist, qk_new], axis=1)[:, 1:]
    as_ = mem_attention(s3(xqs), cache_mem_k[0].reshape(bs, n_mem, X_WIDTH).astype(F32),
                        cache_mem_v[0].reshape(bs, n_mem, X_WIDTH).astype(F32))
    xs = mix_out(0, xs, hs, as_, wh0)
    xs = ffn(xs, norm_ffn[0], bf(w_ffn_gu[0]), bf(w_ffn_d[0]))

    wb = w_in_b[0]
    kv_end = TOK_WIDTH + 6 * KV_WIDTH
    w1 = [bf(_cols_to_rgd(wb[:, :TOK_WIDTH])), bf(wb[:, TOK_WIDTH:kv_end]), bf(wb[:, kv_end + 3 * N_Q:]),
          wb[:, kv_end:kv_end + 3 * N_Q]]
    hi1 = [False] * 3 + [True]
    wh1 = w_out[1][:TOK_WIDTH].reshape(N_KV, GROUP, HEAD_DIM, d)
    wh1 = bf(jnp.swapaxes(wh1, 0, 1).reshape(TOK_WIDTH, d))
    w1bd = bf(_block_diag4(w_cmp1[0].reshape(2, CMP_LEN, HEAD_DIM, HEAD_DIM)))
    w2bd = bf(_block_diag4(w_cmp2[0]))
    pe4 = jnp.tile(pe_cmp[0], (1, 1, N_KV))
    mk1, mv1 = mem_kv(1)

    q1, kv1, xq1, g1 = norm_proj(xp, norm_mix[1], w1, hi1)
    cmp_p = compress_prompt(r3(kv1), w1bd, w2bd, pe4)
    hp1 = nsa_prompt(r3(q1), r3(kv1), cmp_p, r3(g1), b_gate_b[0])
    rows_p = r3(kv1)[:, :, :4 * KV_WIDTH].reshape(bp, seq, 4, N_KV, HEAD_DIM)
    wlen = min(WINDOW, seq)
    win_p = r3(kv1)[:, seq - wlen:, 4 * KV_WIDTH:].reshape(bp, wlen, 2, N_KV, HEAD_DIM)
    ap1 = mem_attention(r3(xq1), mk1, mv1)
    xp = mix_out(1, xp, hp1, ap1, wh1)

    n_pool, page = cache_nsa.shape[1], cache_nsa.shape[2]
    past_len = page_table.shape[1] * page
    wbuf = cache_nsa_win.shape[2]
    n_slc = -(-(past_len + 1) // SLC_BLOCK)
    n_top = min(N_SEL, n_slc)
    pool = cache_nsa.reshape(cache_nsa.shape[0], n_pool, page, 4 * KV_WIDTH).astype(F32)
    q1s, kv1s, xq1s, g1s = norm_proj(xs, norm_mix[1], w1, hi1)
    cmp_s = compress_paged(pool, page_table, 0, w1bd, w2bd, pe4)
    oc_s, idx_s = nsa_select(s3(q1s), cmp_s, past_len, n_slc, n_top)
    win_old = cache_nsa_win[0].astype(F32)
    hs1 = nsa_decode(pool, page_table, 0, idx_s, s3(q1s), s3(kv1s), win_old.reshape(bs, wbuf, 2 * KV_WIDTH),
                     oc_s, s3(g1s), b_gate_b[0], past_len, n_top)
    rows_s = kv1s[:, :4 * KV_WIDTH].reshape(bs, 1, 4, N_KV, HEAD_DIM)
    win_new = kv1s[:, 4 * KV_WIDTH:].reshape(bs, 1, 2, N_KV, HEAD_DIM)
    win_s = jnp.concatenate([win_old, win_new], axis=1)[:, -wbuf:]
    as1 = mem_attention(s3(xq1s), cache_mem_k[1].reshape(bs, n_mem, X_WIDTH).astype(F32),
                        cache_mem_v[1].reshape(bs, n_mem, X_WIDTH).astype(F32))
    xs = mix_out(1, xs, hs1, as1, wh1)

    w_r = jnp.pad(w_router[0], ((0, 0), (0, 128 - N_EXPERTS)))
    w_gu, w_dn = bf(w_moe_gu[0]), bf(w_moe_d[0])

    def experts(x, tm):
        u, comb, rank, counts = router(x, norm_ffn[1], w_r, tm)
        return final_norm(x, moe(u, comb, rank, counts, w_gu, w_dn, tm), norm_final)

    y_prompt = experts(xp, min(MOE_TILE, bp * seq)).reshape(bp, seq, d)
    y_sample = experts(xs, bs).reshape(bs, 1, d)

    unflat = lambda a: a.reshape(bp, n_mem, X_HEADS, HEAD_DIM)
    return (y_prompt, y_sample,
            jnp.stack([unflat(mk0), unflat(mk1)]), jnp.stack([unflat(mv0), unflat(mv1)]),
            c_p[None], n_p[None], m_p[None, :, :, 0], conv_p[None], rows_p[None], win_p[None],
            c_s[None], n_s[None], m_s[None, :, :, 0], conv_s[None], rows_s[None], win_s[None])
```

```python
import functools

import jax
import jax.numpy as jnp
from jax import lax
from jax.experimental import pallas as pl
from jax.experimental.pallas import tpu as pltpu

F32 = jnp.float32
BF16 = jnp.bfloat16
HIGHEST = lax.Precision.HIGHEST

D_MODEL = 1024
HEAD_DIM = 64
X_HEADS = 4
X_WIDTH = X_HEADS * HEAD_DIM
TOK_WIDTH = D_MODEL - X_WIDTH
M_HEADS = 8
M_HEAD_DIM = TOK_WIDTH // M_HEADS
M_HEAD_PAD = 128
M_PAD_WIDTH = M_HEADS * M_HEAD_PAD
CONV_W = 4
N_Q = TOK_WIDTH // HEAD_DIM
N_KV = 4
GROUP = N_Q // N_KV
KV_WIDTH = N_KV * HEAD_DIM
CMP_STRIDE = 16
CMP_LEN = 2 * CMP_STRIDE
SLC_BLOCK = 64
N_SEL = 16
WINDOW = 512
Q_BLOCK = 128
D_FF = 2816
N_EXPERTS = 8
TOP_K = 2
E_FF = 7 * D_MODEL // 4
RMS_EPS = 1e-6
NEG = -1e30
BIG = 1e9

MLSTM_CHUNK = 128
VMEM_LIMIT = 56 * 1024 * 1024


def _cparams(sem):
    return pltpu.CompilerParams(dimension_semantics=sem, vmem_limit_bytes=VMEM_LIMIT)


def _rms(x, g):
    return x * lax.rsqrt(jnp.mean(x * x, axis=-1, keepdims=True) + RMS_EPS) * g


def _dot(a, b):
    return jnp.dot(a, b, preferred_element_type=F32)


def _dot_nt(a, b):
    return lax.dot_general(a, b, (((1,), (1,)), ((), ())), preferred_element_type=F32)


def _dot_tn(a, b):
    return lax.dot_general(a, b, (((0,), (0,)), ((), ())), preferred_element_type=F32)


def _norm_proj_body(x_ref, g_ref, *refs, n_out, hi):
    w_refs, o_refs = refs[:n_out], refs[n_out:]
    u = _rms(x_ref[...], g_ref[...])
    ub = u.astype(BF16)
    for w_ref, o_ref, h in zip(w_refs, o_refs, hi):
        if h:
            o_ref[...] = jnp.dot(u, w_ref[...], precision=HIGHEST, preferred_element_type=F32)
        else:
            o_ref[...] = _dot(ub, w_ref[...])


def norm_proj(x, gamma, weights, hi):
    m, d = x.shape
    tm = min(512, m)
    n_out = len(weights)
    in_specs = [pl.BlockSpec((tm, d), lambda i: (i, 0)), pl.BlockSpec((1, d), lambda i: (0, 0))]
    in_specs += [pl.BlockSpec(w.shape, lambda i: (0, 0)) for w in weights]
    out_specs = [pl.BlockSpec((tm, w.shape[1]), lambda i: (i, 0)) for w in weights]
    out_shape = [jax.ShapeDtypeStruct((m, w.shape[1]), F32) for w in weights]
    return pl.pallas_call(
        functools.partial(_norm_proj_body, n_out=n_out, hi=tuple(hi)),
        grid=(m // tm,), in_specs=in_specs, out_specs=out_specs, out_shape=out_shape,
        compiler_params=_cparams(("parallel",)), name="norm_proj",
    )(x, gamma.reshape(1, d), *weights)


def _mem_attn_body(q_ref, k_ref, v_ref, o_ref):
    q = q_ref[...] * (HEAD_DIM ** -0.5)
    rows = q.shape[0]
    if rows < 8:
        q = jnp.broadcast_to(q[0:1], (8, X_WIDTH))
    kb = k_ref[...].astype(BF16)
    vb = v_ref[...].astype(BF16)
    head = lax.broadcasted_iota(jnp.int32, (1, X_WIDTH), 1) // HEAD_DIM
    out = jnp.zeros(q.shape, F32)
    for h in range(X_HEADS):
        qh = jnp.where(head == h, q, 0.0).astype(BF16)
        s = _dot_nt(qh, kb)
        p = jnp.exp(s - jnp.max(s, axis=-1, keepdims=True))
        o = _dot(p.astype(BF16), vb) / jnp.sum(p, axis=-1, keepdims=True)
        out = jnp.where(head == h, o, out)
    o_ref[...] = out[:rows]


def mem_attention(q, mk, mv):
    b, t, w = q.shape
    n = mk.shape[1]
    tq = min(512, t)
    return pl.pallas_call(
        _mem_attn_body, grid=(b, t // tq),
        in_specs=[pl.BlockSpec((None, tq, w), lambda i, j: (i, j, 0)),
                  pl.BlockSpec((None, n, w), lambda i, j: (i, 0, 0)),
                  pl.BlockSpec((None, n, w), lambda i, j: (i, 0, 0))],
        out_specs=pl.BlockSpec((None, tq, w), lambda i, j: (i, j, 0)),
        out_shape=jax.ShapeDtypeStruct((b, t, w), F32),
        compiler_params=_cparams(("parallel", "parallel")), name="mem_attention",
    )(q, mk, mv)


def _out_proj_body(x_ref, h_ref, a_ref, wh_ref, wa_ref, o_ref):
    o_ref[...] = (x_ref[...] + _dot(h_ref[...].astype(BF16), wh_ref[...])
                  + _dot(a_ref[...].astype(BF16), wa_ref[...]))


def out_proj(x, h, a, wh, wa):
    m, d = x.shape
    tm = min(512, m)
    kh, ka = h.shape[1], a.shape[1]
    return pl.pallas_call(
        _out_proj_body, grid=(m // tm,),
        in_specs=[pl.BlockSpec((tm, d), lambda i: (i, 0)),
                  pl.BlockSpec((tm, kh), lambda i: (i, 0)),
                  pl.BlockSpec((tm, ka), lambda i: (i, 0)),
                  pl.BlockSpec((kh, d), lambda i: (0, 0)),
                  pl.BlockSpec((ka, d), lambda i: (0, 0))],
        out_specs=pl.BlockSpec((tm, d), lambda i: (i, 0)),
        out_shape=jax.ShapeDtypeStruct((m, d), F32),
        compiler_params=_cparams(("parallel",)), name="out_proj",
    )(x, h, a, wh, wa)


def _out_proj_t_body(x_ref, ht_ref, a_ref, wh_ref, wa_ref, o_ref):
    o_ref[...] = (x_ref[...] + _dot_tn(ht_ref[...], wh_ref[...])
                  + _dot(a_ref[...].astype(BF16), wa_ref[...]))


def out_proj_t(x, ht, a, wh, wa):
    m, d = x.shape
    tm = min(512, m)
    kh, ka = ht.shape[0], a.shape[1]
    return pl.pallas_call(
        _out_proj_t_body, grid=(m // tm,),
        in_specs=[pl.BlockSpec((tm, d), lambda i: (i, 0)),
                  pl.BlockSpec((kh, tm), lambda i: (0, i)),
                  pl.BlockSpec((tm, ka), lambda i: (i, 0)),
                  pl.BlockSpec((kh, d), lambda i: (0, 0)),
                  pl.BlockSpec((ka, d), lambda i: (0, 0))],
        out_specs=pl.BlockSpec((tm, d), lambda i: (i, 0)),
        out_shape=jax.ShapeDtypeStruct((m, d), F32),
        compiler_params=_cparams(("parallel",)), name="out_proj_t",
    )(x, ht, a, wh, wa)


def _nsa_proj_body(x_ref, g_ref, wq_ref, wkv_ref, wvt_ref, wxq_ref, wgt_ref,
                   q_ref, kv_ref, kb_ref, vt_ref, xq_ref, gt_ref):
    u = _rms(x_ref[...], g_ref[...])
    ub = u.astype(BF16)
    q_ref[...] = _dot(ub, wq_ref[...]).astype(BF16)
    kv = _dot(ub, wkv_ref[...])
    kv_ref[...] = kv
    kb_ref[...] = jnp.concatenate([kv[:, 2 * KV_WIDTH:3 * KV_WIDTH], kv[:, 4 * KV_WIDTH:5 * KV_WIDTH]],
                                  axis=1).astype(BF16)
    vt_ref[...] = _dot_nt(wvt_ref[...], ub).astype(BF16)
    xq_ref[...] = _dot(ub, wxq_ref[...])
    gt_ref[...] = lax.dot_general(wgt_ref[...], u, (((1,), (1,)), ((), ())), precision=HIGHEST,
                                  preferred_element_type=F32)


def nsa_proj(x, gamma, wq, wkv, wvt, wxq, wgt):
    m, d = x.shape
    tm = min(512, m)
    const = lambda a: pl.BlockSpec(a.shape, lambda i: (0, 0))
    rows = lambda n: pl.BlockSpec((tm, n), lambda i: (i, 0))
    cols = lambda n: pl.BlockSpec((n, tm), lambda i: (0, i))
    ng = wgt.shape[0]
    return pl.pallas_call(
        _nsa_proj_body, grid=(m // tm,),
        in_specs=[rows(d), pl.BlockSpec((1, d), lambda i: (0, 0)),
                  const(wq), const(wkv), const(wvt), const(wxq), const(wgt)],
        out_specs=[rows(TOK_WIDTH), rows(6 * KV_WIDTH), rows(2 * KV_WIDTH), cols(2 * KV_WIDTH),
                   rows(X_WIDTH), cols(ng)],
        out_shape=[jax.ShapeDtypeStruct((m, TOK_WIDTH), BF16),
                   jax.ShapeDtypeStruct((m, 6 * KV_WIDTH), F32),
                   jax.ShapeDtypeStruct((m, 2 * KV_WIDTH), BF16),
                   jax.ShapeDtypeStruct((2 * KV_WIDTH, m), BF16),
                   jax.ShapeDtypeStruct((m, X_WIDTH), F32),
                   jax.ShapeDtypeStruct((ng, m), F32)],
        compiler_params=_cparams(("parallel",)), name="nsa_proj",
    )(x, gamma.reshape(1, d), wq, wkv, wvt, wxq, wgt)


def _ffn_body(x_ref, g_ref, wg_ref, wu_ref, wd_ref, o_ref, u_sc):
    @pl.when(pl.program_id(1) == 0)
    def _():
        x = x_ref[...]
        u_sc[...] = _rms(x, g_ref[...]).astype(BF16)
        o_ref[...] = x

    u = u_sc[...]
    hg = _dot(u, wg_ref[...])
    hu = _dot(u, wu_ref[...])
    hm = (hg * jax.nn.sigmoid(hg) * hu).astype(BF16)
    o_ref[...] += _dot(hm, wd_ref[...])


def ffn(x, gamma, w_gu, w_d):
    m, d = x.shape
    f = w_d.shape[0]
    tm = min(1024, m)
    tf = 256
    nf = f // tf
    return pl.pallas_call(
        _ffn_body, grid=(m // tm, nf),
        in_specs=[pl.BlockSpec((tm, d), lambda i, j: (i, 0)),
                  pl.BlockSpec((1, d), lambda i, j: (0, 0)),
                  pl.BlockSpec((d, tf), lambda i, j: (0, j)),
                  pl.BlockSpec((d, tf), lambda i, j: (0, nf + j)),
                  pl.BlockSpec((tf, d), lambda i, j: (j, 0))],
        out_specs=pl.BlockSpec((tm, d), lambda i, j: (i, 0)),
        out_shape=jax.ShapeDtypeStruct((m, d), F32),
        scratch_shapes=[pltpu.VMEM((tm, d), BF16)],
        compiler_params=_cparams(("parallel", "arbitrary")), name="ffn",
    )(x, gamma.reshape(1, d), w_gu, w_gu, w_d)


def _router_body(x_ref, g_ref, wr_ref, u_ref, comb_ref, rank_ref, cnt_ref):
    tm = x_ref.shape[0]
    u = _rms(x_ref[...], g_ref[...])
    u_ref[...] = u.astype(BF16)
    logits = jnp.dot(u, wr_ref[...], precision=HIGHEST, preferred_element_type=F32)
    lane = lax.broadcasted_iota(jnp.int32, (tm, 128), 1)
    logits = jnp.where(lane < N_EXPERTS, logits, -jnp.inf)
    m1 = jnp.max(logits, axis=-1, keepdims=True)
    i1 = jnp.min(jnp.where(logits == m1, lane, 128), axis=-1, keepdims=True)
    rest = jnp.where(lane == i1, -jnp.inf, logits)
    m2 = jnp.max(rest, axis=-1, keepdims=True)
    i2 = jnp.min(jnp.where(rest == m2, lane, 128), axis=-1, keepdims=True)
    e2 = jnp.exp(m2 - m1)
    g1 = 1.0 / (1.0 + e2)
    g2 = e2 / (1.0 + e2)
    comb_ref[...] = jnp.where(lane == i1, g1, 0.0) + jnp.where(lane == i2, g2, 0.0)
    sel = ((lane == i1) | (lane == i2)).astype(F32)
    r = lax.broadcasted_iota(jnp.int32, (tm, tm), 0)
    c = lax.broadcasted_iota(jnp.int32, (tm, tm), 1)
    tril = (c < r).astype(BF16)
    rank = _dot(tril, sel.astype(BF16))
    rank_ref[...] = jnp.where(sel > 0, rank, -1.0)
    cnt_ref[...] = jnp.sum(sel, axis=0, keepdims=True).astype(jnp.int32)


def router(x, gamma, w_router_pad, tm):
    m, d = x.shape
    nt = m // tm
    return pl.pallas_call(
        _router_body, grid=(nt,),
        in_specs=[pl.BlockSpec((tm, d), lambda i: (i, 0)),
                  pl.BlockSpec((1, d), lambda i: (0, 0)),
                  pl.BlockSpec((d, 128), lambda i: (0, 0))],
        out_specs=[pl.BlockSpec((tm, d), lambda i: (i, 0)),
                   pl.BlockSpec((tm, 128), lambda i: (i, 0)),
                   pl.BlockSpec((tm, 128), lambda i: (i, 0)),
                   pl.BlockSpec((None, 1, 128), lambda i: (i, 0, 0))],
        out_shape=[jax.ShapeDtypeStruct((m, d), BF16),
                   jax.ShapeDtypeStruct((m, 128), F32),
                   jax.ShapeDtypeStruct((m, 128), F32),
                   jax.ShapeDtypeStruct((nt, 1, 128), jnp.int32)],
        compiler_params=_cparams(("parallel",)), name="router",
    )(x, gamma.reshape(1, d), w_router_pad)


def _moe_body(cnt_ref, u_ref, rcol_ref, rrow_ref, gcol_ref, wg_ref, wu_ref, wd_ref, o_ref, *, cap):
    i = pl.program_id(0)
    e = pl.program_id(1)
    tm = u_ref.shape[0]

    @pl.when(e == 0)
    def _():
        o_ref[...] = jnp.zeros(o_ref.shape, F32)

    cnt = cnt_ref[i * N_EXPERTS + e]
    n_chunks = (cnt + cap - 1) // cap

    def chunk(c, carry):
        base = (c * cap).astype(F32)
        slot_r = lax.broadcasted_iota(jnp.int32, (cap, tm), 0).astype(F32) + base
        gather = (rrow_ref[...] == slot_r).astype(BF16)
        xg = _dot(gather, u_ref[...]).astype(BF16)
        hg = _dot(xg, wg_ref[...])
        hu = _dot(xg, wu_ref[...])
        hm = (hg * jax.nn.sigmoid(hg) * hu).astype(BF16)
        y = _dot(hm, wd_ref[...]).astype(BF16)
        slot_c = lax.broadcasted_iota(jnp.int32, (tm, cap), 1).astype(F32) + base
        scatter = (rcol_ref[...] == slot_c).astype(BF16)
        o_ref[...] += gcol_ref[...] * _dot(scatter, y)
        return carry

    lax.fori_loop(0, n_chunks, chunk, 0)


def moe(u, comb, rank, counts, w_gu, w_d, tm):
    m, d = u.shape
    nt = m // tm
    cap = min(256, tm)
    ef = w_d.shape[1]
    comb_e = comb[:, :N_EXPERTS].T
    rank_e = rank[:, :N_EXPERTS].T
    rcol = rank_e.reshape(N_EXPERTS, m, 1)
    rrow = rank_e.reshape(N_EXPERTS, 1, m)
    gcol = comb_e.reshape(N_EXPERTS, m, 1)
    cnt = counts[:, 0, :N_EXPERTS].reshape(nt * N_EXPERTS)
    grid_spec = pltpu.PrefetchScalarGridSpec(
        num_scalar_prefetch=1, grid=(nt, N_EXPERTS),
        in_specs=[pl.BlockSpec((tm, d), lambda i, e, c: (i, 0)),
                  pl.BlockSpec((None, tm, 1), lambda i, e, c: (e, i, 0)),
                  pl.BlockSpec((None, 1, tm), lambda i, e, c: (e, 0, i)),
                  pl.BlockSpec((None, tm, 1), lambda i, e, c: (e, i, 0)),
                  pl.BlockSpec((None, d, ef), lambda i, e, c: (e, 0, 0)),
                  pl.BlockSpec((None, d, ef), lambda i, e, c: (e, 0, 1)),
                  pl.BlockSpec((None, ef, d), lambda i, e, c: (e, 0, 0))],
        out_specs=pl.BlockSpec((tm, d), lambda i, e, c: (i, 0)))
    return pl.pallas_call(
        functools.partial(_moe_body, cap=cap), grid_spec=grid_spec,
        out_shape=jax.ShapeDtypeStruct((m, d), F32),
        compiler_params=_cparams(("parallel", "arbitrary")), name="moe",
    )(cnt, u, rcol, rrow, gcol, w_gu, w_gu, w_d)


def _final_body(x_ref, y_ref, g_ref, o_ref):
    o_ref[...] = _rms(x_ref[...] + y_ref[...], g_ref[...])


def final_norm(x, y, gamma):
    m, d = x.shape
    tm = min(1024, m)
    return pl.pallas_call(
        _final_body, grid=(m // tm,),
        in_specs=[pl.BlockSpec((tm, d), lambda i: (i, 0)),
                  pl.BlockSpec((tm, d), lambda i: (i, 0)),
                  pl.BlockSpec((1, d), lambda i: (0, 0))],
        out_specs=pl.BlockSpec((tm, d), lambda i: (i, 0)),
        out_shape=jax.ShapeDtypeStruct((m, d), F32),
        compiler_params=_cparams(("parallel",)), name="final_norm",
    )(x, y, gamma.reshape(1, d))


def _log_sigmoid(x):
    return jnp.minimum(x, 0.0) - jnp.log(1.0 + jnp.exp(-jnp.abs(x)))


def _silu(x):
    return x * jax.nn.sigmoid(x)


def _mlstm_body(qp_ref, kp_ref, v_ref, o_ref, g_ref, gt_ref, cwq_ref, cwk_ref, bgr_ref, bgc_ref, hg_ref,
                h_ref, c_out, n_out, m_out, c_sc, n_sc, m_sc, tq_sc, tk_sc):
    j = pl.program_id(1)
    L = qp_ref.shape[0]
    P = M_HEAD_PAD

    @pl.when(j == 0)
    def _():
        c_sc[...] = jnp.zeros(c_sc.shape, F32)
        n_sc[...] = jnp.zeros(n_sc.shape, F32)
        m_sc[...] = jnp.zeros(m_sc.shape, F32)
        tq_sc[...] = jnp.zeros(tq_sc.shape, F32)
        tk_sc[...] = jnp.zeros(tk_sc.shape, F32)

    def conv(x_ref, tail_sc, cw_ref):
        x = x_ref[...]
        ext = jnp.concatenate([tail_sc[...], x], axis=0)
        y = cw_ref[3:4, :] * x
        for tap in range(CONV_W - 1):
            lo = 8 - (CONV_W - 1) + tap
            y = y + cw_ref[tap:tap + 1, :] * ext[lo:lo + L]
        tail_sc[...] = x[L - 8:L]
        return _silu(y)

    qc = conv(qp_ref, tq_sc, cwq_ref)
    kc = conv(kp_ref, tk_sc, cwk_ref) * (M_HEAD_DIM ** -0.5)

    g = g_ref[...] + bgr_ref[...]
    gt = gt_ref[...] + bgc_ref[...]
    ig_c, lf_c = g[:, :M_HEADS], _log_sigmoid(g[:, M_HEADS:])
    ig_r, lf_r = gt[:M_HEADS], _log_sigmoid(gt[M_HEADS:])
    row = lax.broadcasted_iota(jnp.int32, (L, L), 0)
    col = lax.broadcasted_iota(jnp.int32, (L, L), 1)
    causal = col <= row
    f_c = jnp.dot(causal.astype(F32), lf_c, precision=HIGHEST, preferred_element_type=F32)
    f_r = jnp.dot(lf_r, (row <= col).astype(F32), precision=HIGHEST, preferred_element_type=F32)

    for h in range(M_HEADS):
        sl = slice(h * P, (h + 1) * P)
        q, k, v = qc[:, sl], kc[:, sl], v_ref[:, sl]
        fc, fr = f_c[:, h:h + 1], f_r[h:h + 1, :]
        m_prev = m_sc[h:h + 1, 0:1]
        d = jnp.where(causal, fc - fr + ig_r[h:h + 1, :], NEG)
        a = fc + m_prev
        mt = jnp.maximum(a, jnp.max(d, axis=1, keepdims=True))
        wm = jnp.exp(d - mt)
        gcol = jnp.exp(a - mt)
        qb, kb, vb = q.astype(BF16), k.astype(BF16), v.astype(BF16)
        s = _dot_nt(qb, kb) * wm
        c_old = c_sc[h]
        n_old = n_sc[h:h + 1, :]
        num = _dot(s.astype(BF16), vb) + gcol * _dot_nt(qb, c_old.astype(BF16))
        den = jnp.sum(s, axis=1, keepdims=True) + gcol * jnp.sum(q * n_old, axis=1, keepdims=True)
        hc = num / jnp.maximum(jnp.abs(den), jnp.exp(-mt))
        og = jax.nn.sigmoid(o_ref[:, sl]) * hc
        ms = jnp.sum(og * og, axis=1, keepdims=True) * (1.0 / M_HEAD_DIM)
        h_ref[:, sl] = og * lax.rsqrt(ms + RMS_EPS) * hg_ref[:, sl]
        mt_last, gl = mt[L - 1:L], gcol[L - 1:L]
        wl = jnp.exp(fc[L - 1:L] - fc + ig_c[:, h:h + 1] - mt_last)
        c_sc[h] = gl * c_old + _dot_tn((v * wl).astype(BF16), kb)
        n_sc[h:h + 1, :] = gl * n_old + jnp.sum(wl * k, axis=0, keepdims=True)
        m_sc[h:h + 1, :] = jnp.broadcast_to(mt_last, (1, P))

    @pl.when(j == pl.num_programs(1) - 1)
    def _():
        for h in range(M_HEADS):
            c_out[h] = c_sc[h][:M_HEAD_DIM, :M_HEAD_DIM]
        n_out[...] = n_sc[:, :M_HEAD_DIM]
        m_out[...] = m_sc[...]


def mlstm_prompt(qp, kp, v, o, g, cwq, cwk, b_gate, head_g):
    b, t, w = qp.shape
    L = min(MLSTM_CHUNK, t)
    gt = jnp.swapaxes(g, 1, 2)
    seq = pl.BlockSpec((None, L, w), lambda i, j: (i, j, 0))
    const = lambda shape: pl.BlockSpec(shape, lambda i, j: (0,) * len(shape))
    return pl.pallas_call(
        _mlstm_body, grid=(b, t // L),
        in_specs=[seq, seq, seq, seq,
                  pl.BlockSpec((None, L, 2 * M_HEADS), lambda i, j: (i, j, 0)),
                  pl.BlockSpec((None, 2 * M_HEADS, L), lambda i, j: (i, 0, j)),
                  const((CONV_W, w)), const((CONV_W, w)), const((1, 2 * M_HEADS)), const((2 * M_HEADS, 1)),
                  const((1, w))],
        out_specs=[seq,
                   pl.BlockSpec((None, M_HEADS, M_HEAD_DIM, M_HEAD_DIM), lambda i, j: (i, 0, 0, 0)),
                   pl.BlockSpec((None, M_HEADS, M_HEAD_DIM), lambda i, j: (i, 0, 0)),
                   pl.BlockSpec((None, M_HEADS, M_HEAD_PAD), lambda i, j: (i, 0, 0))],
        out_shape=[jax.ShapeDtypeStruct((b, t, w), F32),
                   jax.ShapeDtypeStruct((b, M_HEADS, M_HEAD_DIM, M_HEAD_DIM), F32),
                   jax.ShapeDtypeStruct((b, M_HEADS, M_HEAD_DIM), F32),
                   jax.ShapeDtypeStruct((b, M_HEADS, M_HEAD_PAD), F32)],
        scratch_shapes=[pltpu.VMEM((M_HEADS, M_HEAD_PAD, M_HEAD_PAD), F32),
                        pltpu.VMEM((M_HEADS, M_HEAD_PAD), F32),
                        pltpu.VMEM((M_HEADS, M_HEAD_PAD), F32),
                        pltpu.VMEM((8, w), F32), pltpu.VMEM((8, w), F32)],
        compiler_params=_cparams(("parallel", "arbitrary")), name="mlstm_prompt",
    )(qp, kp, v, o, g, gt, cwq, cwk, b_gate.reshape(1, -1), b_gate.reshape(-1, 1), head_g)


def _mlstm_step_body(qp_ref, kp_ref, v_ref, o_ref, g_ref, hq_ref, hk_ref, c0_ref, n0_ref, m0_ref,
                     cwq_ref, cwk_ref, bgr_ref, hg_ref, h_ref, c_out, n_out, m_out):
    P, dh = M_HEAD_PAD, M_HEAD_DIM

    def conv(x_ref, hist_ref, cw_ref):
        y = cw_ref[3:4, :] * x_ref[...]
        for tap in range(CONV_W - 1):
            y = y + cw_ref[tap:tap + 1, :] * hist_ref[tap:tap + 1, :]
        return _silu(y)

    qc = conv(qp_ref, hq_ref, cwq_ref)
    kc = conv(kp_ref, hk_ref, cwk_ref) * (dh ** -0.5)
    g = g_ref[...] + bgr_ref[...]
    m0 = m0_ref[...]
    for h in range(M_HEADS):
        lo = h * P
        q, k, v = qc[:, lo:lo + dh], kc[:, lo:lo + dh], v_ref[:, lo:lo + dh]
        ig = g[:, h:h + 1]
        lf = _log_sigmoid(g[:, M_HEADS + h:M_HEADS + h + 1])
        a = lf + m0[h:h + 1, :]
        mt = jnp.maximum(a, ig)
        wv = jnp.exp(ig - mt)
        gv = jnp.exp(a - mt)
        qf, kf = q.astype(BF16).astype(F32), k.astype(BF16).astype(F32)
        qk = jnp.sum(qf * kf, axis=1, keepdims=True) * wv
        c_old, n_old = c0_ref[h], n0_ref[h:h + 1, :]
        q8 = jnp.broadcast_to(q, (8, dh)).astype(BF16)
        cq = _dot_nt(q8, c_old.astype(BF16))[0:1]
        num = qk * v + gv * cq
        den = qk + gv * jnp.sum(q * n_old, axis=1, keepdims=True)
        hc = num / jnp.maximum(jnp.abs(den), jnp.exp(-mt))
        og = jax.nn.sigmoid(o_ref[:, lo:lo + dh]) * hc
        ms = jnp.sum(og * og, axis=1, keepdims=True) * (1.0 / dh)
        hn = og * lax.rsqrt(ms + RMS_EPS) * hg_ref[:, lo:lo + dh]
        h_ref[:, lo:lo + P] = jnp.concatenate([hn, jnp.zeros((1, P - dh), F32)], axis=1)
        v_col = jnp.transpose(jnp.broadcast_to(v_ref[:, lo:lo + P], (8, P)))[:dh, 0:1]
        c_out[h] = gv * c_old + wv * (v_col * k)
        n_out[h:h + 1, :] = gv * n_old + wv * k
        m_out[h:h + 1, :] = jnp.broadcast_to(mt, (1, P))


def mlstm_step(qp, kp, v, o, g, hist_q, hist_k, c0, n0, m0, cwq, cwk, b_gate, head_g):
    b, _, w = qp.shape
    row = pl.BlockSpec((None, 1, w), lambda i: (i, 0, 0))
    const = lambda shape: pl.BlockSpec(shape, lambda i: (0,) * len(shape))
    cspec = pl.BlockSpec((None, M_HEADS, M_HEAD_DIM, M_HEAD_DIM), lambda i: (i, 0, 0, 0))
    nspec = pl.BlockSpec((None, M_HEADS, M_HEAD_DIM), lambda i: (i, 0, 0))
    return pl.pallas_call(
        _mlstm_step_body, grid=(b,),
        in_specs=[row, row, row, row,
                  pl.BlockSpec((None, 1, 2 * M_HEADS), lambda i: (i, 0, 0)),
                  pl.BlockSpec((None, CONV_W - 1, w), lambda i: (i, 0, 0)),
                  pl.BlockSpec((None, CONV_W - 1, w), lambda i: (i, 0, 0)),
                  cspec, nspec,
                  pl.BlockSpec((None, M_HEADS, 1), lambda i: (i, 0, 0)),
                  const((CONV_W, w)), const((CONV_W, w)), const((1, 2 * M_HEADS)), const((1, w))],
        out_specs=[row, cspec, nspec,
                   pl.BlockSpec((None, M_HEADS, M_HEAD_PAD), lambda i: (i, 0, 0))],
        out_shape=[jax.ShapeDtypeStruct((b, 1, w), F32),
                   jax.ShapeDtypeStruct(c0.shape, F32),
                   jax.ShapeDtypeStruct(n0.shape, F32),
                   jax.ShapeDtypeStruct((b, M_HEADS, M_HEAD_PAD), F32)],
        compiler_params=_cparams(("parallel",)), name="mlstm_step",
    )(qp, kp, v, o, g, hist_q, hist_k, c0, n0, m0.reshape(b, M_HEADS, 1), cwq, cwk,
      b_gate.reshape(1, -1), head_g)


def _pad_heads(a):
    lead = a.shape[:-1]
    a = a.reshape(lead + (M_HEADS, M_HEAD_DIM))
    a = jnp.pad(a, [(0, 0)] * len(lead) + [(0, 0), (0, M_HEAD_PAD - M_HEAD_DIM)])
    return a.reshape(lead + (M_PAD_WIDTH,))


def _unpad_heads(a):
    lead = a.shape[:-1]
    return a.reshape(lead + (M_HEADS, M_HEAD_PAD))[..., :M_HEAD_DIM].reshape(lead + (TOK_WIDTH,))


def _gelu_tanh(x):
    return x * (0.5 * (1.0 + jnp.tanh(0.7978845608028654 * (x + 0.044715 * (x * x * x)))))


def _compress_tail(u, v_next, pe_ref, w1_ref, w2_ref):
    bias = jnp.zeros((8, KV_WIDTH), F32)
    for p in range(CMP_LEN):
        pe8 = jnp.broadcast_to(pe_ref[p:p + 1, :], (8, KV_WIDTH)).astype(BF16)
        bias = bias + _dot(pe8, w1_ref[p])
    pre = u + v_next + bias[0:1]
    return _dot(_gelu_tanh(pre).astype(BF16), w2_ref[...])


PERM_ROWS = CMP_STRIDE * CMP_STRIDE


def _regroup_rows(x, ybuf, grp, transposed=False):
    r = lax.broadcasted_iota(jnp.int32, (PERM_ROWS, PERM_ROWS), 0)
    c = lax.broadcasted_iota(jnp.int32, (PERM_ROWS, PERM_ROWS), 1)
    perm = (c == (r % CMP_STRIDE) * CMP_STRIDE + r // CMP_STRIDE).astype(BF16)
    y = (_dot_nt(perm, x) if transposed else _dot(perm, x)).astype(BF16)
    for p in range(CMP_STRIDE):
        ybuf[p, pl.ds(pl.multiple_of(grp * CMP_STRIDE, CMP_STRIDE), CMP_STRIDE), :] = (
            y[p * CMP_STRIDE:(p + 1) * CMP_STRIDE])


def _compress_from(ybuf, w1_ref, w2_ref, pe_ref, o_ref):
    nb = ybuf.shape[1]
    u = jnp.zeros((nb, KV_WIDTH), F32)
    v = jnp.zeros((nb, KV_WIDTH), F32)
    for p in range(CMP_STRIDE):
        x = ybuf[p]
        u = u + _dot(x, w1_ref[p])
        v = v + _dot(x, w1_ref[CMP_STRIDE + p])
    v_next = jnp.concatenate([v[1:], jnp.zeros((1, KV_WIDTH), F32)], axis=0)
    o_ref[...] = _compress_tail(u, v_next, pe_ref, w1_ref, w2_ref)


def _compress_body(rows_ref, w1_ref, w2_ref, pe_ref, o_ref, ybuf):
    for grp in range(rows_ref.shape[0] // PERM_ROWS):
        _regroup_rows(rows_ref[grp * PERM_ROWS:(grp + 1) * PERM_ROWS, :].astype(BF16), ybuf, grp)
    _compress_from(ybuf, w1_ref, w2_ref, pe_ref, o_ref)


def compress_prompt(kv, w1bd, w2bd, pe4):
    b, t, _ = kv.shape
    nb = t // CMP_STRIDE
    return pl.pallas_call(
        _compress_body, grid=(b, 2),
        in_specs=[pl.BlockSpec((None, t, KV_WIDTH), lambda i, c: (i, 0, c)),
                  pl.BlockSpec((None, CMP_LEN, KV_WIDTH, KV_WIDTH), lambda i, c: (c, 0, 0, 0)),
                  pl.BlockSpec((None, KV_WIDTH, KV_WIDTH), lambda i, c: (c, 0, 0)),
                  pl.BlockSpec((None, CMP_LEN, KV_WIDTH), lambda i, c: (c, 0, 0))],
        out_specs=pl.BlockSpec((None, None, nb, KV_WIDTH), lambda i, c: (i, c, 0, 0)),
        out_shape=jax.ShapeDtypeStruct((b, 2, nb, KV_WIDTH), F32),
        scratch_shapes=[pltpu.VMEM((CMP_STRIDE, nb, KV_WIDTH), BF16)],
        compiler_params=_cparams(("parallel", "parallel")), name="compress_prompt",
    )(kv, w1bd, w2bd, pe4)


def _alibi_slopes():
    return [2.0 ** (-8.0 * (n + 1) / N_Q) for n in range(N_Q)]


SEL_CHUNK = 256


def _nsa_body(q_ref, ks_ref, kw_ref, vst_ref, vwt_ref, cmp_ref, gt_ref, bg_ref, o_ref,
              qe_sc, tot_sc, acc_sc, m_sc, l_sc, sel_sc, *, n_top, n_slc, kw_len):
    qb = pl.program_id(1)
    QB, CK = Q_BLOCK, SEL_CHUNK
    nq = GROUP * QB
    n_cmp = cmp_ref.shape[1]
    slopes = _alibi_slopes()
    lane_head = lax.broadcasted_iota(jnp.int32, (1, KV_WIDTH), 1) // HEAD_DIM
    t_row1 = qb * QB + lax.broadcasted_iota(jnp.int32, (1, QB), 1)
    t_row = jnp.concatenate([t_row1] * GROUP, axis=1)
    gates = jax.nn.sigmoid(gt_ref[...] + bg_ref[...])
    kc = cmp_ref[0].astype(BF16)
    vct = jnp.transpose(cmp_ref[1]).astype(BF16)
    key = lax.broadcasted_iota(jnp.int32, (CK, nq), 0)
    key_f = key.astype(F32)
    cmp_end = lax.broadcasted_iota(jnp.int32, (n_cmp, nq), 0) * CMP_STRIDE + (CMP_LEN - 1)
    mask_c = cmp_end <= t_row
    cmp_end_f = cmp_end.astype(F32)
    oj = lax.broadcasted_iota(jnp.int32, (128, n_cmp), 0)
    on = lax.broadcasted_iota(jnp.int32, (128, n_cmp), 1) * CMP_STRIDE
    overlap_t = ((on < oj * SLC_BLOCK + SLC_BLOCK) & (on + (CMP_LEN - 1) >= oj * SLC_BLOCK)
                 & (oj < n_slc)).astype(F32)
    blk = lax.broadcasted_iota(jnp.int32, (n_slc, 1), 0)
    cur = t_row1 // SLC_BLOCK
    forced = (blk == 0) | (blk == cur) | (blk == cur - 1)
    future = blk * SLC_BLOCK > t_row1

    def slope_row(g):
        return jnp.concatenate([jnp.full((1, QB), slopes[GROUP * g + r], F32) for r in range(GROUP)], axis=1)

    def gate_row(c, g):
        lo = c * N_Q + GROUP * g
        return jnp.concatenate([gates[lo + r:lo + r + 1, :] for r in range(GROUP)], axis=1)

    def head_rows(g):
        return slice(g * HEAD_DIM, (g + 1) * HEAD_DIM)


    for g in range(N_KV):
        gm = lane_head == g
        zero = jnp.zeros((QB, KV_WIDTH), BF16)
        qe = jnp.concatenate(
            [jnp.where(gm, q_ref[:, r * KV_WIDTH:(r + 1) * KV_WIDTH], zero) for r in range(GROUP)],
            axis=0) * (HEAD_DIM ** -0.5)
        qe_sc[g] = qe
        s = _dot_nt(kc, qe) + slope_row(g) * cmp_end_f
        s = jnp.where(mask_c, s, NEG)
        e = jnp.where(mask_c, jnp.exp(s - jnp.max(s, axis=0, keepdims=True)), 0.0)
        p_c = e / jnp.maximum(jnp.sum(e, axis=0, keepdims=True), 1e-30)
        tot_sc[g] = gate_row(0, g) * _dot(vct[head_rows(g)], p_c.astype(BF16))
        p_sum = p_c[:, 0:QB] + p_c[:, QB:2 * QB] + p_c[:, 2 * QB:3 * QB]
        imp = jnp.dot(overlap_t, p_sum, precision=HIGHEST, preferred_element_type=F32)[:n_slc]
        imp = jnp.where(forced, BIG, jnp.where(future, -BIG, imp))
        for j in range(n_slc):
            row_j = imp[j:j + 1, :]
            beats = (imp > row_j) | ((imp == row_j) & (blk < j))
            rank = jnp.sum(beats.astype(F32), axis=0, keepdims=True)
            sel_sc[g, j:j + 1, :] = (rank < n_top).astype(F32)
        m_sc[g] = jnp.full((1, nq), NEG, F32)
        l_sc[g] = jnp.zeros((1, nq), F32)
        acc_sc[g] = jnp.zeros((HEAD_DIM, nq), F32)

    def sel_chunk(c, carry):
        base = pl.multiple_of(c * CK, CK)
        k = ks_ref[pl.ds(base, CK), :]
        causal = key <= t_row - base
        for g in range(N_KV):
            picked = jnp.concatenate(
                [jnp.broadcast_to(sel_sc[g, pl.ds(c * (CK // SLC_BLOCK) + i, 1), :], (SLC_BLOCK, QB))
                 for i in range(CK // SLC_BLOCK)], axis=0)
            picked = jnp.concatenate([picked] * GROUP, axis=1)
            s = _dot_nt(k, qe_sc[g]) + slope_row(g) * key_f
            s = jnp.where((picked > 0.5) & causal, s, NEG)
            off = slope_row(g) * base.astype(F32)
            m_old = m_sc[g]
            m_new = jnp.maximum(m_old, jnp.max(s, axis=0, keepdims=True) + off)
            alpha = jnp.exp(m_old - m_new)
            p = jnp.exp(s - (m_new - off))
            l_sc[g] = alpha * l_sc[g] + jnp.sum(p, axis=0, keepdims=True)
            acc_sc[g] = alpha * acc_sc[g] + _dot(vst_ref[head_rows(g), pl.ds(base, CK)], p.astype(BF16))
            m_sc[g] = m_new
        return carry

    lax.fori_loop(0, (qb * QB + QB + CK - 1) // CK, sel_chunk, 0)

    nkw = kw_len
    start = pl.multiple_of(jnp.maximum(qb * QB + QB - nkw, 0), QB)
    key_w = lax.broadcasted_iota(jnp.int32, (nkw, nq), 0)
    rel = t_row - start
    mask_w = (key_w <= rel) & (key_w >= rel - WINDOW)
    key_w_f = key_w.astype(F32)
    k_w = kw_ref[pl.ds(start, nkw), :]
    for g in range(N_KV):
        o_s = acc_sc[g] / jnp.maximum(l_sc[g], 1e-30)
        s = _dot_nt(k_w, qe_sc[g]) + slope_row(g) * key_w_f
        s = jnp.where(mask_w, s, NEG)
        p = jnp.exp(s - jnp.max(s, axis=0, keepdims=True))
        o_w = (_dot(vwt_ref[head_rows(g), pl.ds(start, nkw)], p.astype(BF16))
               / jnp.sum(p, axis=0, keepdims=True))
        tot = tot_sc[g] + gate_row(1, g) * o_s + gate_row(2, g) * o_w
        for r in range(GROUP):
            o_ref[r * KV_WIDTH + g * HEAD_DIM:r * KV_WIDTH + (g + 1) * HEAD_DIM, :] = (
                tot[:, r * QB:(r + 1) * QB].astype(o_ref.dtype))


def nsa_prompt(q, kb, vt, cmp, g_t, b_gate, b, t):
    n_slc = t // SLC_BLOCK
    n_top = min(N_SEL, n_slc)
    nqb = t // Q_BLOCK
    nq = GROUP * Q_BLOCK
    return pl.pallas_call(
        functools.partial(_nsa_body, n_top=n_top, n_slc=n_slc, kw_len=min(WINDOW + Q_BLOCK, t)),
        grid=(b, nqb),
        in_specs=[pl.BlockSpec((Q_BLOCK, TOK_WIDTH), lambda i, j: (i * nqb + j, 0)),
                  pl.BlockSpec((t, KV_WIDTH), lambda i, j: (i, 0)),
                  pl.BlockSpec((t, KV_WIDTH), lambda i, j: (i, 1)),
                  pl.BlockSpec((KV_WIDTH, t), lambda i, j: (0, i)),
                  pl.BlockSpec((KV_WIDTH, t), lambda i, j: (1, i)),
                  pl.BlockSpec((None, 2, t // CMP_STRIDE, KV_WIDTH), lambda i, j: (i, 0, 0, 0)),
                  pl.BlockSpec((3 * N_Q, Q_BLOCK), lambda i, j: (0, i * nqb + j)),
                  pl.BlockSpec((3 * N_Q, 1), lambda i, j: (0, 0))],
        out_specs=pl.BlockSpec((TOK_WIDTH, Q_BLOCK), lambda i, j: (0, i * nqb + j)),
        out_shape=jax.ShapeDtypeStruct((TOK_WIDTH, b * t), BF16),
        scratch_shapes=[pltpu.VMEM((N_KV, nq, KV_WIDTH), BF16), pltpu.VMEM((N_KV, HEAD_DIM, nq), F32),
                        pltpu.VMEM((N_KV, HEAD_DIM, nq), F32), pltpu.VMEM((N_KV, 1, nq), F32),
                        pltpu.VMEM((N_KV, 1, nq), F32), pltpu.VMEM((N_KV, n_slc, Q_BLOCK), F32)],
        compiler_params=_cparams(("parallel", "arbitrary")), name="nsa_prompt",
    )(q, kb, kb, vt, vt, cmp, g_t, b_gate.reshape(-1, 1))


def _compress_paged_body(pt_ref, pool_ref, w1_ref, w2_ref, pe_ref, o_ref, raw, ybuf, sems,
                         *, layer, n_pages):
    page = raw.shape[2]
    pages_per_grp = PERM_ROWS // page
    n_grp = n_pages // pages_per_grp
    step = pl.program_id(0) * 2 + pl.program_id(1)
    n_steps = pl.num_programs(0) * 2

    def page_copy(stp, p):
        seq, typ = stp // 2, stp % 2
        return pltpu.make_async_copy(
            pool_ref.at[layer, pt_ref[seq * n_pages + p], pl.ds(typ * KV_WIDTH, KV_WIDTH), :],
            raw.at[p], sems.at[p // pages_per_grp])

    def start_all(stp):
        def body(p, carry):
            page_copy(stp, p).start()
            return carry
        lax.fori_loop(0, n_pages, body, 0)

    @pl.when(step == 0)
    def _():
        start_all(step)

    def regroup(grp, carry):
        for k in range(pages_per_grp):
            page_copy(step, grp * pages_per_grp + k).wait()
        x_t = jnp.concatenate([raw[grp * pages_per_grp + k] for k in range(pages_per_grp)], axis=1)
        _regroup_rows(x_t.astype(BF16), ybuf, grp, transposed=True)
        return carry

    lax.fori_loop(0, n_grp, regroup, 0)

    @pl.when(step + 1 < n_steps)
    def _():
        start_all(step + 1)

    _compress_from(ybuf, w1_ref, w2_ref, pe_ref, o_ref)


def compress_paged(pool, page_table, layer, w1bd, w2bd, pe4):
    b, n_pages = page_table.shape
    page = pool.shape[3]
    assert PERM_ROWS % page == 0 and n_pages % (PERM_ROWS // page) == 0
    nb = n_pages * page // CMP_STRIDE
    grid_spec = pltpu.PrefetchScalarGridSpec(
        num_scalar_prefetch=1, grid=(b, 2),
        in_specs=[pl.BlockSpec(memory_space=pl.ANY),
                  pl.BlockSpec((None, CMP_LEN, KV_WIDTH, KV_WIDTH), lambda i, c, pt: (c, 0, 0, 0)),
                  pl.BlockSpec((None, KV_WIDTH, KV_WIDTH), lambda i, c, pt: (c, 0, 0)),
                  pl.BlockSpec((None, CMP_LEN, KV_WIDTH), lambda i, c, pt: (c, 0, 0))],
        out_specs=pl.BlockSpec((None, None, nb, KV_WIDTH), lambda i, c, pt: (i, c, 0, 0)),
        scratch_shapes=[pltpu.VMEM((n_pages, KV_WIDTH, page), F32),
                        pltpu.VMEM((CMP_STRIDE, nb, KV_WIDTH), BF16),
                        pltpu.SemaphoreType.DMA((n_pages * page // PERM_ROWS,))])
    return pl.pallas_call(
        functools.partial(_compress_paged_body, layer=layer, n_pages=n_pages), grid_spec=grid_spec,
        out_shape=jax.ShapeDtypeStruct((b, 2, nb, KV_WIDTH), F32),
        compiler_params=_cparams(("arbitrary", "arbitrary")), name="compress_paged",
    )(page_table.reshape(-1), pool, w1bd, w2bd, pe4)


def _decode_queries(q_ref):
    lane_head = lax.broadcasted_iota(jnp.int32, (1, KV_WIDTH), 1) // HEAD_DIM
    row4 = lax.broadcasted_iota(jnp.int32, (N_KV, 1), 0)
    parts = []
    for r in range(GROUP):
        q_r = jnp.broadcast_to(q_ref[:, r * KV_WIDTH:(r + 1) * KV_WIDTH], (N_KV, KV_WIDTH))
        parts.append(jnp.where(lane_head == row4, q_r, 0.0))
    parts.append(jnp.zeros((16 - N_Q, KV_WIDTH), F32))
    qe = (jnp.concatenate(parts, axis=0) * (HEAD_DIM ** -0.5)).astype(BF16)
    row = lax.broadcasted_iota(jnp.int32, (16, 1), 0)
    slopes = _alibi_slopes()
    slope_col = jnp.zeros((16, 1), F32)
    for r in range(GROUP):
        for g in range(N_KV):
            slope_col = jnp.where(row == N_KV * r + g, slopes[GROUP * g + r], slope_col)
    return qe, slope_col, row % N_KV, lane_head


def _nsa_select_body(q_ref, cmp_ref, oc_ref, idx_ref, *, t_pos, n_slc, n_top):
    nc = cmp_ref.shape[1]
    nj = idx_ref.shape[1]
    qe, slope_col, _, _ = _decode_queries(q_ref)
    kc = cmp_ref[0].astype(BF16)
    vc = cmp_ref[1].astype(BF16)
    cmp_end = lax.broadcasted_iota(jnp.int32, (1, nc), 1) * CMP_STRIDE + (CMP_LEN - 1)
    mask = cmp_end <= t_pos
    s = _dot_nt(qe, kc) + slope_col * cmp_end.astype(F32)
    s = jnp.where(mask, s, NEG)
    e = jnp.where(mask, jnp.exp(s - jnp.max(s, axis=-1, keepdims=True)), 0.0)
    p = e / jnp.maximum(jnp.sum(e, axis=-1, keepdims=True), 1e-30)
    oc_ref[...] = _dot(p.astype(BF16), vc)
    p_sum = p[0:N_KV] + p[N_KV:2 * N_KV] + p[2 * N_KV:3 * N_KV]
    p_sum = jnp.concatenate([p_sum, jnp.zeros((8 - N_KV, nc), F32)], axis=0)
    on = lax.broadcasted_iota(jnp.int32, (nc, nj), 0) * CMP_STRIDE
    oj = lax.broadcasted_iota(jnp.int32, (nc, nj), 1) * SLC_BLOCK
    overlap = ((on < oj + SLC_BLOCK) & (on + (CMP_LEN - 1) >= oj)).astype(F32)
    imp = jnp.dot(p_sum, overlap, precision=HIGHEST, preferred_element_type=F32)
    j = lax.broadcasted_iota(jnp.int32, (8, nj), 1)
    cur = t_pos // SLC_BLOCK
    forced = (j == 0) | (j == cur) | (j == cur - 1)
    imp = jnp.where(forced, BIG, jnp.where(j * SLC_BLOCK > t_pos, -BIG, imp))
    imp = jnp.where(j < n_slc, imp, -jnp.inf)
    picked = jnp.zeros((8, nj), jnp.int32)
    for i in range(n_top):
        mx = jnp.max(imp, axis=-1, keepdims=True)
        arg = jnp.min(jnp.where(imp == mx, j, nj), axis=-1, keepdims=True)
        picked = jnp.where(j == i, arg, picked)
        imp = jnp.where(j == arg, -jnp.inf, imp)
    idx_ref[...] = picked


def nsa_select(q, cmp, t_pos, n_slc, n_top):
    b = q.shape[0]
    nc = cmp.shape[2]
    nj = -(-n_slc // 128) * 128
    return pl.pallas_call(
        functools.partial(_nsa_select_body, t_pos=t_pos, n_slc=n_slc, n_top=n_top), grid=(b,),
        in_specs=[pl.BlockSpec((None, 1, TOK_WIDTH), lambda i: (i, 0, 0)),
                  pl.BlockSpec((None, 2, nc, KV_WIDTH), lambda i: (i, 0, 0, 0))],
        out_specs=[pl.BlockSpec((None, 16, KV_WIDTH), lambda i: (i, 0, 0)),
                   pl.BlockSpec((None, 8, nj), lambda i: (i, 0, 0))],
        out_shape=[jax.ShapeDtypeStruct((b, 16, KV_WIDTH), F32),
                   jax.ShapeDtypeStruct((b, 8, nj), jnp.int32)],
        compiler_params=_cparams(("parallel",)), name="nsa_select",
    )(q, cmp)


def _nsa_decode_body(pt_ref, idx_ref, q_ref, kvn_ref, win_ref, oc_ref, gp_ref, bg_ref, pool_ref, o_ref,
                     blkbuf, sem, *, layer, t_pos, n_top, n_pages, per_page):
    n_past_blocks = n_pages * per_page
    b = pl.program_id(0)

    def blk_copy(k):
        bid = jnp.minimum(idx_ref[b * N_KV * n_top + k], n_past_blocks - 1)
        return pltpu.make_async_copy(
            pool_ref.at[layer, pt_ref[b * n_pages + bid // per_page], pl.ds(2 * KV_WIDTH, 2 * KV_WIDTH), :],
            blkbuf.at[k], sem.at[0])

    for k in range(N_KV * n_top):
        blk_copy(k).start()

    qe, slope_col, row_group, lane_head = _decode_queries(q_ref)
    qf = qe.astype(F32)
    kvn = kvn_ref[...]
    new = lambda c: kvn[:, c * KV_WIDTH:(c + 1) * KV_WIDTH]
    t_f = float(t_pos)

    def attend(k_t, v_t, mask, pos_f, k_new, v_new):
        s = _dot(qe, k_t.astype(BF16)) + slope_col * pos_f
        s = jnp.where(mask, s, NEG)
        s_new = (jnp.sum(qf * k_new.astype(BF16).astype(F32), axis=-1, keepdims=True) + slope_col * t_f)
        m = jnp.maximum(jnp.max(s, axis=-1, keepdims=True), s_new)
        p = jnp.where(mask, jnp.exp(s - m), 0.0)
        p_new = jnp.exp(s_new - m)
        l = jnp.sum(p, axis=-1, keepdims=True) + p_new
        return (_dot_nt(p.astype(BF16), v_t.astype(BF16)) + p_new * v_new) / l

    wbuf = win_ref.shape[1]
    pos_w = (t_pos - wbuf) + lax.broadcasted_iota(jnp.int32, (1, wbuf), 1)
    mask_w = (pos_w >= 0) & (t_pos - pos_w <= WINDOW)
    o_w = attend(win_ref[:KV_WIDTH, :], win_ref[KV_WIDTH:, :], mask_w, pos_w.astype(F32), new(4), new(5))

    for k in range(N_KV * n_top):
        blk_copy(k).wait()

    page = blkbuf.shape[2]
    o_s = jnp.zeros((16, KV_WIDTH), F32)
    key = lax.broadcasted_iota(jnp.int32, (1, n_top * page), 1)
    for g in range(N_KV):
        pos = jnp.full((1, n_top * page), t_pos + 1, jnp.int32)
        for i in range(n_top):
            bid = idx_ref[(b * N_KV + g) * n_top + i]
            base = jnp.where(bid < n_past_blocks, (bid // per_page) * page, t_pos + 1)
            in_blk = (key // page == i) & ((key % page) // SLC_BLOCK == bid % per_page)
            pos = jnp.where(in_blk, base + key % page, pos)
        k_t = jnp.concatenate([blkbuf[g * n_top + i, :KV_WIDTH, :] for i in range(n_top)], axis=1)
        v_t = jnp.concatenate([blkbuf[g * n_top + i, KV_WIDTH:, :] for i in range(n_top)], axis=1)
        o_g = attend(k_t, v_t, pos <= t_pos, pos.astype(F32), new(2), new(3))
        o_s = jnp.where(row_group == g, o_g, o_s)

    gates = jax.nn.sigmoid(gp_ref[...] + bg_ref[...])
    row = lax.broadcasted_iota(jnp.int32, (16, 1), 0)
    tot = jnp.zeros((16, KV_WIDTH), F32)
    for c, o in enumerate((oc_ref[...], o_s, o_w)):
        gcol = jnp.zeros((16, 1), F32)
        for r in range(GROUP):
            for g in range(N_KV):
                lane = c * N_Q + GROUP * g + r
                gcol = jnp.where(row == N_KV * r + g, gates[:, lane:lane + 1], gcol)
        tot = tot + gcol * o
    out = []
    for r in range(GROUP):
        h_r = jnp.zeros((1, KV_WIDTH), F32)
        for g in range(N_KV):
            h_r = jnp.where(lane_head == g, tot[N_KV * r + g:N_KV * r + g + 1], h_r)
        out.append(h_r)
    o_ref[...] = jnp.concatenate(out, axis=1)


def nsa_decode(pool, page_table, layer, idx, q, kv_new, win, o_c, g_pre, b_gate, t_pos, n_top):
    b, n_pages = page_table.shape
    page = pool.shape[3]
    per_page = page // SLC_BLOCK
    wbuf = win.shape[2]
    row = lambda w: pl.BlockSpec((None, 1, w), lambda s, pt, ix: (s, 0, 0))
    grid_spec = pltpu.PrefetchScalarGridSpec(
        num_scalar_prefetch=2, grid=(b,),
        in_specs=[row(TOK_WIDTH), row(6 * KV_WIDTH),
                  pl.BlockSpec((None, 2 * KV_WIDTH, wbuf), lambda s, pt, ix: (s, 0, 0)),
                  pl.BlockSpec((None, 16, KV_WIDTH), lambda s, pt, ix: (s, 0, 0)),
                  row(3 * N_Q),
                  pl.BlockSpec((1, 3 * N_Q), lambda s, pt, ix: (0, 0)),
                  pl.BlockSpec(memory_space=pl.ANY)],
        out_specs=row(TOK_WIDTH),
        scratch_shapes=[pltpu.VMEM((N_KV * n_top, 2 * KV_WIDTH, page), F32),
                        pltpu.SemaphoreType.DMA((1,))])
    idx_flat = idx[:, :N_KV, :n_top].reshape(-1)
    return pl.pallas_call(
        functools.partial(_nsa_decode_body, layer=layer, t_pos=t_pos, n_top=n_top, n_pages=n_pages,
                          per_page=per_page),
        grid_spec=grid_spec, out_shape=jax.ShapeDtypeStruct((b, 1, TOK_WIDTH), F32),
        compiler_params=_cparams(("arbitrary",)), name="nsa_decode",
    )(page_table.reshape(-1), idx_flat, q, kv_new, win, o_c, g_pre, b_gate.reshape(1, -1), pool)


def _rgd_perm():
    idx = []
    for r in range(GROUP):
        for g in range(N_KV):
            n = GROUP * g + r
            idx.extend(range(n * HEAD_DIM, (n + 1) * HEAD_DIM))
    return jnp.array(idx, jnp.int32)


def _cols_to_rgd(w):
    lead = w.shape[:-1]
    w = w.reshape(lead + (N_KV, GROUP, HEAD_DIM))
    return jnp.swapaxes(w, -3, -2).reshape(lead + (TOK_WIDTH,))


def _block_diag4(w):
    eye = jnp.eye(N_KV, dtype=w.dtype)
    out = jnp.einsum('ab,...ij->...aibj', eye, w)
    return out.reshape(w.shape[:-2] + (KV_WIDTH, KV_WIDTH))


MOE_TILE = 1024


def kernel(x_prompt, x_sample, cache_mem_k, cache_mem_v, state_mlstm_C, state_mlstm_n, state_mlstm_m,
           state_mlstm_conv, cache_nsa, cache_nsa_win, page_table, mem_prompt, norm_mix, norm_mem, norm_ffn,
           norm_final, w_mem_kv, w_in_a, b_gate_a, conv_a, head_norm_a, w_in_b, b_gate_b, w_cmp1, w_cmp2,
           pe_cmp, w_out, w_ffn_gu, w_ffn_d, w_router, w_moe_gu, w_moe_d):
    bp, seq, d = x_prompt.shape
    bs, dseq, _ = x_sample.shape
    assert dseq == 1 and norm_mix.shape[0] == 2 and w_in_a.shape[0] == 1 and w_in_b.shape[0] == 1
    n_mem = mem_prompt.shape[1]
    bf = lambda a: a.astype(BF16)
    xp = x_prompt.reshape(bp * seq, d).astype(F32)
    xs = x_sample.reshape(bs, d).astype(F32)
    mem = mem_prompt.reshape(bp * n_mem, d).astype(F32)

    def mem_kv(i):
        w = w_mem_kv[i]
        mk, mv = norm_proj(mem, norm_mem[i], [bf(w[:, :X_WIDTH]), bf(w[:, X_WIDTH:])], [False, False])
        return mk.reshape(bp, n_mem, X_WIDTH), mv.reshape(bp, n_mem, X_WIDTH)

    def mix_out(i, x, h, a, wh):
        return out_proj(x, h.reshape(x.shape[0], -1), a.reshape(x.shape[0], -1), wh, bf(w_out[i][TOK_WIDTH:]))

    wa = w_in_a[0]
    t3, t4 = 3 * TOK_WIDTH, 4 * TOK_WIDTH
    w0 = [bf(_pad_heads(wa[:, :TOK_WIDTH])), bf(_pad_heads(wa[:, TOK_WIDTH:2 * TOK_WIDTH])),
          bf(_pad_heads(wa[:, 2 * TOK_WIDTH:t3])), bf(_pad_heads(wa[:, t3:t4])),
          bf(wa[:, t4 + 2 * M_HEADS:]), wa[:, t4:t4 + 2 * M_HEADS]]
    hi0 = [False] * 5 + [True]
    cwq = _pad_heads(conv_a[0][:, :TOK_WIDTH])
    cwk = _pad_heads(conv_a[0][:, TOK_WIDTH:])
    head_g = _pad_heads(head_norm_a[0]).reshape(1, M_PAD_WIDTH)
    wh0 = w_out[0][:TOK_WIDTH].reshape(M_HEADS, M_HEAD_DIM, d)
    wh0 = bf(jnp.pad(wh0, ((0, 0), (0, M_HEAD_PAD - M_HEAD_DIM), (0, 0))).reshape(M_PAD_WIDTH, d))
    mk0, mv0 = mem_kv(0)

    qp, kp, vp, op, xqp, gp = norm_proj(xp, norm_mix[0], w0, hi0)
    r3 = lambda a: a.reshape(bp, seq, a.shape[-1])
    hp, c_p, n_p, m_p = mlstm_prompt(r3(qp), r3(kp), r3(vp), r3(op), r3(gp), cwq, cwk, b_gate_a[0], head_g)
    conv_p = jnp.concatenate([_unpad_heads(r3(qp)[:, seq - (CONV_W - 1):]),
                              _unpad_heads(r3(kp)[:, seq - (CONV_W - 1):])], axis=-1)
    ap = mem_attention(r3(xqp), mk0, mv0)
    xp = mix_out(0, xp, hp, ap, wh0)
    xp = ffn(xp, norm_ffn[0], bf(w_ffn_gu[0]), bf(w_ffn_d[0]))

    qs, ks_, vs, os_, xqs, gs = norm_proj(xs, norm_mix[0], w0, hi0)
    s3 = lambda a: a.reshape(bs, 1, a.shape[-1])
    hist = state_mlstm_conv[0].astype(F32)
    hs, c_s, n_s, m_s = mlstm_step(
        s3(qs), s3(ks_), s3(vs), s3(os_), s3(gs), _pad_heads(hist[..., :TOK_WIDTH]),
        _pad_heads(hist[..., TOK_WIDTH:]), state_mlstm_C[0].astype(F32), state_mlstm_n[0].astype(F32),
        state_mlstm_m[0].astype(F32), cwq, cwk, b_gate_a[0], head_g)
    qk_new = jnp.concatenate([_unpad_heads(s3(qs)), _unpad_heads(s3(ks_))], axis=-1)
    conv_s = jnp.concatenate([hist, qk_new], axis=1)[:, 1:]
    as_ = mem_attention(s3(xqs), cache_mem_k[0].reshape(bs, n_mem, X_WIDTH).astype(F32),
                        cache_mem_v[0].reshape(bs, n_mem, X_WIDTH).astype(F32))
    xs = mix_out(0, xs, hs, as_, wh0)
    xs = ffn(xs, norm_ffn[0], bf(w_ffn_gu[0]), bf(w_ffn_d[0]))

    wb = w_in_b[0]
    kv_end = TOK_WIDTH + 6 * KV_WIDTH
    w1 = [bf(_cols_to_rgd(wb[:, :TOK_WIDTH])), bf(wb[:, TOK_WIDTH:kv_end]), bf(wb[:, kv_end + 3 * N_Q:]),
          wb[:, kv_end:kv_end + 3 * N_Q]]
    hi1 = [False] * 3 + [True]
    wh1 = w_out[1][:TOK_WIDTH].reshape(N_KV, GROUP, HEAD_DIM, d)
    wh1 = bf(jnp.swapaxes(wh1, 0, 1).reshape(TOK_WIDTH, d))
    w1bd = bf(_block_diag4(w_cmp1[0].reshape(2, CMP_LEN, HEAD_DIM, HEAD_DIM)))
    w2bd = bf(_block_diag4(w_cmp2[0]))
    pe4 = jnp.tile(pe_cmp[0], (1, 1, N_KV))
    mk1, mv1 = mem_kv(1)

    v_cols = lambda c: wb[:, TOK_WIDTH + c * KV_WIDTH:TOK_WIDTH + (c + 1) * KV_WIDTH]
    wvt = bf(jnp.concatenate([v_cols(3), v_cols(5)], axis=1).T)
    q1, kv1, kb1, vt1, xq1, gt1 = nsa_proj(xp, norm_mix[1], w1[0], w1[1], wvt, w1[2], w1[3].T)
    cmp_p = compress_prompt(r3(kv1), w1bd, w2bd, pe4)
    hp1t = nsa_prompt(q1, kb1, vt1, cmp_p, gt1, b_gate_b[0], bp, seq)
    rows_p = r3(kv1)[:, :, :4 * KV_WIDTH].reshape(bp, seq, 4, N_KV, HEAD_DIM)
    wlen = min(WINDOW, seq)
    win_p = r3(kv1)[:, seq - wlen:, 4 * KV_WIDTH:].reshape(bp, wlen, 2, N_KV, HEAD_DIM)
    ap1 = mem_attention(r3(xq1), mk1, mv1)
    xp = out_proj_t(xp, hp1t, ap1.reshape(bp * seq, X_WIDTH), wh1, bf(w_out[1][TOK_WIDTH:]))

    n_pool, page = cache_nsa.shape[1], cache_nsa.shape[2]
    past_len = page_table.shape[1] * page
    wbuf = cache_nsa_win.shape[2]
    n_slc = -(-(past_len + 1) // SLC_BLOCK)
    n_top = min(N_SEL, n_slc)
    pool = jnp.transpose(cache_nsa.astype(F32), (0, 1, 3, 4, 5, 2)).reshape(
        cache_nsa.shape[0], n_pool, 4 * KV_WIDTH, page)
    q1s, kv1s, xq1s, g1s = norm_proj(xs, norm_mix[1], w1, hi1)
    cmp_s = compress_paged(pool, page_table, 0, w1bd, w2bd, pe4)
    oc_s, idx_s = nsa_select(s3(q1s), cmp_s, past_len, n_slc, n_top)
    win_old = cache_nsa_win[0].astype(F32)
    win_t = jnp.transpose(win_old, (0, 2, 3, 4, 1)).reshape(bs, 2 * KV_WIDTH, wbuf)
    hs1 = nsa_decode(pool, page_table, 0, idx_s, s3(q1s), s3(kv1s), win_t,
                     oc_s, s3(g1s), b_gate_b[0], past_len, n_top)
    rows_s = kv1s[:, :4 * KV_WIDTH].reshape(bs, 1, 4, N_KV, HEAD_DIM)
    win_new = kv1s[:, 4 * KV_WIDTH:].reshape(bs, 1, 2, N_KV, HEAD_DIM)
    win_s = jnp.concatenate([win_old, win_new], axis=1)[:, -wbuf:]
    as1 = mem_attention(s3(xq1s), cache_mem_k[1].reshape(bs, n_mem, X_WIDTH).astype(F32),
                        cache_mem_v[1].reshape(bs, n_mem, X_WIDTH).astype(F32))
    xs = mix_out(1, xs, hs1, as1, wh1)

    w_r = jnp.pad(w_router[0], ((0, 0), (0, 128 - N_EXPERTS)))
    w_gu, w_dn = bf(w_moe_gu[0]), bf(w_moe_d[0])

    def experts(x, tm):
        u, comb, rank, counts = router(x, norm_ffn[1], w_r, tm)
        return final_norm(x, moe(u, comb, rank, counts, w_gu, w_dn, tm), norm_final)

    y_prompt = experts(xp, min(MOE_TILE, bp * seq)).reshape(bp, seq, d)
    y_sample = experts(xs, bs).reshape(bs, 1, d)

    unflat = lambda a: a.reshape(bp, n_mem, X_HEADS, HEAD_DIM)
    return (y_prompt, y_sample,
            jnp.stack([unflat(mk0), unflat(mk1)]), jnp.stack([unflat(mv0), unflat(mv1)]),
            c_p[None], n_p[None], m_p[None, :, :, 0], conv_p[None], rows_p[None], win_p[None],
            c_s[None], n_s[None], m_s[None, :, :, 0], conv_s[None], rows_s[None], win_s[None])
```

```python
import functools

import jax
import jax.numpy as jnp
from jax import lax
from jax.experimental import pallas as pl
from jax.experimental.pallas import tpu as pltpu

F32 = jnp.float32
BF16 = jnp.bfloat16
HIGHEST = lax.Precision.HIGHEST

D_MODEL = 1024
HEAD_DIM = 64
X_HEADS = 4
X_WIDTH = X_HEADS * HEAD_DIM
TOK_WIDTH = D_MODEL - X_WIDTH
M_HEADS = 8
M_HEAD_DIM = TOK_WIDTH // M_HEADS
M_HEAD_PAD = 128
M_PAD_WIDTH = M_HEADS * M_HEAD_PAD
CONV_W = 4
N_Q = TOK_WIDTH // HEAD_DIM
N_KV = 4
GROUP = N_Q // N_KV
KV_WIDTH = N_KV * HEAD_DIM
CMP_STRIDE = 16
CMP_LEN = 2 * CMP_STRIDE
SLC_BLOCK = 64
N_SEL = 16
WINDOW = 512
Q_BLOCK = 128
D_FF = 2816
N_EXPERTS = 8
TOP_K = 2
E_FF = 7 * D_MODEL // 4
RMS_EPS = 1e-6
NEG = -1e30
BIG = 1e9

LANES = 128
MLSTM_CHUNK = 128
VMEM_LIMIT = 56 * 1024 * 1024


def _cparams(sem):
    return pltpu.CompilerParams(dimension_semantics=sem, vmem_limit_bytes=VMEM_LIMIT)


def _rms(x, g):
    return x * lax.rsqrt(jnp.mean(x * x, axis=-1, keepdims=True) + RMS_EPS) * g


def _dot(a, b):
    return jnp.dot(a, b, preferred_element_type=F32)


def _dot_nt(a, b):
    return lax.dot_general(a, b, (((1,), (1,)), ((), ())), preferred_element_type=F32)


def _dot_tn(a, b):
    return lax.dot_general(a, b, (((0,), (0,)), ((), ())), preferred_element_type=F32)


def _norm_proj_body(x_ref, g_ref, *refs, n_out, hi):
    w_refs, o_refs = refs[:n_out], refs[n_out:]
    u = _rms(x_ref[...], g_ref[...])
    ub = u.astype(BF16)
    for w_ref, o_ref, h in zip(w_refs, o_refs, hi):
        if h:
            o_ref[...] = jnp.dot(u, w_ref[...], precision=HIGHEST, preferred_element_type=F32)
        else:
            o_ref[...] = _dot(ub, w_ref[...])


def norm_proj(x, gamma, weights, hi):
    m, d = x.shape
    tm = min(512, m)
    n_out = len(weights)
    in_specs = [pl.BlockSpec((tm, d), lambda i: (i, 0)), pl.BlockSpec((1, d), lambda i: (0, 0))]
    in_specs += [pl.BlockSpec(w.shape, lambda i: (0, 0)) for w in weights]
    out_specs = [pl.BlockSpec((tm, w.shape[1]), lambda i: (i, 0)) for w in weights]
    out_shape = [jax.ShapeDtypeStruct((m, w.shape[1]), F32) for w in weights]
    return pl.pallas_call(
        functools.partial(_norm_proj_body, n_out=n_out, hi=tuple(hi)),
        grid=(m // tm,), in_specs=in_specs, out_specs=out_specs, out_shape=out_shape,
        compiler_params=_cparams(("parallel",)), name="norm_proj",
    )(x, gamma.reshape(1, d), *weights)


def _mem_attn_body(q_ref, k_ref, v_ref, o_ref):
    q = q_ref[...] * (HEAD_DIM ** -0.5)
    rows = q.shape[0]
    if rows < 8:
        q = jnp.broadcast_to(q[0:1], (8, X_WIDTH))
    kb = k_ref[...].astype(BF16)
    vb = v_ref[...].astype(BF16)
    head = lax.broadcasted_iota(jnp.int32, (1, X_WIDTH), 1) // HEAD_DIM
    out = jnp.zeros(q.shape, F32)
    for h in range(X_HEADS):
        qh = jnp.where(head == h, q, 0.0).astype(BF16)
        s = _dot_nt(qh, kb)
        p = jnp.exp(s - jnp.max(s, axis=-1, keepdims=True))
        o = _dot(p.astype(BF16), vb) / jnp.sum(p, axis=-1, keepdims=True)
        out = jnp.where(head == h, o, out)
    o_ref[...] = out[:rows]


def mem_attention(q, mk, mv):
    b, t, w = q.shape
    n = mk.shape[1]
    tq = min(512, t)
    return pl.pallas_call(
        _mem_attn_body, grid=(b, t // tq),
        in_specs=[pl.BlockSpec((None, tq, w), lambda i, j: (i, j, 0)),
                  pl.BlockSpec((None, n, w), lambda i, j: (i, 0, 0)),
                  pl.BlockSpec((None, n, w), lambda i, j: (i, 0, 0))],
        out_specs=pl.BlockSpec((None, tq, w), lambda i, j: (i, j, 0)),
        out_shape=jax.ShapeDtypeStruct((b, t, w), F32),
        compiler_params=_cparams(("parallel", "parallel")), name="mem_attention",
    )(q, mk, mv)


def _out_proj_body(x_ref, h_ref, a_ref, wh_ref, wa_ref, o_ref):
    o_ref[...] = (x_ref[...] + _dot(h_ref[...].astype(BF16), wh_ref[...])
                  + _dot(a_ref[...].astype(BF16), wa_ref[...]))


def out_proj(x, h, a, wh, wa):
    m, d = x.shape
    tm = min(512, m)
    kh, ka = h.shape[1], a.shape[1]
    return pl.pallas_call(
        _out_proj_body, grid=(m // tm,),
        in_specs=[pl.BlockSpec((tm, d), lambda i: (i, 0)),
                  pl.BlockSpec((tm, kh), lambda i: (i, 0)),
                  pl.BlockSpec((tm, ka), lambda i: (i, 0)),
                  pl.BlockSpec((kh, d), lambda i: (0, 0)),
                  pl.BlockSpec((ka, d), lambda i: (0, 0))],
        out_specs=pl.BlockSpec((tm, d), lambda i: (i, 0)),
        out_shape=jax.ShapeDtypeStruct((m, d), F32),
        compiler_params=_cparams(("parallel",)), name="out_proj",
    )(x, h, a, wh, wa)


def _out_proj_t_body(x_ref, ht_ref, a_ref, wh_ref, wa_ref, o_ref):
    o_ref[...] = (x_ref[...] + _dot_tn(ht_ref[...], wh_ref[...])
                  + _dot(a_ref[...].astype(BF16), wa_ref[...]))


def out_proj_t(x, ht, a, wh, wa):
    m, d = x.shape
    tm = min(512, m)
    kh, ka = ht.shape[0], a.shape[1]
    return pl.pallas_call(
        _out_proj_t_body, grid=(m // tm,),
        in_specs=[pl.BlockSpec((tm, d), lambda i: (i, 0)),
                  pl.BlockSpec((kh, tm), lambda i: (0, i)),
                  pl.BlockSpec((tm, ka), lambda i: (i, 0)),
                  pl.BlockSpec((kh, d), lambda i: (0, 0)),
                  pl.BlockSpec((ka, d), lambda i: (0, 0))],
        out_specs=pl.BlockSpec((tm, d), lambda i: (i, 0)),
        out_shape=jax.ShapeDtypeStruct((m, d), F32),
        compiler_params=_cparams(("parallel",)), name="out_proj_t",
    )(x, ht, a, wh, wa)


def _nsa_proj_body(x_ref, g_ref, wq_ref, wkv_ref, wvt_ref, wxq_ref, wgt_ref,
                   q_ref, kv_ref, kb_ref, vt_ref, xq_ref, gt_ref):
    u = _rms(x_ref[...], g_ref[...])
    ub = u.astype(BF16)
    q_ref[...] = _dot(ub, wq_ref[...]).astype(BF16)
    kv = _dot(ub, wkv_ref[...])
    kv_ref[...] = kv
    kb_ref[...] = jnp.concatenate([kv[:, 2 * KV_WIDTH:3 * KV_WIDTH], kv[:, 4 * KV_WIDTH:5 * KV_WIDTH]],
                                  axis=1).astype(BF16)
    vt_ref[...] = _dot_nt(wvt_ref[...], ub).astype(BF16)
    xq_ref[...] = _dot(ub, wxq_ref[...])
    gt_ref[...] = lax.dot_general(wgt_ref[...], u, (((1,), (1,)), ((), ())), precision=HIGHEST,
                                  preferred_element_type=F32)


def nsa_proj(x, gamma, wq, wkv, wvt, wxq, wgt):
    m, d = x.shape
    tm = min(512, m)
    const = lambda a: pl.BlockSpec(a.shape, lambda i: (0, 0))
    rows = lambda n: pl.BlockSpec((tm, n), lambda i: (i, 0))
    cols = lambda n: pl.BlockSpec((n, tm), lambda i: (0, i))
    ng = wgt.shape[0]
    return pl.pallas_call(
        _nsa_proj_body, grid=(m // tm,),
        in_specs=[rows(d), pl.BlockSpec((1, d), lambda i: (0, 0)),
                  const(wq), const(wkv), const(wvt), const(wxq), const(wgt)],
        out_specs=[rows(TOK_WIDTH), rows(6 * KV_WIDTH), rows(2 * KV_WIDTH), cols(2 * KV_WIDTH),
                   rows(X_WIDTH), cols(ng)],
        out_shape=[jax.ShapeDtypeStruct((m, TOK_WIDTH), BF16),
                   jax.ShapeDtypeStruct((m, 6 * KV_WIDTH), F32),
                   jax.ShapeDtypeStruct((m, 2 * KV_WIDTH), BF16),
                   jax.ShapeDtypeStruct((2 * KV_WIDTH, m), BF16),
                   jax.ShapeDtypeStruct((m, X_WIDTH), F32),
                   jax.ShapeDtypeStruct((ng, m), F32)],
        compiler_params=_cparams(("parallel",)), name="nsa_proj",
    )(x, gamma.reshape(1, d), wq, wkv, wvt, wxq, wgt)


def _ffn_body(x_ref, g_ref, wg_ref, wu_ref, wd_ref, o_ref, u_sc):
    @pl.when(pl.program_id(1) == 0)
    def _():
        x = x_ref[...]
        u_sc[...] = _rms(x, g_ref[...]).astype(BF16)
        o_ref[...] = x

    u = u_sc[...]
    hg = _dot(u, wg_ref[...])
    hu = _dot(u, wu_ref[...])
    hm = (hg * jax.nn.sigmoid(hg) * hu).astype(BF16)
    o_ref[...] += _dot(hm, wd_ref[...])


def ffn(x, gamma, w_gu, w_d):
    m, d = x.shape
    f = w_d.shape[0]
    tm = min(1024, m)
    tf = f // 2 if (f // 2) % LANES == 0 else LANES * 2
    nf = f // tf
    return pl.pallas_call(
        _ffn_body, grid=(m // tm, nf),
        in_specs=[pl.BlockSpec((tm, d), lambda i, j: (i, 0)),
                  pl.BlockSpec((1, d), lambda i, j: (0, 0)),
                  pl.BlockSpec((d, tf), lambda i, j: (0, j)),
                  pl.BlockSpec((d, tf), lambda i, j: (0, nf + j)),
                  pl.BlockSpec((tf, d), lambda i, j: (j, 0))],
        out_specs=pl.BlockSpec((tm, d), lambda i, j: (i, 0)),
        out_shape=jax.ShapeDtypeStruct((m, d), F32),
        scratch_shapes=[pltpu.VMEM((tm, d), BF16)],
        compiler_params=_cparams(("parallel", "arbitrary")), name="ffn",
    )(x, gamma.reshape(1, d), w_gu, w_gu, w_d)


def _router_body(x_ref, g_ref, wr_ref, u_ref, comb_ref, rank_ref, cnt_ref):
    tm = x_ref.shape[0]
    u = _rms(x_ref[...], g_ref[...])
    u_ref[...] = u.astype(BF16)
    logits = jnp.dot(u, wr_ref[...], precision=HIGHEST, preferred_element_type=F32)
    lane = lax.broadcasted_iota(jnp.int32, (tm, 128), 1)
    logits = jnp.where(lane < N_EXPERTS, logits, -jnp.inf)
    m1 = jnp.max(logits, axis=-1, keepdims=True)
    i1 = jnp.min(jnp.where(logits == m1, lane, 128), axis=-1, keepdims=True)
    rest = jnp.where(lane == i1, -jnp.inf, logits)
    m2 = jnp.max(rest, axis=-1, keepdims=True)
    i2 = jnp.min(jnp.where(rest == m2, lane, 128), axis=-1, keepdims=True)
    e2 = jnp.exp(m2 - m1)
    g1 = 1.0 / (1.0 + e2)
    g2 = e2 / (1.0 + e2)
    comb_ref[...] = jnp.where(lane == i1, g1, 0.0) + jnp.where(lane == i2, g2, 0.0)
    sel = ((lane == i1) | (lane == i2)).astype(F32)
    r = lax.broadcasted_iota(jnp.int32, (tm, tm), 0)
    c = lax.broadcasted_iota(jnp.int32, (tm, tm), 1)
    tril = (c < r).astype(BF16)
    rank = _dot(tril, sel.astype(BF16))
    rank_ref[...] = jnp.where(sel > 0, rank, -1.0)
    cnt_ref[...] = jnp.sum(sel, axis=0, keepdims=True).astype(jnp.int32)


def router(x, gamma, w_router_pad, tm):
    m, d = x.shape
    nt = m // tm
    return pl.pallas_call(
        _router_body, grid=(nt,),
        in_specs=[pl.BlockSpec((tm, d), lambda i: (i, 0)),
                  pl.BlockSpec((1, d), lambda i: (0, 0)),
                  pl.BlockSpec((d, 128), lambda i: (0, 0))],
        out_specs=[pl.BlockSpec((tm, d), lambda i: (i, 0)),
                   pl.BlockSpec((tm, 128), lambda i: (i, 0)),
                   pl.BlockSpec((tm, 128), lambda i: (i, 0)),
                   pl.BlockSpec((None, 1, 128), lambda i: (i, 0, 0))],
        out_shape=[jax.ShapeDtypeStruct((m, d), BF16),
                   jax.ShapeDtypeStruct((m, 128), F32),
                   jax.ShapeDtypeStruct((m, 128), F32),
                   jax.ShapeDtypeStruct((nt, 1, 128), jnp.int32)],
        compiler_params=_cparams(("parallel",)), name="router",
    )(x, gamma.reshape(1, d), w_router_pad)


def _moe_body(cnt_ref, u_ref, rcol_ref, rrow_ref, gcol_ref, wg_ref, wu_ref, wd_ref, o_ref, *, cap):
    i = pl.program_id(0)
    e = pl.program_id(1)
    tm = u_ref.shape[0]

    @pl.when(e == 0)
    def _():
        o_ref[...] = jnp.zeros(o_ref.shape, F32)

    cnt = cnt_ref[i * N_EXPERTS + e]
    n_chunks = (cnt + cap - 1) // cap

    def chunk(c, carry):
        base = (c * cap).astype(F32)
        slot_r = lax.broadcasted_iota(jnp.int32, (cap, tm), 0).astype(F32) + base
        gather = (rrow_ref[...] == slot_r).astype(BF16)
        xg = _dot(gather, u_ref[...]).astype(BF16)
        hg = _dot(xg, wg_ref[...])
        hu = _dot(xg, wu_ref[...])
        hm = (hg * jax.nn.sigmoid(hg) * hu).astype(BF16)
        y = _dot(hm, wd_ref[...]).astype(BF16)
        slot_c = lax.broadcasted_iota(jnp.int32, (tm, cap), 1).astype(F32) + base
        scatter = (rcol_ref[...] == slot_c).astype(BF16)
        o_ref[...] += gcol_ref[...] * _dot(scatter, y)
        return carry

    lax.fori_loop(0, n_chunks, chunk, 0)


def moe(u, comb, rank, counts, w_gu, w_d, tm):
    m, d = u.shape
    nt = m // tm
    cap = min(tm, max(16, -(-(tm * TOP_K // N_EXPERTS) * 9 // 8 // 16) * 16))
    ef = w_d.shape[1]
    comb_e = comb[:, :N_EXPERTS].T
    rank_e = rank[:, :N_EXPERTS].T
    rcol = rank_e.reshape(N_EXPERTS, m, 1)
    rrow = rank_e.reshape(N_EXPERTS, 1, m)
    gcol = comb_e.reshape(N_EXPERTS, m, 1)
    cnt = counts[:, 0, :N_EXPERTS].reshape(nt * N_EXPERTS)
    grid_spec = pltpu.PrefetchScalarGridSpec(
        num_scalar_prefetch=1, grid=(nt, N_EXPERTS),
        in_specs=[pl.BlockSpec((tm, d), lambda i, e, c: (i, 0)),
                  pl.BlockSpec((None, tm, 1), lambda i, e, c: (e, i, 0)),
                  pl.BlockSpec((None, 1, tm), lambda i, e, c: (e, 0, i)),
                  pl.BlockSpec((None, tm, 1), lambda i, e, c: (e, i, 0)),
                  pl.BlockSpec((None, d, ef), lambda i, e, c: (e, 0, 0)),
                  pl.BlockSpec((None, d, ef), lambda i, e, c: (e, 0, 1)),
                  pl.BlockSpec((None, ef, d), lambda i, e, c: (e, 0, 0))],
        out_specs=pl.BlockSpec((tm, d), lambda i, e, c: (i, 0)))
    return pl.pallas_call(
        functools.partial(_moe_body, cap=cap), grid_spec=grid_spec,
        out_shape=jax.ShapeDtypeStruct((m, d), F32),
        compiler_params=_cparams(("parallel", "arbitrary")), name="moe",
    )(cnt, u, rcol, rrow, gcol, w_gu, w_gu, w_d)


def _final_body(x_ref, y_ref, g_ref, o_ref):
    o_ref[...] = _rms(x_ref[...] + y_ref[...], g_ref[...])


def final_norm(x, y, gamma):
    m, d = x.shape
    tm = min(1024, m)
    return pl.pallas_call(
        _final_body, grid=(m // tm,),
        in_specs=[pl.BlockSpec((tm, d), lambda i: (i, 0)),
                  pl.BlockSpec((tm, d), lambda i: (i, 0)),
                  pl.BlockSpec((1, d), lambda i: (0, 0))],
        out_specs=pl.BlockSpec((tm, d), lambda i: (i, 0)),
        out_shape=jax.ShapeDtypeStruct((m, d), F32),
        compiler_params=_cparams(("parallel",)), name="final_norm",
    )(x, y, gamma.reshape(1, d))


def _log_sigmoid(x):
    return jnp.minimum(x, 0.0) - jnp.log(1.0 + jnp.exp(-jnp.abs(x)))


def _silu(x):
    return x * jax.nn.sigmoid(x)


def _mlstm_body(qp_ref, kp_ref, v_ref, o_ref, g_ref, gt_ref, cwq_ref, cwk_ref, bgr_ref, bgc_ref, hg_ref,
                h_ref, c_out, n_out, m_out, c_sc, n_sc, m_sc, tq_sc, tk_sc):
    j = pl.program_id(1)
    L = qp_ref.shape[0]
    P = M_HEAD_PAD

    @pl.when(j == 0)
    def _():
        c_sc[...] = jnp.zeros(c_sc.shape, F32)
        n_sc[...] = jnp.zeros(n_sc.shape, F32)
        m_sc[...] = jnp.zeros(m_sc.shape, F32)
        tq_sc[...] = jnp.zeros(tq_sc.shape, F32)
        tk_sc[...] = jnp.zeros(tk_sc.shape, F32)

    def conv(x_ref, tail_sc, cw_ref):
        x = x_ref[...]
        ext = jnp.concatenate([tail_sc[...], x], axis=0)
        y = cw_ref[3:4, :] * x
        for tap in range(CONV_W - 1):
            lo = 8 - (CONV_W - 1) + tap
            y = y + cw_ref[tap:tap + 1, :] * ext[lo:lo + L]
        tail_sc[...] = x[L - 8:L]
        return _silu(y)

    qc = conv(qp_ref, tq_sc, cwq_ref)
    kc = conv(kp_ref, tk_sc, cwk_ref) * (M_HEAD_DIM ** -0.5)

    g = g_ref[...] + bgr_ref[...]
    gt = gt_ref[...] + bgc_ref[...]
    ig_c, lf_c = g[:, :M_HEADS], _log_sigmoid(g[:, M_HEADS:])
    ig_r, lf_r = gt[:M_HEADS], _log_sigmoid(gt[M_HEADS:])
    row = lax.broadcasted_iota(jnp.int32, (L, L), 0)
    col = lax.broadcasted_iota(jnp.int32, (L, L), 1)
    causal = col <= row
    f_c = jnp.dot(causal.astype(F32), lf_c, precision=HIGHEST, preferred_element_type=F32)
    f_r = jnp.dot(lf_r, (row <= col).astype(F32), precision=HIGHEST, preferred_element_type=F32)

    for h in range(M_HEADS):
        sl = slice(h * P, (h + 1) * P)
        q, k, v = qc[:, sl], kc[:, sl], v_ref[:, sl]
        fc, fr = f_c[:, h:h + 1], f_r[h:h + 1, :]
        m_prev = m_sc[h:h + 1, 0:1]
        d = jnp.where(causal, fc - fr + ig_r[h:h + 1, :], NEG)
        a = fc + m_prev
        mt = jnp.maximum(a, jnp.max(d, axis=1, keepdims=True))
        wm = jnp.exp(d - mt)
        gcol = jnp.exp(a - mt)
        qb, kb, vb = q.astype(BF16), k.astype(BF16), v.astype(BF16)
        s = _dot_nt(qb, kb) * wm
        c_old = c_sc[h]
        n_old = n_sc[h:h + 1, :]
        num = _dot(s.astype(BF16), vb) + gcol * _dot_nt(qb, c_old.astype(BF16))
        den = jnp.sum(s, axis=1, keepdims=True) + gcol * jnp.sum(q * n_old, axis=1, keepdims=True)
        hc = num / jnp.maximum(jnp.abs(den), jnp.exp(-mt))
        og = jax.nn.sigmoid(o_ref[:, sl]) * hc
        ms = jnp.sum(og * og, axis=1, keepdims=True) * (1.0 / M_HEAD_DIM)
        h_ref[:, sl] = og * lax.rsqrt(ms + RMS_EPS) * hg_ref[:, sl]
        mt_last, gl = mt[L - 1:L], gcol[L - 1:L]
        wl = jnp.exp(fc[L - 1:L] - fc + ig_c[:, h:h + 1] - mt_last)
        c_sc[h] = gl * c_old + _dot_tn((v * wl).astype(BF16), kb)
        n_sc[h:h + 1, :] = gl * n_old + jnp.sum(wl * k, axis=0, keepdims=True)
        m_sc[h:h + 1, :] = jnp.broadcast_to(mt_last, (1, P))

    @pl.when(j == pl.num_programs(1) - 1)
    def _():
        for h in range(M_HEADS):
            c_out[h] = c_sc[h][:M_HEAD_DIM, :M_HEAD_DIM]
        n_out[...] = n_sc[:, :M_HEAD_DIM]
        m_out[...] = m_sc[...]


def mlstm_prompt(qp, kp, v, o, g, cwq, cwk, b_gate, head_g):
    b, t, w = qp.shape
    L = min(MLSTM_CHUNK, t)
    gt = jnp.swapaxes(g, 1, 2)
    seq = pl.BlockSpec((None, L, w), lambda i, j: (i, j, 0))
    const = lambda shape: pl.BlockSpec(shape, lambda i, j: (0,) * len(shape))
    return pl.pallas_call(
        _mlstm_body, grid=(b, t // L),
        in_specs=[seq, seq, seq, seq,
                  pl.BlockSpec((None, L, 2 * M_HEADS), lambda i, j: (i, j, 0)),
                  pl.BlockSpec((None, 2 * M_HEADS, L), lambda i, j: (i, 0, j)),
                  const((CONV_W, w)), const((CONV_W, w)), const((1, 2 * M_HEADS)), const((2 * M_HEADS, 1)),
                  const((1, w))],
        out_specs=[seq,
                   pl.BlockSpec((None, M_HEADS, M_HEAD_DIM, M_HEAD_DIM), lambda i, j: (i, 0, 0, 0)),
                   pl.BlockSpec((None, M_HEADS, M_HEAD_DIM), lambda i, j: (i, 0, 0)),
                   pl.BlockSpec((None, M_HEADS, M_HEAD_PAD), lambda i, j: (i, 0, 0))],
        out_shape=[jax.ShapeDtypeStruct((b, t, w), F32),
                   jax.ShapeDtypeStruct((b, M_HEADS, M_HEAD_DIM, M_HEAD_DIM), F32),
                   jax.ShapeDtypeStruct((b, M_HEADS, M_HEAD_DIM), F32),
                   jax.ShapeDtypeStruct((b, M_HEADS, M_HEAD_PAD), F32)],
        scratch_shapes=[pltpu.VMEM((M_HEADS, M_HEAD_PAD, M_HEAD_PAD), F32),
                        pltpu.VMEM((M_HEADS, M_HEAD_PAD), F32),
                        pltpu.VMEM((M_HEADS, M_HEAD_PAD), F32),
                        pltpu.VMEM((8, w), F32), pltpu.VMEM((8, w), F32)],
        compiler_params=_cparams(("parallel", "arbitrary")), name="mlstm_prompt",
    )(qp, kp, v, o, g, gt, cwq, cwk, b_gate.reshape(1, -1), b_gate.reshape(-1, 1), head_g)


def _mlstm_step_body(qp_ref, kp_ref, v_ref, o_ref, g_ref, hq_ref, hk_ref, c0_ref, n0_ref, m0_ref,
                     cwq_ref, cwk_ref, bgr_ref, hg_ref, h_ref, c_out, n_out, m_out):
    P, dh = M_HEAD_PAD, M_HEAD_DIM

    def conv(x_ref, hist_ref, cw_ref):
        y = cw_ref[3:4, :] * x_ref[...]
        for tap in range(CONV_W - 1):
            y = y + cw_ref[tap:tap + 1, :] * hist_ref[tap:tap + 1, :]
        return _silu(y)

    qc = conv(qp_ref, hq_ref, cwq_ref)
    kc = conv(kp_ref, hk_ref, cwk_ref) * (dh ** -0.5)
    g = g_ref[...] + bgr_ref[...]
    m0 = m0_ref[...]
    for h in range(M_HEADS):
        lo = h * P
        q, k, v = qc[:, lo:lo + dh], kc[:, lo:lo + dh], v_ref[:, lo:lo + dh]
        ig = g[:, h:h + 1]
        lf = _log_sigmoid(g[:, M_HEADS + h:M_HEADS + h + 1])
        a = lf + m0[h:h + 1, :]
        mt = jnp.maximum(a, ig)
        wv = jnp.exp(ig - mt)
        gv = jnp.exp(a - mt)
        qf, kf = q.astype(BF16).astype(F32), k.astype(BF16).astype(F32)
        qk = jnp.sum(qf * kf, axis=1, keepdims=True) * wv
        c_old, n_old = c0_ref[h], n0_ref[h:h + 1, :]
        q8 = jnp.broadcast_to(q, (8, dh)).astype(BF16)
        cq = _dot_nt(q8, c_old.astype(BF16))[0:1]
        num = qk * v + gv * cq
        den = qk + gv * jnp.sum(q * n_old, axis=1, keepdims=True)
        hc = num / jnp.maximum(jnp.abs(den), jnp.exp(-mt))
        og = jax.nn.sigmoid(o_ref[:, lo:lo + dh]) * hc
        ms = jnp.sum(og * og, axis=1, keepdims=True) * (1.0 / dh)
        hn = og * lax.rsqrt(ms + RMS_EPS) * hg_ref[:, lo:lo + dh]
        h_ref[:, lo:lo + P] = jnp.concatenate([hn, jnp.zeros((1, P - dh), F32)], axis=1)
        v_col = jnp.transpose(jnp.broadcast_to(v_ref[:, lo:lo + P], (8, P)))[:dh, 0:1]
        c_out[h] = gv * c_old + wv * (v_col * k)
        n_out[h:h + 1, :] = gv * n_old + wv * k
        m_out[h:h + 1, :] = jnp.broadcast_to(mt, (1, P))


def mlstm_step(qp, kp, v, o, g, hist_q, hist_k, c0, n0, m0, cwq, cwk, b_gate, head_g):
    b, _, w = qp.shape
    row = pl.BlockSpec((None, 1, w), lambda i: (i, 0, 0))
    const = lambda shape: pl.BlockSpec(shape, lambda i: (0,) * len(shape))
    cspec = pl.BlockSpec((None, M_HEADS, M_HEAD_DIM, M_HEAD_DIM), lambda i: (i, 0, 0, 0))
    nspec = pl.BlockSpec((None, M_HEADS, M_HEAD_DIM), lambda i: (i, 0, 0))
    return pl.pallas_call(
        _mlstm_step_body, grid=(b,),
        in_specs=[row, row, row, row,
                  pl.BlockSpec((None, 1, 2 * M_HEADS), lambda i: (i, 0, 0)),
                  pl.BlockSpec((None, CONV_W - 1, w), lambda i: (i, 0, 0)),
                  pl.BlockSpec((None, CONV_W - 1, w), lambda i: (i, 0, 0)),
                  cspec, nspec,
                  pl.BlockSpec((None, M_HEADS, 1), lambda i: (i, 0, 0)),
                  const((CONV_W, w)), const((CONV_W, w)), const((1, 2 * M_HEADS)), const((1, w))],
        out_specs=[row, cspec, nspec,
                   pl.BlockSpec((None, M_HEADS, M_HEAD_PAD), lambda i: (i, 0, 0))],
        out_shape=[jax.ShapeDtypeStruct((b, 1, w), F32),
                   jax.ShapeDtypeStruct(c0.shape, F32),
                   jax.ShapeDtypeStruct(n0.shape, F32),
                   jax.ShapeDtypeStruct((b, M_HEADS, M_HEAD_PAD), F32)],
        compiler_params=_cparams(("parallel",)), name="mlstm_step",
    )(qp, kp, v, o, g, hist_q, hist_k, c0, n0, m0.reshape(b, M_HEADS, 1), cwq, cwk,
      b_gate.reshape(1, -1), head_g)


def _pad_heads(a):
    lead = a.shape[:-1]
    a = a.reshape(lead + (M_HEADS, M_HEAD_DIM))
    a = jnp.pad(a, [(0, 0)] * len(lead) + [(0, 0), (0, M_HEAD_PAD - M_HEAD_DIM)])
    return a.reshape(lead + (M_PAD_WIDTH,))


def _unpad_heads(a):
    lead = a.shape[:-1]
    return a.reshape(lead + (M_HEADS, M_HEAD_PAD))[..., :M_HEAD_DIM].reshape(lead + (TOK_WIDTH,))


def _gelu_tanh(x):
    return x * (0.5 * (1.0 + jnp.tanh(0.7978845608028654 * (x + 0.044715 * (x * x * x)))))


def _compress_tail(u, v_next, pe_ref, w1_ref, w2_ref):
    bias = jnp.zeros((8, KV_WIDTH), F32)
    for p in range(CMP_LEN):
        pe8 = jnp.broadcast_to(pe_ref[p:p + 1, :], (8, KV_WIDTH)).astype(BF16)
        bias = bias + _dot(pe8, w1_ref[p])
    pre = u + v_next + bias[0:1]
    return _dot(_gelu_tanh(pre).astype(BF16), w2_ref[...])


PERM_ROWS = CMP_STRIDE * CMP_STRIDE


def _regroup_perm():
    r = lax.broadcasted_iota(jnp.int32, (PERM_ROWS, PERM_ROWS), 0)
    c = lax.broadcasted_iota(jnp.int32, (PERM_ROWS, PERM_ROWS), 1)
    return (c == (r % CMP_STRIDE) * CMP_STRIDE + r // CMP_STRIDE).astype(BF16)


def _regroup_rows(perm, x, ybuf, grp, transposed=False):
    y = (_dot_nt(perm, x) if transposed else _dot(perm, x)).astype(BF16)
    for p in range(CMP_STRIDE):
        ybuf[p, pl.ds(pl.multiple_of(grp * CMP_STRIDE, CMP_STRIDE), CMP_STRIDE), :] = (
            y[p * CMP_STRIDE:(p + 1) * CMP_STRIDE])


def _compress_from(rows_at, nb, w1_ref, w2_ref, pe_ref, o_ref):
    u = jnp.zeros((nb, KV_WIDTH), F32)
    v = jnp.zeros((nb, KV_WIDTH), F32)
    for p in range(CMP_STRIDE):
        x = rows_at(p)
        u = u + _dot(x, w1_ref[p])
        v = v + _dot(x, w1_ref[CMP_STRIDE + p])
    v_next = jnp.concatenate([v[1:], jnp.zeros((1, KV_WIDTH), F32)], axis=0)
    o_ref[...] = _compress_tail(u, v_next, pe_ref, w1_ref, w2_ref)


def _compress_body(rows_ref, w1_ref, w2_ref, pe_ref, o_ref, ybuf):
    perm = _regroup_perm()
    for grp in range(rows_ref.shape[0] // PERM_ROWS):
        _regroup_rows(perm, rows_ref[grp * PERM_ROWS:(grp + 1) * PERM_ROWS, :].astype(BF16), ybuf, grp)
    _compress_from(lambda p: ybuf[p], ybuf.shape[1], w1_ref, w2_ref, pe_ref, o_ref)


def compress_prompt(kv, w1bd, w2bd, pe4):
    b, t, _ = kv.shape
    nb = t // CMP_STRIDE
    return pl.pallas_call(
        _compress_body, grid=(b, 2),
        in_specs=[pl.BlockSpec((None, t, KV_WIDTH), lambda i, c: (i, 0, c)),
                  pl.BlockSpec((None, CMP_LEN, KV_WIDTH, KV_WIDTH), lambda i, c: (c, 0, 0, 0)),
                  pl.BlockSpec((None, KV_WIDTH, KV_WIDTH), lambda i, c: (c, 0, 0)),
                  pl.BlockSpec((None, CMP_LEN, KV_WIDTH), lambda i, c: (c, 0, 0))],
        out_specs=pl.BlockSpec((None, None, nb, KV_WIDTH), lambda i, c: (i, c, 0, 0)),
        out_shape=jax.ShapeDtypeStruct((b, 2, nb, KV_WIDTH), F32),
        scratch_shapes=[pltpu.VMEM((CMP_STRIDE, nb, KV_WIDTH), BF16)],
        compiler_params=_cparams(("parallel", "parallel")), name="compress_prompt",
    )(kv, w1bd, w2bd, pe4)


def _alibi_slopes():
    return [2.0 ** (-8.0 * (n + 1) / N_Q) for n in range(N_Q)]


SEL_CHUNK = 512


def _nsa_body(q_ref, ks_ref, kw_ref, vst_ref, vwt_ref, cmp_ref, gt_ref, bg_ref, o_ref,
              qe_sc, tot_sc, acc_sc, m_sc, l_sc, sel_sc, bias_sc, *, n_top, n_slc, kw_len):
    qb = pl.program_id(1)
    QB, CK = Q_BLOCK, SEL_CHUNK
    nq = GROUP * QB
    n_cmp = cmp_ref.shape[1]
    slopes = _alibi_slopes()
    lane_head = lax.broadcasted_iota(jnp.int32, (1, KV_WIDTH), 1) // HEAD_DIM
    t_row1 = qb * QB + lax.broadcasted_iota(jnp.int32, (1, QB), 1)
    t_row = jnp.concatenate([t_row1] * GROUP, axis=1)
    gates = jax.nn.sigmoid(gt_ref[...] + bg_ref[...])
    kc = cmp_ref[0].astype(BF16)
    vct = jnp.transpose(cmp_ref[1]).astype(BF16)
    key = lax.broadcasted_iota(jnp.int32, (CK, nq), 0)
    key_f = key.astype(F32)
    cmp_end = lax.broadcasted_iota(jnp.int32, (n_cmp, nq), 0) * CMP_STRIDE + (CMP_LEN - 1)
    mask_c = cmp_end <= t_row
    cmp_end_f = cmp_end.astype(F32)
    oj = lax.broadcasted_iota(jnp.int32, (128, n_cmp), 0)
    on = lax.broadcasted_iota(jnp.int32, (128, n_cmp), 1) * CMP_STRIDE
    overlap_t = ((on < oj * SLC_BLOCK + SLC_BLOCK) & (on + (CMP_LEN - 1) >= oj * SLC_BLOCK)
                 & (oj < n_slc)).astype(F32)
    blk = lax.broadcasted_iota(jnp.int32, (n_slc, 1), 0)
    cur = t_row1 // SLC_BLOCK
    forced = (blk == 0) | (blk == cur) | (blk == cur - 1)
    future = blk * SLC_BLOCK > t_row1

    def slope_row(g):
        return jnp.concatenate([jnp.full((1, QB), slopes[GROUP * g + r], F32) for r in range(GROUP)], axis=1)

    def gate_row(c, g):
        lo = c * N_Q + GROUP * g
        return jnp.concatenate([gates[lo + r:lo + r + 1, :] for r in range(GROUP)], axis=1)

    def head_rows(g):
        return slice(g * HEAD_DIM, (g + 1) * HEAD_DIM)


    for g in range(N_KV):
        gm = lane_head == g
        zero = jnp.zeros((QB, KV_WIDTH), BF16)
        qe = jnp.concatenate(
            [jnp.where(gm, q_ref[:, r * KV_WIDTH:(r + 1) * KV_WIDTH], zero) for r in range(GROUP)],
            axis=0) * (HEAD_DIM ** -0.5)
        qe_sc[g] = qe
        s = _dot_nt(kc, qe) + slope_row(g) * cmp_end_f
        s = jnp.where(mask_c, s, NEG)
        e = jnp.where(mask_c, jnp.exp(s - jnp.max(s, axis=0, keepdims=True)), 0.0)
        p_c = e / jnp.maximum(jnp.sum(e, axis=0, keepdims=True), 1e-30)
        tot_sc[g] = gate_row(0, g) * _dot(vct[head_rows(g)], p_c.astype(BF16))
        p_sum = p_c[:, 0:QB] + p_c[:, QB:2 * QB] + p_c[:, 2 * QB:3 * QB]
        imp = jnp.dot(overlap_t, p_sum, precision=HIGHEST, preferred_element_type=F32)[:n_slc]
        imp = jnp.where(forced, BIG, jnp.where(future, -BIG, imp))
        for j in range(n_slc):
            row_j = imp[j:j + 1, :]
            beats = (imp > row_j) | ((imp == row_j) & (blk < j))
            rank = jnp.sum(beats.astype(F32), axis=0, keepdims=True)
            sel_sc[g, j:j + 1, :] = jnp.where(rank < n_top, 0.0, NEG)
        m_sc[g] = jnp.full((1, nq), NEG, F32)
        l_sc[g] = jnp.zeros((1, nq), F32)
        acc_sc[g] = jnp.zeros((HEAD_DIM, nq), F32)
        bias_sc[g] = slope_row(g) * key_f

    def sel_chunk(c, diagonal):
        base = pl.multiple_of(c * CK, CK)
        k = ks_ref[pl.ds(base, CK), :]
        for g in range(N_KV):
            picked = jnp.concatenate(
                [jnp.broadcast_to(sel_sc[g, pl.ds(c * (CK // SLC_BLOCK) + i, 1), :], (SLC_BLOCK, QB))
                 for i in range(CK // SLC_BLOCK)], axis=0)
            picked = jnp.concatenate([picked] * GROUP, axis=1)
            s = _dot_nt(k, qe_sc[g]) + (bias_sc[g] + picked)
            if diagonal:
                s = jnp.where(key <= t_row - base, s, NEG)
            off = slope_row(g) * base.astype(F32)
            m_old = m_sc[g]
            m_new = jnp.maximum(m_old, jnp.max(s, axis=0, keepdims=True) + off)
            alpha = jnp.exp(m_old - m_new)
            p = jnp.exp(s - (m_new - off))
            l_sc[g] = alpha * l_sc[g] + jnp.sum(p, axis=0, keepdims=True)
            acc_sc[g] = alpha * acc_sc[g] + _dot(vst_ref[head_rows(g), pl.ds(base, CK)], p.astype(BF16))
            m_sc[g] = m_new

    def full_chunk(c, carry):
        sel_chunk(c, False)
        return carry

    n_full = (qb * QB) // CK
    lax.fori_loop(0, n_full, full_chunk, 0)
    sel_chunk(n_full, True)

    nkw = kw_len
    start = pl.multiple_of(jnp.maximum(qb * QB + QB - nkw, 0), QB)
    key_w = lax.broadcasted_iota(jnp.int32, (nkw, nq), 0)
    rel = t_row - start
    mask_w = (key_w <= rel) & (key_w >= rel - WINDOW)
    key_w_f = key_w.astype(F32)
    k_w = kw_ref[pl.ds(start, nkw), :]
    for g in range(N_KV):
        o_s = acc_sc[g] / jnp.maximum(l_sc[g], 1e-30)
        s = _dot_nt(k_w, qe_sc[g]) + slope_row(g) * key_w_f
        s = jnp.where(mask_w, s, NEG)
        p = jnp.exp(s - jnp.max(s, axis=0, keepdims=True))
        o_w = (_dot(vwt_ref[head_rows(g), pl.ds(start, nkw)], p.astype(BF16))
               / jnp.sum(p, axis=0, keepdims=True))
        tot = tot_sc[g] + gate_row(1, g) * o_s + gate_row(2, g) * o_w
        for r in range(GROUP):
            o_ref[r * KV_WIDTH + g * HEAD_DIM:r * KV_WIDTH + (g + 1) * HEAD_DIM, :] = (
                tot[:, r * QB:(r + 1) * QB].astype(o_ref.dtype))


def nsa_prompt(q, kb, vt, cmp, g_t, b_gate, b, t):
    n_slc = t // SLC_BLOCK
    n_top = min(N_SEL, n_slc)
    nqb = t // Q_BLOCK
    nq = GROUP * Q_BLOCK
    return pl.pallas_call(
        functools.partial(_nsa_body, n_top=n_top, n_slc=n_slc, kw_len=min(WINDOW + Q_BLOCK, t)),
        grid=(b, nqb),
        in_specs=[pl.BlockSpec((Q_BLOCK, TOK_WIDTH), lambda i, j: (i * nqb + j, 0)),
                  pl.BlockSpec((t, KV_WIDTH), lambda i, j: (i, 0)),
                  pl.BlockSpec((t, KV_WIDTH), lambda i, j: (i, 1)),
                  pl.BlockSpec((KV_WIDTH, t), lambda i, j: (0, i)),
                  pl.BlockSpec((KV_WIDTH, t), lambda i, j: (1, i)),
                  pl.BlockSpec((None, 2, t // CMP_STRIDE, KV_WIDTH), lambda i, j: (i, 0, 0, 0)),
                  pl.BlockSpec((3 * N_Q, Q_BLOCK), lambda i, j: (0, i * nqb + j)),
                  pl.BlockSpec((3 * N_Q, 1), lambda i, j: (0, 0))],
        out_specs=pl.BlockSpec((TOK_WIDTH, Q_BLOCK), lambda i, j: (0, i * nqb + j)),
        out_shape=jax.ShapeDtypeStruct((TOK_WIDTH, b * t), BF16),
        scratch_shapes=[pltpu.VMEM((N_KV, nq, KV_WIDTH), BF16), pltpu.VMEM((N_KV, HEAD_DIM, nq), F32),
                        pltpu.VMEM((N_KV, HEAD_DIM, nq), F32), pltpu.VMEM((N_KV, 1, nq), F32),
                        pltpu.VMEM((N_KV, 1, nq), F32), pltpu.VMEM((N_KV, n_slc, Q_BLOCK), F32),
                        pltpu.VMEM((N_KV, SEL_CHUNK, nq), F32)],
        compiler_params=_cparams(("parallel", "arbitrary")), name="nsa_prompt",
    )(q, kb, kb, vt, vt, cmp, g_t, b_gate.reshape(-1, 1))


def _compress_paged_body(pt_ref, pool_ref, w1_ref, w2_ref, pe_ref, o_ref, raw, ybuf, sems,
                         *, layer, n_pages):
    page = raw.shape[2]
    pages_per_grp = PERM_ROWS // page
    n_grp = n_pages // pages_per_grp
    step = pl.program_id(0) * 2 + pl.program_id(1)
    n_steps = pl.num_programs(0) * 2

    def page_copy(stp, p):
        seq, typ = stp // 2, stp % 2
        return pltpu.make_async_copy(
            pool_ref.at[layer, pt_ref[seq * n_pages + p], pl.ds(typ * KV_WIDTH, KV_WIDTH), :],
            raw.at[p], sems.at[p // pages_per_grp])

    def start_all(stp):
        def body(p, carry):
            page_copy(stp, p).start()
            return carry
        lax.fori_loop(0, n_pages, body, 0)

    @pl.when(step == 0)
    def _():
        start_all(step)

    perm = _regroup_perm()

    unroll = 4 if n_grp % 4 == 0 else 1

    def regroup(it, carry):
        for j in range(unroll * pages_per_grp):
            page_copy(step, it * unroll * pages_per_grp + j).wait()
        for j in range(unroll):
            grp = it * unroll + j
            x_t = jnp.concatenate([raw[grp * pages_per_grp + k] for k in range(pages_per_grp)], axis=1)
            _regroup_rows(perm, x_t.astype(BF16), ybuf, grp, transposed=True)
        return carry

    lax.fori_loop(0, n_grp // unroll, regroup, 0)

    @pl.when(step + 1 < n_steps)
    def _():
        start_all(step + 1)

    _compress_from(lambda p: ybuf[p], ybuf.shape[1], w1_ref, w2_ref, pe_ref, o_ref)


def compress_paged(pool, page_table, layer, w1bd, w2bd, pe4):
    b, n_pages = page_table.shape
    page = pool.shape[3]
    assert PERM_ROWS % page == 0 and n_pages % (PERM_ROWS // page) == 0
    nb = n_pages * page // CMP_STRIDE
    grid_spec = pltpu.PrefetchScalarGridSpec(
        num_scalar_prefetch=1, grid=(b, 2),
        in_specs=[pl.BlockSpec(memory_space=pl.ANY),
                  pl.BlockSpec((None, CMP_LEN, KV_WIDTH, KV_WIDTH), lambda i, c, pt: (c, 0, 0, 0)),
                  pl.BlockSpec((None, KV_WIDTH, KV_WIDTH), lambda i, c, pt: (c, 0, 0)),
                  pl.BlockSpec((None, CMP_LEN, KV_WIDTH), lambda i, c, pt: (c, 0, 0))],
        out_specs=pl.BlockSpec((None, None, nb, KV_WIDTH), lambda i, c, pt: (i, c, 0, 0)),
        scratch_shapes=[pltpu.VMEM((n_pages, KV_WIDTH, page), F32),
                        pltpu.VMEM((CMP_STRIDE, nb, KV_WIDTH), BF16),
                        pltpu.SemaphoreType.DMA((n_pages * page // PERM_ROWS,))])
    return pl.pallas_call(
        functools.partial(_compress_paged_body, layer=layer, n_pages=n_pages), grid_spec=grid_spec,
        out_shape=jax.ShapeDtypeStruct((b, 2, nb, KV_WIDTH), F32),
        compiler_params=_cparams(("arbitrary", "arbitrary")), name="compress_paged",
    )(page_table.reshape(-1), pool, w1bd, w2bd, pe4)


def _decode_queries(q_ref):
    lane_head = lax.broadcasted_iota(jnp.int32, (1, KV_WIDTH), 1) // HEAD_DIM
    row4 = lax.broadcasted_iota(jnp.int32, (N_KV, 1), 0)
    parts = []
    for r in range(GROUP):
        q_r = jnp.broadcast_to(q_ref[:, r * KV_WIDTH:(r + 1) * KV_WIDTH], (N_KV, KV_WIDTH))
        parts.append(jnp.where(lane_head == row4, q_r, 0.0))
    parts.append(jnp.zeros((16 - N_Q, KV_WIDTH), F32))
    qe = (jnp.concatenate(parts, axis=0) * (HEAD_DIM ** -0.5)).astype(BF16)
    row = lax.broadcasted_iota(jnp.int32, (16, 1), 0)
    slopes = _alibi_slopes()
    slope_col = jnp.zeros((16, 1), F32)
    for r in range(GROUP):
        for g in range(N_KV):
            slope_col = jnp.where(row == N_KV * r + g, slopes[GROUP * g + r], slope_col)
    return qe, slope_col, row % N_KV, lane_head


def _nsa_select_body(q_ref, cmp_ref, oc_ref, idx_ref, *, t_pos, n_slc, n_top):
    nc = cmp_ref.shape[1]
    nj = idx_ref.shape[1]
    qe, slope_col, _, _ = _decode_queries(q_ref)
    kc = cmp_ref[0].astype(BF16)
    vc = cmp_ref[1].astype(BF16)
    cmp_end = lax.broadcasted_iota(jnp.int32, (1, nc), 1) * CMP_STRIDE + (CMP_LEN - 1)
    mask = cmp_end <= t_pos
    s = _dot_nt(qe, kc) + slope_col * cmp_end.astype(F32)
    s = jnp.where(mask, s, NEG)
    e = jnp.where(mask, jnp.exp(s - jnp.max(s, axis=-1, keepdims=True)), 0.0)
    p = e / jnp.maximum(jnp.sum(e, axis=-1, keepdims=True), 1e-30)
    oc_ref[...] = _dot(p.astype(BF16), vc)
    p_sum = p[0:N_KV] + p[N_KV:2 * N_KV] + p[2 * N_KV:3 * N_KV]
    p_sum = jnp.concatenate([p_sum, jnp.zeros((8 - N_KV, nc), F32)], axis=0)
    on = lax.broadcasted_iota(jnp.int32, (nc, nj), 0) * CMP_STRIDE
    oj = lax.broadcasted_iota(jnp.int32, (nc, nj), 1) * SLC_BLOCK
    overlap = ((on < oj + SLC_BLOCK) & (on + (CMP_LEN - 1) >= oj)).astype(F32)
    imp = jnp.dot(p_sum, overlap, precision=HIGHEST, preferred_element_type=F32)
    j = lax.broadcasted_iota(jnp.int32, (8, nj), 1)
    cur = t_pos // SLC_BLOCK
    forced = (j == 0) | (j == cur) | (j == cur - 1)
    imp = jnp.where(forced, BIG, jnp.where(j * SLC_BLOCK > t_pos, -BIG, imp))
    imp = jnp.where(j < n_slc, imp, -jnp.inf)
    picked = jnp.zeros((8, nj), jnp.int32)
    for i in range(n_top):
        mx = jnp.max(imp, axis=-1, keepdims=True)
        arg = jnp.min(jnp.where(imp == mx, j, nj), axis=-1, keepdims=True)
        picked = jnp.where(j == i, arg, picked)
        imp = jnp.where(j == arg, -jnp.inf, imp)
    idx_ref[...] = picked


def nsa_select(q, cmp, t_pos, n_slc, n_top):
    b = q.shape[0]
    nc = cmp.shape[2]
    nj = -(-n_slc // 128) * 128
    return pl.pallas_call(
        functools.partial(_nsa_select_body, t_pos=t_pos, n_slc=n_slc, n_top=n_top), grid=(b,),
        in_specs=[pl.BlockSpec((None, 1, TOK_WIDTH), lambda i: (i, 0, 0)),
                  pl.BlockSpec((None, 2, nc, KV_WIDTH), lambda i: (i, 0, 0, 0))],
        out_specs=[pl.BlockSpec((None, 16, KV_WIDTH), lambda i: (i, 0, 0)),
                   pl.BlockSpec((None, 8, nj), lambda i: (i, 0, 0))],
        out_shape=[jax.ShapeDtypeStruct((b, 16, KV_WIDTH), F32),
                   jax.ShapeDtypeStruct((b, 8, nj), jnp.int32)],
        compiler_params=_cparams(("parallel",)), name="nsa_select",
    )(q, cmp)


def _nsa_decode_body(pt_ref, idx_ref, q_ref, kvn_ref, win_ref, oc_ref, gp_ref, bg_ref, pool_ref, o_ref,
                     blkbuf, sem, *, layer, t_pos, n_top, n_pages, per_page):
    n_past_blocks = n_pages * per_page
    b = pl.program_id(0)

    def blk_copy(k):
        bid = jnp.minimum(idx_ref[b * N_KV * n_top + k], n_past_blocks - 1)
        return pltpu.make_async_copy(
            pool_ref.at[layer, pt_ref[b * n_pages + bid // per_page], pl.ds(2 * KV_WIDTH, 2 * KV_WIDTH), :],
            blkbuf.at[k], sem.at[0])

    for k in range(N_KV * n_top):
        blk_copy(k).start()

    qe, slope_col, row_group, lane_head = _decode_queries(q_ref)
    qf = qe.astype(F32)
    kvn = kvn_ref[...]
    new = lambda c: kvn[:, c * KV_WIDTH:(c + 1) * KV_WIDTH]
    t_f = float(t_pos)

    def attend(k_t, v_t, mask, pos_f, k_new, v_new):
        s = _dot(qe, k_t.astype(BF16)) + slope_col * pos_f
        s = jnp.where(mask, s, NEG)
        s_new = (jnp.sum(qf * k_new.astype(BF16).astype(F32), axis=-1, keepdims=True) + slope_col * t_f)
        m = jnp.maximum(jnp.max(s, axis=-1, keepdims=True), s_new)
        p = jnp.where(mask, jnp.exp(s - m), 0.0)
        p_new = jnp.exp(s_new - m)
        l = jnp.sum(p, axis=-1, keepdims=True) + p_new
        return (_dot_nt(p.astype(BF16), v_t.astype(BF16)) + p_new * v_new) / l

    wbuf = win_ref.shape[1]
    pos_w = (t_pos - wbuf) + lax.broadcasted_iota(jnp.int32, (1, wbuf), 1)
    mask_w = (pos_w >= 0) & (t_pos - pos_w <= WINDOW)
    o_w = attend(win_ref[:KV_WIDTH, :], win_ref[KV_WIDTH:, :], mask_w, pos_w.astype(F32), new(4), new(5))

    for k in range(N_KV * n_top):
        blk_copy(k).wait()

    page = blkbuf.shape[2]
    o_s = jnp.zeros((16, KV_WIDTH), F32)
    key = lax.broadcasted_iota(jnp.int32, (1, n_top * page), 1)
    for g in range(N_KV):
        pos = jnp.full((1, n_top * page), t_pos + 1, jnp.int32)
        for i in range(n_top):
            bid = idx_ref[(b * N_KV + g) * n_top + i]
            base = jnp.where(bid < n_past_blocks, (bid // per_page) * page, t_pos + 1)
            in_blk = (key // page == i) & ((key % page) // SLC_BLOCK == bid % per_page)
            pos = jnp.where(in_blk, base + key % page, pos)
        k_t = jnp.concatenate([blkbuf[g * n_top + i, :KV_WIDTH, :] for i in range(n_top)], axis=1)
        v_t = jnp.concatenate([blkbuf[g * n_top + i, KV_WIDTH:, :] for i in range(n_top)], axis=1)
        o_g = attend(k_t, v_t, pos <= t_pos, pos.astype(F32), new(2), new(3))
        o_s = jnp.where(row_group == g, o_g, o_s)

    gates = jax.nn.sigmoid(gp_ref[...] + bg_ref[...])
    row = lax.broadcasted_iota(jnp.int32, (16, 1), 0)
    tot = jnp.zeros((16, KV_WIDTH), F32)
    for c, o in enumerate((oc_ref[...], o_s, o_w)):
        gcol = jnp.zeros((16, 1), F32)
        for r in range(GROUP):
            for g in range(N_KV):
                lane = c * N_Q + GROUP * g + r
                gcol = jnp.where(row == N_KV * r + g, gates[:, lane:lane + 1], gcol)
        tot = tot + gcol * o
    out = []
    for r in range(GROUP):
        h_r = jnp.zeros((1, KV_WIDTH), F32)
        for g in range(N_KV):
            h_r = jnp.where(lane_head == g, tot[N_KV * r + g:N_KV * r + g + 1], h_r)
        out.append(h_r)
    o_ref[...] = jnp.concatenate(out, axis=1)


def nsa_decode(pool, page_table, layer, idx, q, kv_new, win, o_c, g_pre, b_gate, t_pos, n_top):
    b, n_pages = page_table.shape
    page = pool.shape[3]
    per_page = page // SLC_BLOCK
    wbuf = win.shape[2]
    row = lambda w: pl.BlockSpec((None, 1, w), lambda s, pt, ix: (s, 0, 0))
    grid_spec = pltpu.PrefetchScalarGridSpec(
        num_scalar_prefetch=2, grid=(b,),
        in_specs=[row(TOK_WIDTH), row(6 * KV_WIDTH),
                  pl.BlockSpec((None, 2 * KV_WIDTH, wbuf), lambda s, pt, ix: (s, 0, 0)),
                  pl.BlockSpec((None, 16, KV_WIDTH), lambda s, pt, ix: (s, 0, 0)),
                  row(3 * N_Q),
                  pl.BlockSpec((1, 3 * N_Q), lambda s, pt, ix: (0, 0)),
                  pl.BlockSpec(memory_space=pl.ANY)],
        out_specs=row(TOK_WIDTH),
        scratch_shapes=[pltpu.VMEM((N_KV * n_top, 2 * KV_WIDTH, page), F32),
                        pltpu.SemaphoreType.DMA((1,))])
    idx_flat = idx[:, :N_KV, :n_top].reshape(-1)
    return pl.pallas_call(
        functools.partial(_nsa_decode_body, layer=layer, t_pos=t_pos, n_top=n_top, n_pages=n_pages,
                          per_page=per_page),
        grid_spec=grid_spec, out_shape=jax.ShapeDtypeStruct((b, 1, TOK_WIDTH), F32),
        compiler_params=_cparams(("arbitrary",)), name="nsa_decode",
    )(page_table.reshape(-1), idx_flat, q, kv_new, win, o_c, g_pre, b_gate.reshape(1, -1), pool)


def _rgd_perm():
    idx = []
    for r in range(GROUP):
        for g in range(N_KV):
            n = GROUP * g + r
            idx.extend(range(n * HEAD_DIM, (n + 1) * HEAD_DIM))
    return jnp.array(idx, jnp.int32)


def _cols_to_rgd(w):
    lead = w.shape[:-1]
    w = w.reshape(lead + (N_KV, GROUP, HEAD_DIM))
    return jnp.swapaxes(w, -3, -2).reshape(lead + (TOK_WIDTH,))


def _block_diag4(w):
    eye = jnp.eye(N_KV, dtype=w.dtype)
    out = jnp.einsum('ab,...ij->...aibj', eye, w)
    return out.reshape(w.shape[:-2] + (KV_WIDTH, KV_WIDTH))


MOE_TILE = 1024


def kernel(x_prompt, x_sample, cache_mem_k, cache_mem_v, state_mlstm_C, state_mlstm_n, state_mlstm_m,
           state_mlstm_conv, cache_nsa, cache_nsa_win, page_table, mem_prompt, norm_mix, norm_mem, norm_ffn,
           norm_final, w_mem_kv, w_in_a, b_gate_a, conv_a, head_norm_a, w_in_b, b_gate_b, w_cmp1, w_cmp2,
           pe_cmp, w_out, w_ffn_gu, w_ffn_d, w_router, w_moe_gu, w_moe_d):
    bp, seq, d = x_prompt.shape
    bs, dseq, _ = x_sample.shape
    assert dseq == 1 and norm_mix.shape[0] == 2 and w_in_a.shape[0] == 1 and w_in_b.shape[0] == 1
    n_mem = mem_prompt.shape[1]
    bf = lambda a: a.astype(BF16)
    xp = x_prompt.reshape(bp * seq, d).astype(F32)
    xs = x_sample.reshape(bs, d).astype(F32)
    mem = mem_prompt.reshape(bp * n_mem, d).astype(F32)

    def mem_kv(i):
        w = w_mem_kv[i]
        mk, mv = norm_proj(mem, norm_mem[i], [bf(w[:, :X_WIDTH]), bf(w[:, X_WIDTH:])], [False, False])
        return mk.reshape(bp, n_mem, X_WIDTH), mv.reshape(bp, n_mem, X_WIDTH)

    def mix_out(i, x, h, a, wh):
        return out_proj(x, h.reshape(x.shape[0], -1), a.reshape(x.shape[0], -1), wh, bf(w_out[i][TOK_WIDTH:]))

    wa = w_in_a[0]
    t3, t4 = 3 * TOK_WIDTH, 4 * TOK_WIDTH
    w0 = [bf(_pad_heads(wa[:, :TOK_WIDTH])), bf(_pad_heads(wa[:, TOK_WIDTH:2 * TOK_WIDTH])),
          bf(_pad_heads(wa[:, 2 * TOK_WIDTH:t3])), bf(_pad_heads(wa[:, t3:t4])),
          bf(wa[:, t4 + 2 * M_HEADS:]), wa[:, t4:t4 + 2 * M_HEADS]]
    hi0 = [False] * 5 + [True]
    cwq = _pad_heads(conv_a[0][:, :TOK_WIDTH])
    cwk = _pad_heads(conv_a[0][:, TOK_WIDTH:])
    head_g = _pad_heads(head_norm_a[0]).reshape(1, M_PAD_WIDTH)
    wh0 = w_out[0][:TOK_WIDTH].reshape(M_HEADS, M_HEAD_DIM, d)
    wh0 = bf(jnp.pad(wh0, ((0, 0), (0, M_HEAD_PAD - M_HEAD_DIM), (0, 0))).reshape(M_PAD_WIDTH, d))
    mk0, mv0 = mem_kv(0)

    qp, kp, vp, op, xqp, gp = norm_proj(xp, norm_mix[0], w0, hi0)
    r3 = lambda a: a.reshape(bp, seq, a.shape[-1])
    hp, c_p, n_p, m_p = mlstm_prompt(r3(qp), r3(kp), r3(vp), r3(op), r3(gp), cwq, cwk, b_gate_a[0], head_g)
    conv_p = jnp.concatenate([_unpad_heads(r3(qp)[:, seq - (CONV_W - 1):]),
                              _unpad_heads(r3(kp)[:, seq - (CONV_W - 1):])], axis=-1)
    ap = mem_attention(r3(xqp), mk0, mv0)
    xp = mix_out(0, xp, hp, ap, wh0)
    xp = ffn(xp, norm_ffn[0], bf(w_ffn_gu[0]), bf(w_ffn_d[0]))

    qs, ks_, vs, os_, xqs, gs = norm_proj(xs, norm_mix[0], w0, hi0)
    s3 = lambda a: a.reshape(bs, 1, a.shape[-1])
    hist = state_mlstm_conv[0].astype(F32)
    hs, c_s, n_s, m_s = mlstm_step(
        s3(qs), s3(ks_), s3(vs), s3(os_), s3(gs), _pad_heads(hist[..., :TOK_WIDTH]),
        _pad_heads(hist[..., TOK_WIDTH:]), state_mlstm_C[0].astype(F32), state_mlstm_n[0].astype(F32),
        state_mlstm_m[0].astype(F32), cwq, cwk, b_gate_a[0], head_g)
    qk_new = jnp.concatenate([_unpad_heads(s3(qs)), _unpad_heads(s3(ks_))], axis=-1)
    conv_s = jnp.concatenate([hist, qk_new], axis=1)[:, 1:]
    as_ = mem_attention(s3(xqs), cache_mem_k[0].reshape(bs, n_mem, X_WIDTH).astype(F32),
                        cache_mem_v[0].reshape(bs, n_mem, X_WIDTH).astype(F32))
    xs = mix_out(0, xs, hs, as_, wh0)
    xs = ffn(xs, norm_ffn[0], bf(w_ffn_gu[0]), bf(w_ffn_d[0]))

    wb = w_in_b[0]
    kv_end = TOK_WIDTH + 6 * KV_WIDTH
    w1 = [bf(_cols_to_rgd(wb[:, :TOK_WIDTH])), bf(wb[:, TOK_WIDTH:kv_end]), bf(wb[:, kv_end + 3 * N_Q:]),
          wb[:, kv_end:kv_end + 3 * N_Q]]
    hi1 = [False] * 3 + [True]
    wh1 = w_out[1][:TOK_WIDTH].reshape(N_KV, GROUP, HEAD_DIM, d)
    wh1 = bf(jnp.swapaxes(wh1, 0, 1).reshape(TOK_WIDTH, d))
    w1bd = bf(_block_diag4(w_cmp1[0].reshape(2, CMP_LEN, HEAD_DIM, HEAD_DIM)))
    w2bd = bf(_block_diag4(w_cmp2[0]))
    pe4 = jnp.tile(pe_cmp[0], (1, 1, N_KV))
    mk1, mv1 = mem_kv(1)

    v_cols = lambda c: wb[:, TOK_WIDTH + c * KV_WIDTH:TOK_WIDTH + (c + 1) * KV_WIDTH]
    wvt = bf(jnp.concatenate([v_cols(3), v_cols(5)], axis=1).T)
    q1, kv1, kb1, vt1, xq1, gt1 = nsa_proj(xp, norm_mix[1], w1[0], w1[1], wvt, w1[2], w1[3].T)
    cmp_p = compress_prompt(r3(kv1), w1bd, w2bd, pe4)
    hp1t = nsa_prompt(q1, kb1, vt1, cmp_p, gt1, b_gate_b[0], bp, seq)
    rows_p = r3(kv1)[:, :, :4 * KV_WIDTH].reshape(bp, seq, 4, N_KV, HEAD_DIM)
    wlen = min(WINDOW, seq)
    win_p = r3(kv1)[:, seq - wlen:, 4 * KV_WIDTH:].reshape(bp, wlen, 2, N_KV, HEAD_DIM)
    ap1 = mem_attention(r3(xq1), mk1, mv1)
    xp = out_proj_t(xp, hp1t, ap1.reshape(bp * seq, X_WIDTH), wh1, bf(w_out[1][TOK_WIDTH:]))

    n_pool, page = cache_nsa.shape[1], cache_nsa.shape[2]
    past_len = page_table.shape[1] * page
    wbuf = cache_nsa_win.shape[2]
    n_slc = -(-(past_len + 1) // SLC_BLOCK)
    n_top = min(N_SEL, n_slc)
    pool = jnp.transpose(cache_nsa.astype(F32), (0, 1, 3, 4, 5, 2)).reshape(
        cache_nsa.shape[0], n_pool, 4 * KV_WIDTH, page)
    q1s, kv1s, xq1s, g1s = norm_proj(xs, norm_mix[1], w1, hi1)
    cmp_s = compress_paged(pool, page_table, 0, w1bd, w2bd, pe4)
    oc_s, idx_s = nsa_select(s3(q1s), cmp_s, past_len, n_slc, n_top)
    win_old = cache_nsa_win[0].astype(F32)
    win_t = jnp.transpose(win_old, (0, 2, 3, 4, 1)).reshape(bs, 2 * KV_WIDTH, wbuf)
    hs1 = nsa_decode(pool, page_table, 0, idx_s, s3(q1s), s3(kv1s), win_t,
                     oc_s, s3(g1s), b_gate_b[0], past_len, n_top)
    rows_s = kv1s[:, :4 * KV_WIDTH].reshape(bs, 1, 4, N_KV, HEAD_DIM)
    win_new = kv1s[:, 4 * KV_WIDTH:].reshape(bs, 1, 2, N_KV, HEAD_DIM)
    win_s = jnp.concatenate([win_old, win_new], axis=1)[:, -wbuf:]
    as1 = mem_attention(s3(xq1s), cache_mem_k[1].reshape(bs, n_mem, X_WIDTH).astype(F32),
                        cache_mem_v[1].reshape(bs, n_mem, X_WIDTH).astype(F32))
    xs = mix_out(1, xs, hs1, as1, wh1)

    w_r = jnp.pad(w_router[0], ((0, 0), (0, 128 - N_EXPERTS)))
    w_gu, w_dn = bf(w_moe_gu[0]), bf(w_moe_d[0])

    def experts(x, tm):
        u, comb, rank, counts = router(x, norm_ffn[1], w_r, tm)
        return final_norm(x, moe(u, comb, rank, counts, w_gu, w_dn, tm), norm_final)

    y_prompt = experts(xp, min(MOE_TILE, bp * seq)).reshape(bp, seq, d)
    y_sample = experts(xs, bs).reshape(bs, 1, d)

    unflat = lambda a: a.reshape(bp, n_mem, X_HEADS, HEAD_DIM)
    return (y_prompt, y_sample,
            jnp.stack([unflat(mk0), unflat(mk1)]), jnp.stack([unflat(mv0), unflat(mv1)]),
            c_p[None], n_p[None], m_p[None, :, :, 0], conv_p[None], rows_p[None], win_p[None],
            c_s[None], n_s[None], m_s[None, :, :, 0], conv_s[None], rows_s[None], win_s[None])
```

```python
import functools

import jax
import jax.numpy as jnp
from jax import lax
from jax.experimental import pallas as pl
from jax.experimental.pallas import tpu as pltpu

F32 = jnp.float32
BF16 = jnp.bfloat16
HIGHEST = lax.Precision.HIGHEST

D_MODEL = 1024
HEAD_DIM = 64
X_HEADS = 4
X_WIDTH = X_HEADS * HEAD_DIM
TOK_WIDTH = D_MODEL - X_WIDTH
M_HEADS = 8
M_HEAD_DIM = TOK_WIDTH // M_HEADS
M_HEAD_PAD = 128
M_PAD_WIDTH = M_HEADS * M_HEAD_PAD
CONV_W = 4
N_Q = TOK_WIDTH // HEAD_DIM
N_KV = 4
GROUP = N_Q // N_KV
KV_WIDTH = N_KV * HEAD_DIM
CMP_STRIDE = 16
CMP_LEN = 2 * CMP_STRIDE
SLC_BLOCK = 64
N_SEL = 16
WINDOW = 512
Q_BLOCK = 128
D_FF = 2816
N_EXPERTS = 8
TOP_K = 2
E_FF = 7 * D_MODEL // 4
RMS_EPS = 1e-6
NEG = -1e30
BIG = 1e9

LANES = 128
MLSTM_CHUNK = 128
VMEM_LIMIT = 56 * 1024 * 1024


def _cparams(sem):
    return pltpu.CompilerParams(dimension_semantics=sem, vmem_limit_bytes=VMEM_LIMIT)


def _rms(x, g):
    return x * lax.rsqrt(jnp.mean(x * x, axis=-1, keepdims=True) + RMS_EPS) * g


def _dot(a, b):
    return jnp.dot(a, b, preferred_element_type=F32)


def _dot_nt(a, b):
    return lax.dot_general(a, b, (((1,), (1,)), ((), ())), preferred_element_type=F32)


def _dot_tn(a, b):
    return lax.dot_general(a, b, (((0,), (0,)), ((), ())), preferred_element_type=F32)


def _norm_proj_body(x_ref, g_ref, *refs, n_out, hi):
    w_refs, o_refs = refs[:n_out], refs[n_out:]
    u = _rms(x_ref[...], g_ref[...])
    ub = u.astype(BF16)
    for w_ref, o_ref, h in zip(w_refs, o_refs, hi):
        if h:
            o_ref[...] = jnp.dot(u, w_ref[...], precision=HIGHEST, preferred_element_type=F32)
        else:
            o_ref[...] = _dot(ub, w_ref[...])


def norm_proj(x, gamma, weights, hi):
    m, d = x.shape
    tm = min(512, m)
    n_out = len(weights)
    in_specs = [pl.BlockSpec((tm, d), lambda i: (i, 0)), pl.BlockSpec((1, d), lambda i: (0, 0))]
    in_specs += [pl.BlockSpec(w.shape, lambda i: (0, 0)) for w in weights]
    out_specs = [pl.BlockSpec((tm, w.shape[1]), lambda i: (i, 0)) for w in weights]
    out_shape = [jax.ShapeDtypeStruct((m, w.shape[1]), F32) for w in weights]
    return pl.pallas_call(
        functools.partial(_norm_proj_body, n_out=n_out, hi=tuple(hi)),
        grid=(m // tm,), in_specs=in_specs, out_specs=out_specs, out_shape=out_shape,
        compiler_params=_cparams(("parallel",)), name="norm_proj",
    )(x, gamma.reshape(1, d), *weights)


def _mem_attn_body(q_ref, k_ref, v_ref, o_ref):
    q = q_ref[...] * (HEAD_DIM ** -0.5)
    rows = q.shape[0]
    if rows < 8:
        q = jnp.broadcast_to(q[0:1], (8, X_WIDTH))
    kb = k_ref[...].astype(BF16)
    vb = v_ref[...].astype(BF16)
    head = lax.broadcasted_iota(jnp.int32, (1, X_WIDTH), 1) // HEAD_DIM
    out = jnp.zeros(q.shape, F32)
    for h in range(X_HEADS):
        qh = jnp.where(head == h, q, 0.0).astype(BF16)
        s = _dot_nt(qh, kb)
        p = jnp.exp(s - jnp.max(s, axis=-1, keepdims=True))
        o = _dot(p.astype(BF16), vb) / jnp.sum(p, axis=-1, keepdims=True)
        out = jnp.where(head == h, o, out)
    o_ref[...] = out[:rows]


def mem_attention(q, mk, mv):
    b, t, w = q.shape
    n = mk.shape[1]
    tq = min(512, t)
    return pl.pallas_call(
        _mem_attn_body, grid=(b, t // tq),
        in_specs=[pl.BlockSpec((None, tq, w), lambda i, j: (i, j, 0)),
                  pl.BlockSpec((None, n, w), lambda i, j: (i, 0, 0)),
                  pl.BlockSpec((None, n, w), lambda i, j: (i, 0, 0))],
        out_specs=pl.BlockSpec((None, tq, w), lambda i, j: (i, j, 0)),
        out_shape=jax.ShapeDtypeStruct((b, t, w), F32),
        compiler_params=_cparams(("parallel", "parallel")), name="mem_attention",
    )(q, mk, mv)


def _out_proj_body(x_ref, h_ref, a_ref, wh_ref, wa_ref, o_ref):
    o_ref[...] = (x_ref[...] + _dot(h_ref[...].astype(BF16), wh_ref[...])
                  + _dot(a_ref[...].astype(BF16), wa_ref[...]))


def out_proj(x, h, a, wh, wa):
    m, d = x.shape
    tm = min(512, m)
    kh, ka = h.shape[1], a.shape[1]
    return pl.pallas_call(
        _out_proj_body, grid=(m // tm,),
        in_specs=[pl.BlockSpec((tm, d), lambda i: (i, 0)),
                  pl.BlockSpec((tm, kh), lambda i: (i, 0)),
                  pl.BlockSpec((tm, ka), lambda i: (i, 0)),
                  pl.BlockSpec((kh, d), lambda i: (0, 0)),
                  pl.BlockSpec((ka, d), lambda i: (0, 0))],
        out_specs=pl.BlockSpec((tm, d), lambda i: (i, 0)),
        out_shape=jax.ShapeDtypeStruct((m, d), F32),
        compiler_params=_cparams(("parallel",)), name="out_proj",
    )(x, h, a, wh, wa)


def _out_proj_t_body(x_ref, ht_ref, a_ref, wh_ref, wa_ref, o_ref):
    o_ref[...] = (x_ref[...] + _dot_tn(ht_ref[...], wh_ref[...])
                  + _dot(a_ref[...].astype(BF16), wa_ref[...]))


def out_proj_t(x, ht, a, wh, wa):
    m, d = x.shape
    tm = min(512, m)
    kh, ka = ht.shape[0], a.shape[1]
    return pl.pallas_call(
        _out_proj_t_body, grid=(m // tm,),
        in_specs=[pl.BlockSpec((tm, d), lambda i: (i, 0)),
                  pl.BlockSpec((kh, tm), lambda i: (0, i)),
                  pl.BlockSpec((tm, ka), lambda i: (i, 0)),
                  pl.BlockSpec((kh, d), lambda i: (0, 0)),
                  pl.BlockSpec((ka, d), lambda i: (0, 0))],
        out_specs=pl.BlockSpec((tm, d), lambda i: (i, 0)),
        out_shape=jax.ShapeDtypeStruct((m, d), F32),
        compiler_params=_cparams(("parallel",)), name="out_proj_t",
    )(x, ht, a, wh, wa)


def _nsa_proj_body(x_ref, g_ref, wq_ref, wkv_ref, wvt_ref, wxq_ref, wgt_ref,
                   q_ref, rows_ref, win_ref, kb_ref, vt_ref, xq_ref, gt_ref):
    u = _rms(x_ref[...], g_ref[...])
    ub = u.astype(BF16)
    q_ref[...] = _dot(ub, wq_ref[...]).astype(BF16)
    kv = _dot(ub, wkv_ref[...])
    rows_ref[...] = kv[:, :4 * KV_WIDTH]
    win_ref[...] = kv[:, 4 * KV_WIDTH:]
    kb_ref[...] = jnp.concatenate([kv[:, 2 * KV_WIDTH:3 * KV_WIDTH], kv[:, 4 * KV_WIDTH:5 * KV_WIDTH]],
                                  axis=1).astype(BF16)
    vt_ref[...] = _dot_nt(wvt_ref[...], ub).astype(BF16)
    xq_ref[...] = _dot(ub, wxq_ref[...])
    gt_ref[...] = lax.dot_general(wgt_ref[...], u, (((1,), (1,)), ((), ())), precision=HIGHEST,
                                  preferred_element_type=F32)


def nsa_proj(x, gamma, wq, wkv, wvt, wxq, wgt):
    m, d = x.shape
    tm = min(512, m)
    const = lambda a: pl.BlockSpec(a.shape, lambda i: (0, 0))
    rows = lambda n: pl.BlockSpec((tm, n), lambda i: (i, 0))
    cols = lambda n: pl.BlockSpec((n, tm), lambda i: (0, i))
    ng = wgt.shape[0]
    return pl.pallas_call(
        _nsa_proj_body, grid=(m // tm,),
        in_specs=[rows(d), pl.BlockSpec((1, d), lambda i: (0, 0)),
                  const(wq), const(wkv), const(wvt), const(wxq), const(wgt)],
        out_specs=[rows(TOK_WIDTH), rows(4 * KV_WIDTH), rows(2 * KV_WIDTH), rows(2 * KV_WIDTH),
                   cols(2 * KV_WIDTH), rows(X_WIDTH), cols(ng)],
        out_shape=[jax.ShapeDtypeStruct((m, TOK_WIDTH), BF16),
                   jax.ShapeDtypeStruct((m, 4 * KV_WIDTH), F32),
                   jax.ShapeDtypeStruct((m, 2 * KV_WIDTH), F32),
                   jax.ShapeDtypeStruct((m, 2 * KV_WIDTH), BF16),
                   jax.ShapeDtypeStruct((2 * KV_WIDTH, m), BF16),
                   jax.ShapeDtypeStruct((m, X_WIDTH), F32),
                   jax.ShapeDtypeStruct((ng, m), F32)],
        compiler_params=_cparams(("parallel",)), name="nsa_proj",
    )(x, gamma.reshape(1, d), wq, wkv, wvt, wxq, wgt)


def _ffn_body(x_ref, g_ref, wg_ref, wu_ref, wd_ref, o_ref, u_sc):
    @pl.when(pl.program_id(1) == 0)
    def _():
        x = x_ref[...]
        u_sc[...] = _rms(x, g_ref[...]).astype(BF16)
        o_ref[...] = x

    u = u_sc[...]
    hg = _dot(u, wg_ref[...])
    hu = _dot(u, wu_ref[...])
    hm = (hg * jax.nn.sigmoid(hg) * hu).astype(BF16)
    o_ref[...] += _dot(hm, wd_ref[...])


def ffn(x, gamma, w_gu, w_d):
    m, d = x.shape
    f = w_d.shape[0]
    tm = min(1024, m)
    tf = f // 2 if (f // 2) % LANES == 0 else LANES * 2
    nf = f // tf
    return pl.pallas_call(
        _ffn_body, grid=(m // tm, nf),
        in_specs=[pl.BlockSpec((tm, d), lambda i, j: (i, 0)),
                  pl.BlockSpec((1, d), lambda i, j: (0, 0)),
                  pl.BlockSpec((d, tf), lambda i, j: (0, j)),
                  pl.BlockSpec((d, tf), lambda i, j: (0, nf + j)),
                  pl.BlockSpec((tf, d), lambda i, j: (j, 0))],
        out_specs=pl.BlockSpec((tm, d), lambda i, j: (i, 0)),
        out_shape=jax.ShapeDtypeStruct((m, d), F32),
        scratch_shapes=[pltpu.VMEM((tm, d), BF16)],
        compiler_params=_cparams(("parallel", "arbitrary")), name="ffn",
    )(x, gamma.reshape(1, d), w_gu, w_gu, w_d)


def _router_body(x_ref, g_ref, wr_ref, u_ref, comb_ref, rank_ref, cnt_ref):
    tm = x_ref.shape[0]
    u = _rms(x_ref[...], g_ref[...])
    u_ref[...] = u.astype(BF16)
    logits = jnp.dot(u, wr_ref[...], precision=HIGHEST, preferred_element_type=F32)
    lane = lax.broadcasted_iota(jnp.int32, (tm, 128), 1)
    logits = jnp.where(lane < N_EXPERTS, logits, -jnp.inf)
    m1 = jnp.max(logits, axis=-1, keepdims=True)
    i1 = jnp.min(jnp.where(logits == m1, lane, 128), axis=-1, keepdims=True)
    rest = jnp.where(lane == i1, -jnp.inf, logits)
    m2 = jnp.max(rest, axis=-1, keepdims=True)
    i2 = jnp.min(jnp.where(rest == m2, lane, 128), axis=-1, keepdims=True)
    e2 = jnp.exp(m2 - m1)
    g1 = 1.0 / (1.0 + e2)
    g2 = e2 / (1.0 + e2)
    comb_ref[...] = jnp.where(lane == i1, g1, 0.0) + jnp.where(lane == i2, g2, 0.0)
    sel = ((lane == i1) | (lane == i2)).astype(F32)
    r = lax.broadcasted_iota(jnp.int32, (tm, tm), 0)
    c = lax.broadcasted_iota(jnp.int32, (tm, tm), 1)
    tril = (c < r).astype(BF16)
    rank = _dot(tril, sel.astype(BF16))
    rank_ref[...] = jnp.where(sel > 0, rank, -1.0)
    cnt_ref[...] = jnp.sum(sel, axis=0, keepdims=True).astype(jnp.int32)


def router(x, gamma, w_router_pad, tm):
    m, d = x.shape
    nt = m // tm
    return pl.pallas_call(
        _router_body, grid=(nt,),
        in_specs=[pl.BlockSpec((tm, d), lambda i: (i, 0)),
                  pl.BlockSpec((1, d), lambda i: (0, 0)),
                  pl.BlockSpec((d, 128), lambda i: (0, 0))],
        out_specs=[pl.BlockSpec((tm, d), lambda i: (i, 0)),
                   pl.BlockSpec((tm, 128), lambda i: (i, 0)),
                   pl.BlockSpec((tm, 128), lambda i: (i, 0)),
                   pl.BlockSpec((None, 1, 128), lambda i: (i, 0, 0))],
        out_shape=[jax.ShapeDtypeStruct((m, d), BF16),
                   jax.ShapeDtypeStruct((m, 128), F32),
                   jax.ShapeDtypeStruct((m, 128), F32),
                   jax.ShapeDtypeStruct((nt, 1, 128), jnp.int32)],
        compiler_params=_cparams(("parallel",)), name="router",
    )(x, gamma.reshape(1, d), w_router_pad)


def _moe_body(cnt_ref, u_ref, rank_ref, rrow_ref, comb_ref, wg_ref, wu_ref, wd_ref, o_ref, rcol_sc, gcol_sc,
              *, cap, tm):
    i = pl.program_id(0)
    e = pl.program_id(1)
    n_sub = u_ref.shape[0] // tm

    @pl.when(e == 0)
    def _():
        o_ref[...] = jnp.zeros(o_ref.shape, F32)

    mine = lax.broadcasted_iota(jnp.int32, (tm, 128), 1) == e
    for s in range(n_sub):
        rows = pl.ds(s * tm, tm)
        rcol_sc[...] = jnp.sum(jnp.where(mine, rank_ref[rows, :], 0.0), axis=1, keepdims=True)
        gcol_sc[...] = jnp.sum(jnp.where(mine, comb_ref[rows, :], 0.0), axis=1, keepdims=True)
        cnt = cnt_ref[(i * n_sub + s) * N_EXPERTS + e]
        n_chunks = (cnt + cap - 1) // cap

        def chunk(c, carry, rows=rows):
            base = (c * cap).astype(F32)
            slot_r = lax.broadcasted_iota(jnp.int32, (cap, tm), 0).astype(F32) + base
            gather = (rrow_ref[:, rows] == slot_r).astype(BF16)
            xg = _dot(gather, u_ref[rows, :]).astype(BF16)
            hg = _dot(xg, wg_ref[...])
            hu = _dot(xg, wu_ref[...])
            hm = (hg * jax.nn.sigmoid(hg) * hu).astype(BF16)
            y = _dot(hm, wd_ref[...]).astype(BF16)
            slot_c = lax.broadcasted_iota(jnp.int32, (tm, cap), 1).astype(F32) + base
            scatter = (rcol_sc[...] == slot_c).astype(BF16)
            o_ref[rows, :] += gcol_sc[...] * _dot(scatter, y)
            return carry

        lax.fori_loop(0, n_chunks, chunk, 0)


def moe(u, comb, rank, counts, w_gu, w_d, tm):
    m, d = u.shape
    nt = m // tm
    cap = min(tm, max(16, -(-(tm * TOP_K // N_EXPERTS) * 9 // 8 // 16) * 16))
    ef = w_d.shape[1]
    rrow = rank[:, :N_EXPERTS].T.reshape(N_EXPERTS, 1, m)
    cnt = counts[:, 0, :N_EXPERTS].reshape(nt * N_EXPERTS)
    n_sub = MOE_TILES_PER_STEP if nt % MOE_TILES_PER_STEP == 0 else 1
    ts = n_sub * tm
    grid_spec = pltpu.PrefetchScalarGridSpec(
        num_scalar_prefetch=1, grid=(nt // n_sub, N_EXPERTS),
        in_specs=[pl.BlockSpec((ts, d), lambda i, e, c: (i, 0)),
                  pl.BlockSpec((ts, 128), lambda i, e, c: (i, 0)),
                  pl.BlockSpec((None, 1, ts), lambda i, e, c: (e, 0, i)),
                  pl.BlockSpec((ts, 128), lambda i, e, c: (i, 0)),
                  pl.BlockSpec((None, d, ef), lambda i, e, c: (e, 0, 0)),
                  pl.BlockSpec((None, d, ef), lambda i, e, c: (e, 0, 1)),
                  pl.BlockSpec((None, ef, d), lambda i, e, c: (e, 0, 0))],
        out_specs=pl.BlockSpec((ts, d), lambda i, e, c: (i, 0)),
        scratch_shapes=[pltpu.VMEM((tm, 1), F32), pltpu.VMEM((tm, 1), F32)])
    return pl.pallas_call(
        functools.partial(_moe_body, cap=cap, tm=tm), grid_spec=grid_spec,
        out_shape=jax.ShapeDtypeStruct((m, d), F32),
        compiler_params=_cparams(("parallel", "arbitrary")), name="moe",
    )(cnt, u, rank, rrow, comb, w_gu, w_gu, w_d)


def _final_body(x_ref, y_ref, g_ref, o_ref):
    o_ref[...] = _rms(x_ref[...] + y_ref[...], g_ref[...])


def final_norm(x, y, gamma):
    m, d = x.shape
    tm = min(1024, m)
    return pl.pallas_call(
        _final_body, grid=(m // tm,),
        in_specs=[pl.BlockSpec((tm, d), lambda i: (i, 0)),
                  pl.BlockSpec((tm, d), lambda i: (i, 0)),
                  pl.BlockSpec((1, d), lambda i: (0, 0))],
        out_specs=pl.BlockSpec((tm, d), lambda i: (i, 0)),
        out_shape=jax.ShapeDtypeStruct((m, d), F32),
        compiler_params=_cparams(("parallel",)), name="final_norm",
    )(x, y, gamma.reshape(1, d))


def _log_sigmoid(x):
    return jnp.minimum(x, 0.0) - jnp.log(1.0 + jnp.exp(-jnp.abs(x)))


def _silu(x):
    return x * jax.nn.sigmoid(x)


def _mlstm_body(qp_ref, kp_ref, v_ref, o_ref, g_ref, gt_ref, cwq_ref, cwk_ref, bgr_ref, bgc_ref, hg_ref,
                h_ref, c_out, n_out, m_out, c_sc, n_sc, m_sc, tq_sc, tk_sc):
    j = pl.program_id(1)
    L = qp_ref.shape[0]
    P = M_HEAD_PAD

    @pl.when(j == 0)
    def _():
        c_sc[...] = jnp.zeros(c_sc.shape, F32)
        n_sc[...] = jnp.zeros(n_sc.shape, F32)
        m_sc[...] = jnp.zeros(m_sc.shape, F32)
        tq_sc[...] = jnp.zeros(tq_sc.shape, F32)
        tk_sc[...] = jnp.zeros(tk_sc.shape, F32)

    def conv(x_ref, tail_sc, cw_ref):
        x = x_ref[...]
        ext = jnp.concatenate([tail_sc[...], x], axis=0)
        y = cw_ref[3:4, :] * x
        for tap in range(CONV_W - 1):
            lo = 8 - (CONV_W - 1) + tap
            y = y + cw_ref[tap:tap + 1, :] * ext[lo:lo + L]
        tail_sc[...] = x[L - 8:L]
        return _silu(y)

    qc = conv(qp_ref, tq_sc, cwq_ref)
    kc = conv(kp_ref, tk_sc, cwk_ref) * (M_HEAD_DIM ** -0.5)

    g = g_ref[...] + bgr_ref[...]
    gt = gt_ref[...] + bgc_ref[...]
    ig_c, lf_c = g[:, :M_HEADS], _log_sigmoid(g[:, M_HEADS:])
    ig_r, lf_r = gt[:M_HEADS], _log_sigmoid(gt[M_HEADS:])
    row = lax.broadcasted_iota(jnp.int32, (L, L), 0)
    col = lax.broadcasted_iota(jnp.int32, (L, L), 1)
    causal = col <= row
    f_c = jnp.dot(causal.astype(F32), lf_c, precision=HIGHEST, preferred_element_type=F32)
    f_r = jnp.dot(lf_r, (row <= col).astype(F32), precision=HIGHEST, preferred_element_type=F32)

    for h in range(M_HEADS):
        sl = slice(h * P, (h + 1) * P)
        q, k, v = qc[:, sl], kc[:, sl], v_ref[:, sl]
        fc, fr = f_c[:, h:h + 1], f_r[h:h + 1, :]
        m_prev = m_sc[h:h + 1, 0:1]
        d = jnp.where(causal, fc - fr + ig_r[h:h + 1, :], NEG)
        a = fc + m_prev
        mt = jnp.maximum(a, jnp.max(d, axis=1, keepdims=True))
        wm = jnp.exp(d - mt)
        gcol = jnp.exp(a - mt)
        qb, kb, vb = q.astype(BF16), k.astype(BF16), v.astype(BF16)
        s = _dot_nt(qb, kb) * wm
        c_old = c_sc[h]
        n_old = n_sc[h:h + 1, :]
        num = _dot(s.astype(BF16), vb) + gcol * _dot_nt(qb, c_old.astype(BF16))
        den = jnp.sum(s, axis=1, keepdims=True) + gcol * jnp.sum(q * n_old, axis=1, keepdims=True)
        hc = num / jnp.maximum(jnp.abs(den), jnp.exp(-mt))
        og = jax.nn.sigmoid(o_ref[:, sl]) * hc
        ms = jnp.sum(og * og, axis=1, keepdims=True) * (1.0 / M_HEAD_DIM)
        h_ref[:, sl] = og * lax.rsqrt(ms + RMS_EPS) * hg_ref[:, sl]
        mt_last, gl = mt[L - 1:L], gcol[L - 1:L]
        wl = jnp.exp(fc[L - 1:L] - fc + ig_c[:, h:h + 1] - mt_last)
        c_sc[h] = gl * c_old + _dot_tn((v * wl).astype(BF16), kb)
        n_sc[h:h + 1, :] = gl * n_old + jnp.sum(wl * k, axis=0, keepdims=True)
        m_sc[h:h + 1, :] = jnp.broadcast_to(mt_last, (1, P))

    @pl.when(j == pl.num_programs(1) - 1)
    def _():
        for h in range(M_HEADS):
            c_out[h] = c_sc[h][:M_HEAD_DIM, :M_HEAD_DIM]
        n_out[...] = n_sc[:, :M_HEAD_DIM]
        m_out[...] = m_sc[...]


def mlstm_prompt(qp, kp, v, o, g, cwq, cwk, b_gate, head_g):
    b, t, w = qp.shape
    L = min(MLSTM_CHUNK, t)
    gt = jnp.swapaxes(g, 1, 2)
    seq = pl.BlockSpec((None, L, w), lambda i, j: (i, j, 0))
    const = lambda shape: pl.BlockSpec(shape, lambda i, j: (0,) * len(shape))
    return pl.pallas_call(
        _mlstm_body, grid=(b, t // L),
        in_specs=[seq, seq, seq, seq,
                  pl.BlockSpec((None, L, 2 * M_HEADS), lambda i, j: (i, j, 0)),
                  pl.BlockSpec((None, 2 * M_HEADS, L), lambda i, j: (i, 0, j)),
                  const((CONV_W, w)), const((CONV_W, w)), const((1, 2 * M_HEADS)), const((2 * M_HEADS, 1)),
                  const((1, w))],
        out_specs=[seq,
                   pl.BlockSpec((None, M_HEADS, M_HEAD_DIM, M_HEAD_DIM), lambda i, j: (i, 0, 0, 0)),
                   pl.BlockSpec((None, M_HEADS, M_HEAD_DIM), lambda i, j: (i, 0, 0)),
                   pl.BlockSpec((None, M_HEADS, M_HEAD_PAD), lambda i, j: (i, 0, 0))],
        out_shape=[jax.ShapeDtypeStruct((b, t, w), F32),
                   jax.ShapeDtypeStruct((b, M_HEADS, M_HEAD_DIM, M_HEAD_DIM), F32),
                   jax.ShapeDtypeStruct((b, M_HEADS, M_HEAD_DIM), F32),
                   jax.ShapeDtypeStruct((b, M_HEADS, M_HEAD_PAD), F32)],
        scratch_shapes=[pltpu.VMEM((M_HEADS, M_HEAD_PAD, M_HEAD_PAD), F32),
                        pltpu.VMEM((M_HEADS, M_HEAD_PAD), F32),
                        pltpu.VMEM((M_HEADS, M_HEAD_PAD), F32),
                        pltpu.VMEM((8, w), F32), pltpu.VMEM((8, w), F32)],
        compiler_params=_cparams(("parallel", "arbitrary")), name="mlstm_prompt",
    )(qp, kp, v, o, g, gt, cwq, cwk, b_gate.reshape(1, -1), b_gate.reshape(-1, 1), head_g)


def _mlstm_step_body(qp_ref, kp_ref, v_ref, o_ref, g_ref, hq_ref, hk_ref, c0_ref, n0_ref, m0_ref,
                     cwq_ref, cwk_ref, bgr_ref, hg_ref, h_ref, c_out, n_out, m_out):
    P, dh = M_HEAD_PAD, M_HEAD_DIM

    def conv(x_ref, hist_ref, cw_ref):
        y = cw_ref[3:4, :] * x_ref[...]
        for tap in range(CONV_W - 1):
            y = y + cw_ref[tap:tap + 1, :] * hist_ref[tap:tap + 1, :]
        return _silu(y)

    qc = conv(qp_ref, hq_ref, cwq_ref)
    kc = conv(kp_ref, hk_ref, cwk_ref) * (dh ** -0.5)
    g = g_ref[...] + bgr_ref[...]
    m0 = m0_ref[...]
    for h in range(M_HEADS):
        lo = h * P
        q, k, v = qc[:, lo:lo + dh], kc[:, lo:lo + dh], v_ref[:, lo:lo + dh]
        ig = g[:, h:h + 1]
        lf = _log_sigmoid(g[:, M_HEADS + h:M_HEADS + h + 1])
        a = lf + m0[h:h + 1, :]
        mt = jnp.maximum(a, ig)
        wv = jnp.exp(ig - mt)
        gv = jnp.exp(a - mt)
        qf, kf = q.astype(BF16).astype(F32), k.astype(BF16).astype(F32)
        qk = jnp.sum(qf * kf, axis=1, keepdims=True) * wv
        c_old, n_old = c0_ref[h], n0_ref[h:h + 1, :]
        q8 = jnp.broadcast_to(q, (8, dh)).astype(BF16)
        cq = _dot_nt(q8, c_old.astype(BF16))[0:1]
        num = qk * v + gv * cq
        den = qk + gv * jnp.sum(q * n_old, axis=1, keepdims=True)
        hc = num / jnp.maximum(jnp.abs(den), jnp.exp(-mt))
        og = jax.nn.sigmoid(o_ref[:, lo:lo + dh]) * hc
        ms = jnp.sum(og * og, axis=1, keepdims=True) * (1.0 / dh)
        hn = og * lax.rsqrt(ms + RMS_EPS) * hg_ref[:, lo:lo + dh]
        h_ref[:, lo:lo + P] = jnp.concatenate([hn, jnp.zeros((1, P - dh), F32)], axis=1)
        v_col = jnp.transpose(jnp.broadcast_to(v_ref[:, lo:lo + P], (8, P)))[:dh, 0:1]
        c_out[h] = gv * c_old + wv * (v_col * k)
        n_out[h:h + 1, :] = gv * n_old + wv * k
        m_out[h:h + 1, :] = jnp.broadcast_to(mt, (1, P))


def mlstm_step(qp, kp, v, o, g, hist_q, hist_k, c0, n0, m0, cwq, cwk, b_gate, head_g):
    b, _, w = qp.shape
    row = pl.BlockSpec((None, 1, w), lambda i: (i, 0, 0))
    const = lambda shape: pl.BlockSpec(shape, lambda i: (0,) * len(shape))
    cspec = pl.BlockSpec((None, M_HEADS, M_HEAD_DIM, M_HEAD_DIM), lambda i: (i, 0, 0, 0))
    nspec = pl.BlockSpec((None, M_HEADS, M_HEAD_DIM), lambda i: (i, 0, 0))
    return pl.pallas_call(
        _mlstm_step_body, grid=(b,),
        in_specs=[row, row, row, row,
                  pl.BlockSpec((None, 1, 2 * M_HEADS), lambda i: (i, 0, 0)),
                  pl.BlockSpec((None, CONV_W - 1, w), lambda i: (i, 0, 0)),
                  pl.BlockSpec((None, CONV_W - 1, w), lambda i: (i, 0, 0)),
                  cspec, nspec,
                  pl.BlockSpec((None, M_HEADS, 1), lambda i: (i, 0, 0)),
                  const((CONV_W, w)), const((CONV_W, w)), const((1, 2 * M_HEADS)), const((1, w))],
        out_specs=[row, cspec, nspec,
                   pl.BlockSpec((None, M_HEADS, M_HEAD_PAD), lambda i: (i, 0, 0))],
        out_shape=[jax.ShapeDtypeStruct((b, 1, w), F32),
                   jax.ShapeDtypeStruct(c0.shape, F32),
                   jax.ShapeDtypeStruct(n0.shape, F32),
                   jax.ShapeDtypeStruct((b, M_HEADS, M_HEAD_PAD), F32)],
        compiler_params=_cparams(("parallel",)), name="mlstm_step",
    )(qp, kp, v, o, g, hist_q, hist_k, c0, n0, m0.reshape(b, M_HEADS, 1), cwq, cwk,
      b_gate.reshape(1, -1), head_g)


def _pad_heads(a):
    lead = a.shape[:-1]
    a = a.reshape(lead + (M_HEADS, M_HEAD_DIM))
    a = jnp.pad(a, [(0, 0)] * len(lead) + [(0, 0), (0, M_HEAD_PAD - M_HEAD_DIM)])
    return a.reshape(lead + (M_PAD_WIDTH,))


def _unpad_heads(a):
    lead = a.shape[:-1]
    return a.reshape(lead + (M_HEADS, M_HEAD_PAD))[..., :M_HEAD_DIM].reshape(lead + (TOK_WIDTH,))


def _gelu_tanh(x):
    return x * (0.5 * (1.0 + jnp.tanh(0.7978845608028654 * (x + 0.044715 * (x * x * x)))))


def _compress_tail(u, v_next, pe_ref, w1_ref, w2_ref):
    bias = jnp.zeros((8, KV_WIDTH), F32)
    for p in range(CMP_LEN):
        pe8 = jnp.broadcast_to(pe_ref[p:p + 1, :], (8, KV_WIDTH)).astype(BF16)
        bias = bias + _dot(pe8, w1_ref[p])
    pre = u + v_next + bias[0:1]
    return _dot(_gelu_tanh(pre).astype(BF16), w2_ref[...])


PERM_ROWS = CMP_STRIDE * CMP_STRIDE


def _regroup_perm():
    r = lax.broadcasted_iota(jnp.int32, (PERM_ROWS, PERM_ROWS), 0)
    c = lax.broadcasted_iota(jnp.int32, (PERM_ROWS, PERM_ROWS), 1)
    return (c == (r % CMP_STRIDE) * CMP_STRIDE + r // CMP_STRIDE).astype(BF16)


def _regroup_rows(perm, x, ybuf, grp, transposed=False):
    y = (_dot_nt(perm, x) if transposed else _dot(perm, x)).astype(BF16)
    for p in range(CMP_STRIDE):
        ybuf[p, pl.ds(pl.multiple_of(grp * CMP_STRIDE, CMP_STRIDE), CMP_STRIDE), :] = (
            y[p * CMP_STRIDE:(p + 1) * CMP_STRIDE])


def _compress_from(rows_at, nb, w1_ref, w2_ref, pe_ref, o_ref):
    u = jnp.zeros((nb, KV_WIDTH), F32)
    v = jnp.zeros((nb, KV_WIDTH), F32)
    for p in range(CMP_STRIDE):
        x = rows_at(p)
        u = u + _dot(x, w1_ref[p])
        v = v + _dot(x, w1_ref[CMP_STRIDE + p])
    v_next = jnp.concatenate([v[1:], jnp.zeros((1, KV_WIDTH), F32)], axis=0)
    o_ref[...] = _compress_tail(u, v_next, pe_ref, w1_ref, w2_ref)


def _compress_body(rows_ref, w1_ref, w2_ref, pe_ref, o_ref, ybuf):
    perm = _regroup_perm()
    for grp in range(rows_ref.shape[0] // PERM_ROWS):
        _regroup_rows(perm, rows_ref[grp * PERM_ROWS:(grp + 1) * PERM_ROWS, :].astype(BF16), ybuf, grp)
    _compress_from(lambda p: ybuf[p], ybuf.shape[1], w1_ref, w2_ref, pe_ref, o_ref)


def compress_prompt(kv, w1bd, w2bd, pe4):
    b, t, _ = kv.shape
    nb = t // CMP_STRIDE
    return pl.pallas_call(
        _compress_body, grid=(b, 2),
        in_specs=[pl.BlockSpec((None, t, KV_WIDTH), lambda i, c: (i, 0, c)),
                  pl.BlockSpec((None, CMP_LEN, KV_WIDTH, KV_WIDTH), lambda i, c: (c, 0, 0, 0)),
                  pl.BlockSpec((None, KV_WIDTH, KV_WIDTH), lambda i, c: (c, 0, 0)),
                  pl.BlockSpec((None, CMP_LEN, KV_WIDTH), lambda i, c: (c, 0, 0))],
        out_specs=pl.BlockSpec((None, None, nb, KV_WIDTH), lambda i, c: (i, c, 0, 0)),
        out_shape=jax.ShapeDtypeStruct((b, 2, nb, KV_WIDTH), F32),
        scratch_shapes=[pltpu.VMEM((CMP_STRIDE, nb, KV_WIDTH), BF16)],
        compiler_params=_cparams(("parallel", "parallel")), name="compress_prompt",
    )(kv, w1bd, w2bd, pe4)


def _alibi_slopes():
    return [2.0 ** (-8.0 * (n + 1) / N_Q) for n in range(N_Q)]


SEL_CHUNK = 512


def _nsa_body(q_ref, ks_ref, kw_ref, vst_ref, vwt_ref, cmp_ref, gt_ref, bg_ref, o_ref,
              qe_sc, tot_sc, acc_sc, m_sc, l_sc, sel_sc, bias_sc, *, n_top, n_slc, kw_len):
    qb = pl.program_id(1)
    QB, CK = Q_BLOCK, SEL_CHUNK
    nq = GROUP * QB
    n_cmp = cmp_ref.shape[1]
    slopes = _alibi_slopes()
    lane_head = lax.broadcasted_iota(jnp.int32, (1, KV_WIDTH), 1) // HEAD_DIM
    t_row1 = qb * QB + lax.broadcasted_iota(jnp.int32, (1, QB), 1)
    t_row = jnp.concatenate([t_row1] * GROUP, axis=1)
    gates = jax.nn.sigmoid(gt_ref[...] + bg_ref[...])
    kc = cmp_ref[0].astype(BF16)
    vct = jnp.transpose(cmp_ref[1]).astype(BF16)
    key = lax.broadcasted_iota(jnp.int32, (CK, nq), 0)
    key_f = key.astype(F32)
    cmp_end = lax.broadcasted_iota(jnp.int32, (n_cmp, nq), 0) * CMP_STRIDE + (CMP_LEN - 1)
    mask_c = cmp_end <= t_row
    cmp_end_f = cmp_end.astype(F32)
    oj = lax.broadcasted_iota(jnp.int32, (128, n_cmp), 0)
    on = lax.broadcasted_iota(jnp.int32, (128, n_cmp), 1) * CMP_STRIDE
    overlap_t = ((on < oj * SLC_BLOCK + SLC_BLOCK) & (on + (CMP_LEN - 1) >= oj * SLC_BLOCK)
                 & (oj < n_slc)).astype(F32)
    blk = lax.broadcasted_iota(jnp.int32, (n_slc, 1), 0)
    cur = t_row1 // SLC_BLOCK
    forced = (blk == 0) | (blk == cur) | (blk == cur - 1)
    future = blk * SLC_BLOCK > t_row1

    def slope_row(g):
        return jnp.concatenate([jnp.full((1, QB), slopes[GROUP * g + r], F32) for r in range(GROUP)], axis=1)

    def gate_row(c, g):
        lo = c * N_Q + GROUP * g
        return jnp.concatenate([gates[lo + r:lo + r + 1, :] for r in range(GROUP)], axis=1)

    def head_rows(g):
        return slice(g * HEAD_DIM, (g + 1) * HEAD_DIM)


    for g in range(N_KV):
        gm = lane_head == g
        zero = jnp.zeros((QB, KV_WIDTH), BF16)
        qe = jnp.concatenate(
            [jnp.where(gm, q_ref[:, r * KV_WIDTH:(r + 1) * KV_WIDTH], zero) for r in range(GROUP)],
            axis=0) * (HEAD_DIM ** -0.5)
        qe_sc[g] = qe
        s = _dot_nt(kc, qe) + slope_row(g) * cmp_end_f
        s = jnp.where(mask_c, s, NEG)
        e = jnp.where(mask_c, jnp.exp(s - jnp.max(s, axis=0, keepdims=True)), 0.0)
        p_c = e / jnp.maximum(jnp.sum(e, axis=0, keepdims=True), 1e-30)
        tot_sc[g] = gate_row(0, g) * _dot(vct[head_rows(g)], p_c.astype(BF16))
        p_sum = p_c[:, 0:QB] + p_c[:, QB:2 * QB] + p_c[:, 2 * QB:3 * QB]
        imp = jnp.dot(overlap_t, p_sum, precision=HIGHEST, preferred_element_type=F32)[:n_slc]
        imp = jnp.where(forced, BIG, jnp.where(future, -BIG, imp))
        for j in range(n_slc):
            row_j = imp[j:j + 1, :]
            beats = (imp > row_j) | ((imp == row_j) & (blk < j))
            rank = jnp.sum(beats.astype(F32), axis=0, keepdims=True)
            sel_sc[g, j:j + 1, :] = jnp.where(rank < n_top, 0.0, NEG)
        m_sc[g] = jnp.full((1, nq), NEG, F32)
        l_sc[g] = jnp.zeros((1, nq), F32)
        acc_sc[g] = jnp.zeros((HEAD_DIM, nq), F32)
        bias_sc[g] = slope_row(g) * key_f

    def sel_chunk(c, diagonal):
        base = pl.multiple_of(c * CK, CK)
        k = ks_ref[pl.ds(base, CK), :]
        for g in range(N_KV):
            picked = jnp.concatenate(
                [jnp.broadcast_to(sel_sc[g, pl.ds(c * (CK // SLC_BLOCK) + i, 1), :], (SLC_BLOCK, QB))
                 for i in range(CK // SLC_BLOCK)], axis=0)
            picked = jnp.concatenate([picked] * GROUP, axis=1)
            s = _dot_nt(k, qe_sc[g]) + (bias_sc[g] + picked)
            if diagonal:
                s = jnp.where(key <= t_row - base, s, NEG)
            off = slope_row(g) * base.astype(F32)
            m_old = m_sc[g]
            m_new = jnp.maximum(m_old, jnp.max(s, axis=0, keepdims=True) + off)
            alpha = jnp.exp(m_old - m_new)
            p = jnp.exp(s - (m_new - off))
            l_sc[g] = alpha * l_sc[g] + jnp.sum(p, axis=0, keepdims=True)
            acc_sc[g] = alpha * acc_sc[g] + _dot(vst_ref[head_rows(g), pl.ds(base, CK)], p.astype(BF16))
            m_sc[g] = m_new

    def full_chunk(c, carry):
        sel_chunk(c, False)
        return carry

    n_full = (qb * QB) // CK
    lax.fori_loop(0, n_full, full_chunk, 0)
    sel_chunk(n_full, True)

    nkw = kw_len
    start = pl.multiple_of(jnp.maximum(qb * QB + QB - nkw, 0), QB)
    key_w = lax.broadcasted_iota(jnp.int32, (nkw, nq), 0)
    rel = t_row - start
    mask_w = (key_w <= rel) & (key_w >= rel - WINDOW)
    key_w_f = key_w.astype(F32)
    k_w = kw_ref[pl.ds(start, nkw), :]
    for g in range(N_KV):
        o_s = acc_sc[g] / jnp.maximum(l_sc[g], 1e-30)
        s = _dot_nt(k_w, qe_sc[g]) + slope_row(g) * key_w_f
        s = jnp.where(mask_w, s, NEG)
        p = jnp.exp(s - jnp.max(s, axis=0, keepdims=True))
        o_w = (_dot(vwt_ref[head_rows(g), pl.ds(start, nkw)], p.astype(BF16))
               / jnp.sum(p, axis=0, keepdims=True))
        tot = tot_sc[g] + gate_row(1, g) * o_s + gate_row(2, g) * o_w
        for r in range(GROUP):
            o_ref[r * KV_WIDTH + g * HEAD_DIM:r * KV_WIDTH + (g + 1) * HEAD_DIM, :] = (
                tot[:, r * QB:(r + 1) * QB].astype(o_ref.dtype))


def nsa_prompt(q, kb, vt, cmp, g_t, b_gate, b, t):
    n_slc = t // SLC_BLOCK
    n_top = min(N_SEL, n_slc)
    nqb = t // Q_BLOCK
    nq = GROUP * Q_BLOCK
    return pl.pallas_call(
        functools.partial(_nsa_body, n_top=n_top, n_slc=n_slc, kw_len=min(WINDOW + Q_BLOCK, t)),
        grid=(b, nqb),
        in_specs=[pl.BlockSpec((Q_BLOCK, TOK_WIDTH), lambda i, j: (i * nqb + j, 0)),
                  pl.BlockSpec((t, KV_WIDTH), lambda i, j: (i, 0)),
                  pl.BlockSpec((t, KV_WIDTH), lambda i, j: (i, 1)),
                  pl.BlockSpec((KV_WIDTH, t), lambda i, j: (0, i)),
                  pl.BlockSpec((KV_WIDTH, t), lambda i, j: (1, i)),
                  pl.BlockSpec((None, 2, t // CMP_STRIDE, KV_WIDTH), lambda i, j: (i, 0, 0, 0)),
                  pl.BlockSpec((3 * N_Q, Q_BLOCK), lambda i, j: (0, i * nqb + j)),
                  pl.BlockSpec((3 * N_Q, 1), lambda i, j: (0, 0))],
        out_specs=pl.BlockSpec((TOK_WIDTH, Q_BLOCK), lambda i, j: (0, i * nqb + j)),
        out_shape=jax.ShapeDtypeStruct((TOK_WIDTH, b * t), BF16),
        scratch_shapes=[pltpu.VMEM((N_KV, nq, KV_WIDTH), BF16), pltpu.VMEM((N_KV, HEAD_DIM, nq), F32),
                        pltpu.VMEM((N_KV, HEAD_DIM, nq), F32), pltpu.VMEM((N_KV, 1, nq), F32),
                        pltpu.VMEM((N_KV, 1, nq), F32), pltpu.VMEM((N_KV, n_slc, Q_BLOCK), F32),
                        pltpu.VMEM((N_KV, SEL_CHUNK, nq), F32)],
        compiler_params=_cparams(("parallel", "arbitrary")), name="nsa_prompt",
    )(q, kb, kb, vt, vt, cmp, g_t, b_gate.reshape(-1, 1))


def _compress_paged_body(pt_ref, pool_ref, w1_ref, w2_ref, pe_ref, o_ref, raw, ybuf, sems,
                         *, layer, n_pages):
    page = raw.shape[2]
    pages_per_grp = PERM_ROWS // page
    n_grp = n_pages // pages_per_grp
    step = pl.program_id(0) * 2 + pl.program_id(1)
    n_steps = pl.num_programs(0) * 2

    def page_copy(stp, p):
        seq, typ = stp // 2, stp % 2
        return pltpu.make_async_copy(
            pool_ref.at[layer, pt_ref[seq * n_pages + p], pl.ds(typ * KV_WIDTH, KV_WIDTH), :],
            raw.at[p], sems.at[p // pages_per_grp])

    def start_all(stp):
        def body(p, carry):
            page_copy(stp, p).start()
            return carry
        lax.fori_loop(0, n_pages, body, 0)

    @pl.when(step == 0)
    def _():
        start_all(step)

    perm = _regroup_perm()

    unroll = 4 if n_grp % 4 == 0 else 1

    def regroup(it, carry):
        for j in range(unroll * pages_per_grp):
            page_copy(step, it * unroll * pages_per_grp + j).wait()
        for j in range(unroll):
            grp = it * unroll + j
            x_t = jnp.concatenate([raw[grp * pages_per_grp + k] for k in range(pages_per_grp)], axis=1)
            _regroup_rows(perm, x_t.astype(BF16), ybuf, grp, transposed=True)
        return carry

    lax.fori_loop(0, n_grp // unroll, regroup, 0)

    @pl.when(step + 1 < n_steps)
    def _():
        start_all(step + 1)

    _compress_from(lambda p: ybuf[p], ybuf.shape[1], w1_ref, w2_ref, pe_ref, o_ref)


def compress_paged(pool, page_table, layer, w1bd, w2bd, pe4):
    b, n_pages = page_table.shape
    page = pool.shape[3]
    assert PERM_ROWS % page == 0 and n_pages % (PERM_ROWS // page) == 0
    nb = n_pages * page // CMP_STRIDE
    grid_spec = pltpu.PrefetchScalarGridSpec(
        num_scalar_prefetch=1, grid=(b, 2),
        in_specs=[pl.BlockSpec(memory_space=pl.ANY),
                  pl.BlockSpec((None, CMP_LEN, KV_WIDTH, KV_WIDTH), lambda i, c, pt: (c, 0, 0, 0)),
                  pl.BlockSpec((None, KV_WIDTH, KV_WIDTH), lambda i, c, pt: (c, 0, 0)),
                  pl.BlockSpec((None, CMP_LEN, KV_WIDTH), lambda i, c, pt: (c, 0, 0))],
        out_specs=pl.BlockSpec((None, None, nb, KV_WIDTH), lambda i, c, pt: (i, c, 0, 0)),
        scratch_shapes=[pltpu.VMEM((n_pages, KV_WIDTH, page), F32),
                        pltpu.VMEM((CMP_STRIDE, nb, KV_WIDTH), BF16),
                        pltpu.SemaphoreType.DMA((n_pages * page // PERM_ROWS,))])
    return pl.pallas_call(
        functools.partial(_compress_paged_body, layer=layer, n_pages=n_pages), grid_spec=grid_spec,
        out_shape=jax.ShapeDtypeStruct((b, 2, nb, KV_WIDTH), F32),
        compiler_params=_cparams(("arbitrary", "arbitrary")), name="compress_paged",
    )(page_table.reshape(-1), pool, w1bd, w2bd, pe4)


def _decode_queries(q_ref):
    lane_head = lax.broadcasted_iota(jnp.int32, (1, KV_WIDTH), 1) // HEAD_DIM
    row4 = lax.broadcasted_iota(jnp.int32, (N_KV, 1), 0)
    parts = []
    for r in range(GROUP):
        q_r = jnp.broadcast_to(q_ref[:, r * KV_WIDTH:(r + 1) * KV_WIDTH], (N_KV, KV_WIDTH))
        parts.append(jnp.where(lane_head == row4, q_r, 0.0))
    parts.append(jnp.zeros((16 - N_Q, KV_WIDTH), F32))
    qe = (jnp.concatenate(parts, axis=0) * (HEAD_DIM ** -0.5)).astype(BF16)
    row = lax.broadcasted_iota(jnp.int32, (16, 1), 0)
    slopes = _alibi_slopes()
    slope_col = jnp.zeros((16, 1), F32)
    for r in range(GROUP):
        for g in range(N_KV):
            slope_col = jnp.where(row == N_KV * r + g, slopes[GROUP * g + r], slope_col)
    return qe, slope_col, row % N_KV, lane_head


def _nsa_select_body(q_ref, cmp_ref, oc_ref, idx_ref, *, t_pos, n_slc, n_top):
    nc = cmp_ref.shape[1]
    nj = idx_ref.shape[1]
    qe, slope_col, _, _ = _decode_queries(q_ref)
    kc = cmp_ref[0].astype(BF16)
    vc = cmp_ref[1].astype(BF16)
    cmp_end = lax.broadcasted_iota(jnp.int32, (1, nc), 1) * CMP_STRIDE + (CMP_LEN - 1)
    mask = cmp_end <= t_pos
    s = _dot_nt(qe, kc) + slope_col * cmp_end.astype(F32)
    s = jnp.where(mask, s, NEG)
    e = jnp.where(mask, jnp.exp(s - jnp.max(s, axis=-1, keepdims=True)), 0.0)
    p = e / jnp.maximum(jnp.sum(e, axis=-1, keepdims=True), 1e-30)
    oc_ref[...] = _dot(p.astype(BF16), vc)
    p_sum = p[0:N_KV] + p[N_KV:2 * N_KV] + p[2 * N_KV:3 * N_KV]
    p_sum = jnp.concatenate([p_sum, jnp.zeros((8 - N_KV, nc), F32)], axis=0)
    oj = lax.broadcasted_iota(jnp.int32, (nj, nc), 0) * SLC_BLOCK
    on = lax.broadcasted_iota(jnp.int32, (nj, nc), 1) * CMP_STRIDE
    overlap_t = ((on < oj + SLC_BLOCK) & (on + (CMP_LEN - 1) >= oj)).astype(F32)
    imp = jnp.transpose(jnp.dot(overlap_t, jnp.transpose(p_sum), precision=HIGHEST,
                                preferred_element_type=F32))
    j = lax.broadcasted_iota(jnp.int32, (8, nj), 1)
    cur = t_pos // SLC_BLOCK
    forced = (j == 0) | (j == cur) | (j == cur - 1)
    imp = jnp.where(forced, BIG, jnp.where(j * SLC_BLOCK > t_pos, -BIG, imp))
    imp = jnp.where(j < n_slc, imp, -jnp.inf)
    picked = jnp.zeros((8, nj), jnp.int32)
    for i in range(n_top):
        mx = jnp.max(imp, axis=-1, keepdims=True)
        arg = jnp.min(jnp.where(imp == mx, j, nj), axis=-1, keepdims=True)
        picked = jnp.where(j == i, arg, picked)
        imp = jnp.where(j == arg, -jnp.inf, imp)
    idx_ref[...] = picked


def nsa_select(q, cmp, t_pos, n_slc, n_top):
    b = q.shape[0]
    nc = cmp.shape[2]
    nj = -(-n_slc // 128) * 128
    return pl.pallas_call(
        functools.partial(_nsa_select_body, t_pos=t_pos, n_slc=n_slc, n_top=n_top), grid=(b,),
        in_specs=[pl.BlockSpec((None, 1, TOK_WIDTH), lambda i: (i, 0, 0)),
                  pl.BlockSpec((None, 2, nc, KV_WIDTH), lambda i: (i, 0, 0, 0))],
        out_specs=[pl.BlockSpec((None, 16, KV_WIDTH), lambda i: (i, 0, 0)),
                   pl.BlockSpec((None, 8, nj), lambda i: (i, 0, 0))],
        out_shape=[jax.ShapeDtypeStruct((b, 16, KV_WIDTH), F32),
                   jax.ShapeDtypeStruct((b, 8, nj), jnp.int32)],
        compiler_params=_cparams(("parallel",)), name="nsa_select",
    )(q, cmp)


def _nsa_decode_body(pt_ref, idx_ref, q_ref, kvn_ref, win_ref, oc_ref, gp_ref, bg_ref, pool_ref, o_ref,
                     blkbuf, sem, *, layer, t_pos, n_top, n_pages, per_page):
    n_past_blocks = n_pages * per_page
    b = pl.program_id(0)

    slot = b % 2

    def blk_copy(seq, k):
        bid = jnp.minimum(idx_ref[seq * N_KV * n_top + k], n_past_blocks - 1)
        return pltpu.make_async_copy(
            pool_ref.at[layer, pt_ref[seq * n_pages + bid // per_page], pl.ds(2 * KV_WIDTH, 2 * KV_WIDTH), :],
            blkbuf.at[seq % 2, k], sem.at[seq % 2])

    def start_all(seq):
        for k in range(N_KV * n_top):
            blk_copy(seq, k).start()

    @pl.when(b == 0)
    def _():
        start_all(b)

    @pl.when(b + 1 < pl.num_programs(0))
    def _():
        start_all(b + 1)

    qe, slope_col, row_group, lane_head = _decode_queries(q_ref)
    qf = qe.astype(F32)
    kvn = kvn_ref[...]
    new = lambda c: kvn[:, c * KV_WIDTH:(c + 1) * KV_WIDTH]
    t_f = float(t_pos)

    def attend(k_t, v_t, mask, pos_f, k_new, v_new):
        s = _dot(qe, k_t.astype(BF16)) + slope_col * pos_f
        s = jnp.where(mask, s, NEG)
        s_new = (jnp.sum(qf * k_new.astype(BF16).astype(F32), axis=-1, keepdims=True) + slope_col * t_f)
        m = jnp.maximum(jnp.max(s, axis=-1, keepdims=True), s_new)
        p = jnp.where(mask, jnp.exp(s - m), 0.0)
        p_new = jnp.exp(s_new - m)
        l = jnp.sum(p, axis=-1, keepdims=True) + p_new
        return (_dot_nt(p.astype(BF16), v_t.astype(BF16)) + p_new * v_new) / l

    wbuf = win_ref.shape[1]
    pos_w = (t_pos - wbuf) + lax.broadcasted_iota(jnp.int32, (1, wbuf), 1)
    mask_w = (pos_w >= 0) & (t_pos - pos_w <= WINDOW)
    o_w = attend(win_ref[:KV_WIDTH, :], win_ref[KV_WIDTH:, :], mask_w, pos_w.astype(F32), new(4), new(5))

    for k in range(N_KV * n_top):
        blk_copy(b, k).wait()

    page = blkbuf.shape[3]
    o_s = jnp.zeros((16, KV_WIDTH), F32)
    key = lax.broadcasted_iota(jnp.int32, (1, n_top * page), 1)
    for g in range(N_KV):
        pos = jnp.full((1, n_top * page), t_pos + 1, jnp.int32)
        for i in range(n_top):
            bid = idx_ref[(b * N_KV + g) * n_top + i]
            base = jnp.where(bid < n_past_blocks, (bid // per_page) * page, t_pos + 1)
            in_blk = (key // page == i) & ((key % page) // SLC_BLOCK == bid % per_page)
            pos = jnp.where(in_blk, base + key % page, pos)
        k_t = jnp.concatenate([blkbuf[slot, g * n_top + i, :KV_WIDTH, :] for i in range(n_top)], axis=1)
        v_t = jnp.concatenate([blkbuf[slot, g * n_top + i, KV_WIDTH:, :] for i in range(n_top)], axis=1)
        o_g = attend(k_t, v_t, pos <= t_pos, pos.astype(F32), new(2), new(3))
        o_s = jnp.where(row_group == g, o_g, o_s)

    gates = jax.nn.sigmoid(gp_ref[...] + bg_ref[...])
    row = lax.broadcasted_iota(jnp.int32, (16, 1), 0)
    tot = jnp.zeros((16, KV_WIDTH), F32)
    for c, o in enumerate((oc_ref[...], o_s, o_w)):
        gcol = jnp.zeros((16, 1), F32)
        for r in range(GROUP):
            for g in range(N_KV):
                lane = c * N_Q + GROUP * g + r
                gcol = jnp.where(row == N_KV * r + g, gates[:, lane:lane + 1], gcol)
        tot = tot + gcol * o
    out = []
    for r in range(GROUP):
        h_r = jnp.zeros((1, KV_WIDTH), F32)
        for g in range(N_KV):
            h_r = jnp.where(lane_head == g, tot[N_KV * r + g:N_KV * r + g + 1], h_r)
        out.append(h_r)
    o_ref[...] = jnp.concatenate(out, axis=1)


def nsa_decode(pool, page_table, layer, idx, q, kv_new, win, o_c, g_pre, b_gate, t_pos, n_top):
    b, n_pages = page_table.shape
    page = pool.shape[3]
    per_page = page // SLC_BLOCK
    wbuf = win.shape[2]
    row = lambda w: pl.BlockSpec((None, 1, w), lambda s, pt, ix: (s, 0, 0))
    grid_spec = pltpu.PrefetchScalarGridSpec(
        num_scalar_prefetch=2, grid=(b,),
        in_specs=[row(TOK_WIDTH), row(6 * KV_WIDTH),
                  pl.BlockSpec((None, 2 * KV_WIDTH, wbuf), lambda s, pt, ix: (s, 0, 0)),
                  pl.BlockSpec((None, 16, KV_WIDTH), lambda s, pt, ix: (s, 0, 0)),
                  row(3 * N_Q),
                  pl.BlockSpec((1, 3 * N_Q), lambda s, pt, ix: (0, 0)),
                  pl.BlockSpec(memory_space=pl.ANY)],
        out_specs=row(TOK_WIDTH),
        scratch_shapes=[pltpu.VMEM((2, N_KV * n_top, 2 * KV_WIDTH, page), F32),
                        pltpu.SemaphoreType.DMA((2,))])
    idx_flat = idx[:, :N_KV, :n_top].reshape(-1)
    return pl.pallas_call(
        functools.partial(_nsa_decode_body, layer=layer, t_pos=t_pos, n_top=n_top, n_pages=n_pages,
                          per_page=per_page),
        grid_spec=grid_spec, out_shape=jax.ShapeDtypeStruct((b, 1, TOK_WIDTH), F32),
        compiler_params=_cparams(("arbitrary",)), name="nsa_decode",
    )(page_table.reshape(-1), idx_flat, q, kv_new, win, o_c, g_pre, b_gate.reshape(1, -1), pool)


def _rgd_perm():
    idx = []
    for r in range(GROUP):
        for g in range(N_KV):
            n = GROUP * g + r
            idx.extend(range(n * HEAD_DIM, (n + 1) * HEAD_DIM))
    return jnp.array(idx, jnp.int32)


def _cols_to_rgd(w):
    lead = w.shape[:-1]
    w = w.reshape(lead + (N_KV, GROUP, HEAD_DIM))
    return jnp.swapaxes(w, -3, -2).reshape(lead + (TOK_WIDTH,))


def _block_diag4(w):
    eye = jnp.eye(N_KV, dtype=w.dtype)
    out = jnp.einsum('ab,...ij->...aibj', eye, w)
    return out.reshape(w.shape[:-2] + (KV_WIDTH, KV_WIDTH))


MOE_TILE = 1024
MOE_TILES_PER_STEP = 2


def kernel(x_prompt, x_sample, cache_mem_k, cache_mem_v, state_mlstm_C, state_mlstm_n, state_mlstm_m,
           state_mlstm_conv, cache_nsa, cache_nsa_win, page_table, mem_prompt, norm_mix, norm_mem, norm_ffn,
           norm_final, w_mem_kv, w_in_a, b_gate_a, conv_a, head_norm_a, w_in_b, b_gate_b, w_cmp1, w_cmp2,
           pe_cmp, w_out, w_ffn_gu, w_ffn_d, w_router, w_moe_gu, w_moe_d):
    bp, seq, d = x_prompt.shape
    bs, dseq, _ = x_sample.shape
    assert dseq == 1 and norm_mix.shape[0] == 2 and w_in_a.shape[0] == 1 and w_in_b.shape[0] == 1
    n_mem = mem_prompt.shape[1]
    bf = lambda a: a.astype(BF16)
    xp = x_prompt.reshape(bp * seq, d).astype(F32)
    xs = x_sample.reshape(bs, d).astype(F32)
    mem = mem_prompt.reshape(bp * n_mem, d).astype(F32)

    def mem_kv(i):
        w = w_mem_kv[i]
        mk, mv = norm_proj(mem, norm_mem[i], [bf(w[:, :X_WIDTH]), bf(w[:, X_WIDTH:])], [False, False])
        return mk.reshape(bp, n_mem, X_WIDTH), mv.reshape(bp, n_mem, X_WIDTH)

    def mix_out(i, x, h, a, wh):
        return out_proj(x, h.reshape(x.shape[0], -1), a.reshape(x.shape[0], -1), wh, bf(w_out[i][TOK_WIDTH:]))

    wa = w_in_a[0]
    t3, t4 = 3 * TOK_WIDTH, 4 * TOK_WIDTH
    w0 = [bf(_pad_heads(wa[:, :TOK_WIDTH])), bf(_pad_heads(wa[:, TOK_WIDTH:2 * TOK_WIDTH])),
          bf(_pad_heads(wa[:, 2 * TOK_WIDTH:t3])), bf(_pad_heads(wa[:, t3:t4])),
          bf(wa[:, t4 + 2 * M_HEADS:]), wa[:, t4:t4 + 2 * M_HEADS]]
    hi0 = [False] * 5 + [True]
    cwq = _pad_heads(conv_a[0][:, :TOK_WIDTH])
    cwk = _pad_heads(conv_a[0][:, TOK_WIDTH:])
    head_g = _pad_heads(head_norm_a[0]).reshape(1, M_PAD_WIDTH)
    wh0 = w_out[0][:TOK_WIDTH].reshape(M_HEADS, M_HEAD_DIM, d)
    wh0 = bf(jnp.pad(wh0, ((0, 0), (0, M_HEAD_PAD - M_HEAD_DIM), (0, 0))).reshape(M_PAD_WIDTH, d))
    mk0, mv0 = mem_kv(0)

    qp, kp, vp, op, xqp, gp = norm_proj(xp, norm_mix[0], w0, hi0)
    r3 = lambda a: a.reshape(bp, seq, a.shape[-1])
    hp, c_p, n_p, m_p = mlstm_prompt(r3(qp), r3(kp), r3(vp), r3(op), r3(gp), cwq, cwk, b_gate_a[0], head_g)
    conv_p = jnp.concatenate([_unpad_heads(r3(qp)[:, seq - (CONV_W - 1):]),
                              _unpad_heads(r3(kp)[:, seq - (CONV_W - 1):])], axis=-1)
    ap = mem_attention(r3(xqp), mk0, mv0)
    xp = mix_out(0, xp, hp, ap, wh0)
    xp = ffn(xp, norm_ffn[0], bf(w_ffn_gu[0]), bf(w_ffn_d[0]))

    qs, ks_, vs, os_, xqs, gs = norm_proj(xs, norm_mix[0], w0, hi0)
    s3 = lambda a: a.reshape(bs, 1, a.shape[-1])
    hist = state_mlstm_conv[0].astype(F32)
    hs, c_s, n_s, m_s = mlstm_step(
        s3(qs), s3(ks_), s3(vs), s3(os_), s3(gs), _pad_heads(hist[..., :TOK_WIDTH]),
        _pad_heads(hist[..., TOK_WIDTH:]), state_mlstm_C[0].astype(F32), state_mlstm_n[0].astype(F32),
        state_mlstm_m[0].astype(F32), cwq, cwk, b_gate_a[0], head_g)
    qk_new = jnp.concatenate([_unpad_heads(s3(qs)), _unpad_heads(s3(ks_))], axis=-1)
    conv_s = jnp.concatenate([hist, qk_new], axis=1)[:, 1:]
    as_ = mem_attention(s3(xqs), cache_mem_k[0].reshape(bs, n_mem, X_WIDTH).astype(F32),
                        cache_mem_v[0].reshape(bs, n_mem, X_WIDTH).astype(F32))
    xs = mix_out(0, xs, hs, as_, wh0)
    xs = ffn(xs, norm_ffn[0], bf(w_ffn_gu[0]), bf(w_ffn_d[0]))

    wb = w_in_b[0]
    kv_end = TOK_WIDTH + 6 * KV_WIDTH
    w1 = [bf(_cols_to_rgd(wb[:, :TOK_WIDTH])), bf(wb[:, TOK_WIDTH:kv_end]), bf(wb[:, kv_end + 3 * N_Q:]),
          wb[:, kv_end:kv_end + 3 * N_Q]]
    hi1 = [False] * 3 + [True]
    wh1 = w_out[1][:TOK_WIDTH].reshape(N_KV, GROUP, HEAD_DIM, d)
    wh1 = bf(jnp.swapaxes(wh1, 0, 1).reshape(TOK_WIDTH, d))
    w1bd = bf(_block_diag4(w_cmp1[0].reshape(2, CMP_LEN, HEAD_DIM, HEAD_DIM)))
    w2bd = bf(_block_diag4(w_cmp2[0]))
    pe4 = jnp.tile(pe_cmp[0], (1, 1, N_KV))
    mk1, mv1 = mem_kv(1)

    v_cols = lambda c: wb[:, TOK_WIDTH + c * KV_WIDTH:TOK_WIDTH + (c + 1) * KV_WIDTH]
    wvt = bf(jnp.concatenate([v_cols(3), v_cols(5)], axis=1).T)
    q1, rows1, win1, kb1, vt1, xq1, gt1 = nsa_proj(xp, norm_mix[1], w1[0], w1[1], wvt, w1[2], w1[3].T)
    cmp_p = compress_prompt(r3(rows1), w1bd, w2bd, pe4)
    hp1t = nsa_prompt(q1, kb1, vt1, cmp_p, gt1, b_gate_b[0], bp, seq)
    rows_p = rows1.reshape(bp, seq, 4, N_KV, HEAD_DIM)
    wlen = min(WINDOW, seq)
    win_p = r3(win1)[:, seq - wlen:].reshape(bp, wlen, 2, N_KV, HEAD_DIM)
    ap1 = mem_attention(r3(xq1), mk1, mv1)
    xp = out_proj_t(xp, hp1t, ap1.reshape(bp * seq, X_WIDTH), wh1, bf(w_out[1][TOK_WIDTH:]))

    n_pool, page = cache_nsa.shape[1], cache_nsa.shape[2]
    past_len = page_table.shape[1] * page
    wbuf = cache_nsa_win.shape[2]
    n_slc = -(-(past_len + 1) // SLC_BLOCK)
    n_top = min(N_SEL, n_slc)
    pool = jnp.transpose(cache_nsa.astype(F32), (0, 1, 3, 4, 5, 2)).reshape(
        cache_nsa.shape[0], n_pool, 4 * KV_WIDTH, page)
    q1s, kv1s, xq1s, g1s = norm_proj(xs, norm_mix[1], w1, hi1)
    cmp_s = compress_paged(pool, page_table, 0, w1bd, w2bd, pe4)
    oc_s, idx_s = nsa_select(s3(q1s), cmp_s, past_len, n_slc, n_top)
    win_old = cache_nsa_win[0].astype(F32)
    win_t = jnp.transpose(win_old, (0, 2, 3, 4, 1)).reshape(bs, 2 * KV_WIDTH, wbuf)
    hs1 = nsa_decode(pool, page_table, 0, idx_s, s3(q1s), s3(kv1s), win_t,
                     oc_s, s3(g1s), b_gate_b[0], past_len, n_top)
    rows_s = kv1s[:, :4 * KV_WIDTH].reshape(bs, 1, 4, N_KV, HEAD_DIM)
    win_new = kv1s[:, 4 * KV_WIDTH:].reshape(bs, 1, 2, N_KV, HEAD_DIM)
    win_s = jnp.concatenate([win_old, win_new], axis=1)[:, -wbuf:]
    as1 = mem_attention(s3(xq1s), cache_mem_k[1].reshape(bs, n_mem, X_WIDTH).astype(F32),
                        cache_mem_v[1].reshape(bs, n_mem, X_WIDTH).astype(F32))
    xs = mix_out(1, xs, hs1, as1, wh1)

    w_r = jnp.pad(w_router[0], ((0, 0), (0, 128 - N_EXPERTS)))
    w_gu, w_dn = bf(w_moe_gu[0]), bf(w_moe_d[0])

    def experts(x, tm):
        u, comb, rank, counts = router(x, norm_ffn[1], w_r, tm)
        return final_norm(x, moe(u, comb, rank, counts, w_gu, w_dn, tm), norm_final)

    y_prompt = experts(xp, min(MOE_TILE, bp * seq)).reshape(bp, seq, d)
    y_sample = experts(xs, bs).reshape(bs, 1, d)

    unflat = lambda a: a.reshape(bp, n_mem, X_HEADS, HEAD_DIM)
    return (y_prompt, y_sample,
            jnp.stack([unflat(mk0), unflat(mk1)]), jnp.stack([unflat(mv0), unflat(mv1)]),
            c_p[None], n_p[None], m_p[None, :, :, 0], conv_p[None], rows_p[None], win_p[None],
            c_s[None], n_s[None], m_s[None, :, :, 0], conv_s[None], rows_s[None], win_s[None])
```

```python
import functools

import jax
import jax.numpy as jnp
from jax import lax
from jax.experimental import pallas as pl
from jax.experimental.pallas import tpu as pltpu

F32 = jnp.float32
BF16 = jnp.bfloat16
HIGHEST = lax.Precision.HIGHEST

D_MODEL = 1024
HEAD_DIM = 64
X_HEADS = 4
X_WIDTH = X_HEADS * HEAD_DIM
TOK_WIDTH = D_MODEL - X_WIDTH
M_HEADS = 8
M_HEAD_DIM = TOK_WIDTH // M_HEADS
M_HEAD_PAD = 128
M_PAD_WIDTH = M_HEADS * M_HEAD_PAD
CONV_W = 4
N_Q = TOK_WIDTH // HEAD_DIM
N_KV = 4
GROUP = N_Q // N_KV
KV_WIDTH = N_KV * HEAD_DIM
CMP_STRIDE = 16
CMP_LEN = 2 * CMP_STRIDE
SLC_BLOCK = 64
N_SEL = 16
WINDOW = 512
Q_BLOCK = 128
D_FF = 2816
N_EXPERTS = 8
TOP_K = 2
E_FF = 7 * D_MODEL // 4
RMS_EPS = 1e-6
NEG = -1e30
BIG = 1e9

LANES = 128
MLSTM_CHUNK = 128
VMEM_LIMIT = 56 * 1024 * 1024


def _cparams(sem):
    return pltpu.CompilerParams(dimension_semantics=sem, vmem_limit_bytes=VMEM_LIMIT)


def _rms(x, g):
    return x * lax.rsqrt(jnp.mean(x * x, axis=-1, keepdims=True) + RMS_EPS) * g


def _dot(a, b):
    return jnp.dot(a, b, preferred_element_type=F32)


def _dot_nt(a, b):
    return lax.dot_general(a, b, (((1,), (1,)), ((), ())), preferred_element_type=F32)


def _dot_tn(a, b):
    return lax.dot_general(a, b, (((0,), (0,)), ((), ())), preferred_element_type=F32)


def _norm_proj_body(x_ref, g_ref, *refs, n_out, hi):
    w_refs, o_refs = refs[:n_out], refs[n_out:]
    u = _rms(x_ref[...], g_ref[...])
    ub = u.astype(BF16)
    for w_ref, o_ref, h in zip(w_refs, o_refs, hi):
        if h:
            o_ref[...] = jnp.dot(u, w_ref[...], precision=HIGHEST, preferred_element_type=F32)
        else:
            o_ref[...] = _dot(ub, w_ref[...])


def norm_proj(x, gamma, weights, hi):
    m, d = x.shape
    tm = min(512, m)
    n_out = len(weights)
    in_specs = [pl.BlockSpec((tm, d), lambda i: (i, 0)), pl.BlockSpec((1, d), lambda i: (0, 0))]
    in_specs += [pl.BlockSpec(w.shape, lambda i: (0, 0)) for w in weights]
    out_specs = [pl.BlockSpec((tm, w.shape[1]), lambda i: (i, 0)) for w in weights]
    out_shape = [jax.ShapeDtypeStruct((m, w.shape[1]), F32) for w in weights]
    return pl.pallas_call(
        functools.partial(_norm_proj_body, n_out=n_out, hi=tuple(hi)),
        grid=(m // tm,), in_specs=in_specs, out_specs=out_specs, out_shape=out_shape,
        compiler_params=_cparams(("parallel",)), name="norm_proj",
    )(x, gamma.reshape(1, d), *weights)


def _mem_attn_body(q_ref, k_ref, v_ref, o_ref):
    q = q_ref[...] * (HEAD_DIM ** -0.5)
    rows = q.shape[0]
    if rows < 8:
        q = jnp.broadcast_to(q[0:1], (8, X_WIDTH))
    kb = k_ref[...].astype(BF16)
    vb = v_ref[...].astype(BF16)
    head = lax.broadcasted_iota(jnp.int32, (1, X_WIDTH), 1) // HEAD_DIM
    out = jnp.zeros(q.shape, F32)
    for h in range(X_HEADS):
        qh = jnp.where(head == h, q, 0.0).astype(BF16)
        s = _dot_nt(qh, kb)
        p = jnp.exp(s - jnp.max(s, axis=-1, keepdims=True))
        o = _dot(p.astype(BF16), vb) / jnp.sum(p, axis=-1, keepdims=True)
        out = jnp.where(head == h, o, out)
    o_ref[...] = out[:rows]


def mem_attention(q, mk, mv):
    b, t, w = q.shape
    n = mk.shape[1]
    tq = min(512, t)
    return pl.pallas_call(
        _mem_attn_body, grid=(b, t // tq),
        in_specs=[pl.BlockSpec((None, tq, w), lambda i, j: (i, j, 0)),
                  pl.BlockSpec((None, n, w), lambda i, j: (i, 0, 0)),
                  pl.BlockSpec((None, n, w), lambda i, j: (i, 0, 0))],
        out_specs=pl.BlockSpec((None, tq, w), lambda i, j: (i, j, 0)),
        out_shape=jax.ShapeDtypeStruct((b, t, w), F32),
        compiler_params=_cparams(("parallel", "parallel")), name="mem_attention",
    )(q, mk, mv)


def _out_proj_body(x_ref, h_ref, a_ref, wh_ref, wa_ref, o_ref):
    o_ref[...] = (x_ref[...] + _dot(h_ref[...].astype(BF16), wh_ref[...])
                  + _dot(a_ref[...].astype(BF16), wa_ref[...]))


def out_proj(x, h, a, wh, wa):
    m, d = x.shape
    tm = min(512, m)
    kh, ka = h.shape[1], a.shape[1]
    return pl.pallas_call(
        _out_proj_body, grid=(m // tm,),
        in_specs=[pl.BlockSpec((tm, d), lambda i: (i, 0)),
                  pl.BlockSpec((tm, kh), lambda i: (i, 0)),
                  pl.BlockSpec((tm, ka), lambda i: (i, 0)),
                  pl.BlockSpec((kh, d), lambda i: (0, 0)),
                  pl.BlockSpec((ka, d), lambda i: (0, 0))],
        out_specs=pl.BlockSpec((tm, d), lambda i: (i, 0)),
        out_shape=jax.ShapeDtypeStruct((m, d), F32),
        compiler_params=_cparams(("parallel",)), name="out_proj",
    )(x, h, a, wh, wa)


def _out_proj_t_body(x_ref, ht_ref, a_ref, wh_ref, wa_ref, o_ref):
    o_ref[...] = (x_ref[...] + _dot_tn(ht_ref[...], wh_ref[...])
                  + _dot(a_ref[...].astype(BF16), wa_ref[...]))


def out_proj_t(x, ht, a, wh, wa):
    m, d = x.shape
    tm = min(512, m)
    kh, ka = ht.shape[0], a.shape[1]
    return pl.pallas_call(
        _out_proj_t_body, grid=(m // tm,),
        in_specs=[pl.BlockSpec((tm, d), lambda i: (i, 0)),
                  pl.BlockSpec((kh, tm), lambda i: (0, i)),
                  pl.BlockSpec((tm, ka), lambda i: (i, 0)),
                  pl.BlockSpec((kh, d), lambda i: (0, 0)),
                  pl.BlockSpec((ka, d), lambda i: (0, 0))],
        out_specs=pl.BlockSpec((tm, d), lambda i: (i, 0)),
        out_shape=jax.ShapeDtypeStruct((m, d), F32),
        compiler_params=_cparams(("parallel",)), name="out_proj_t",
    )(x, ht, a, wh, wa)


def _nsa_proj_body(x_ref, g_ref, wq_ref, wkv_ref, wvt_ref, wxq_ref, wgt_ref,
                   q_ref, kv_ref, kb_ref, vt_ref, xq_ref, gt_ref):
    u = _rms(x_ref[...], g_ref[...])
    ub = u.astype(BF16)
    q_ref[...] = _dot(ub, wq_ref[...]).astype(BF16)
    kv = _dot(ub, wkv_ref[...])
    kv_ref[...] = kv
    kb_ref[...] = jnp.concatenate([kv[:, 2 * KV_WIDTH:3 * KV_WIDTH], kv[:, 4 * KV_WIDTH:5 * KV_WIDTH]],
                                  axis=1).astype(BF16)
    vt_ref[...] = _dot_nt(wvt_ref[...], ub).astype(BF16)
    xq_ref[...] = _dot(ub, wxq_ref[...])
    gt_ref[...] = lax.dot_general(wgt_ref[...], u, (((1,), (1,)), ((), ())), precision=HIGHEST,
                                  preferred_element_type=F32)


def nsa_proj(x, gamma, wq, wkv, wvt, wxq, wgt):
    m, d = x.shape
    tm = min(512, m)
    const = lambda a: pl.BlockSpec(a.shape, lambda i: (0, 0))
    rows = lambda n: pl.BlockSpec((tm, n), lambda i: (i, 0))
    cols = lambda n: pl.BlockSpec((n, tm), lambda i: (0, i))
    ng = wgt.shape[0]
    return pl.pallas_call(
        _nsa_proj_body, grid=(m // tm,),
        in_specs=[rows(d), pl.BlockSpec((1, d), lambda i: (0, 0)),
                  const(wq), const(wkv), const(wvt), const(wxq), const(wgt)],
        out_specs=[rows(TOK_WIDTH), rows(6 * KV_WIDTH), rows(2 * KV_WIDTH), cols(2 * KV_WIDTH),
                   rows(X_WIDTH), cols(ng)],
        out_shape=[jax.ShapeDtypeStruct((m, TOK_WIDTH), BF16),
                   jax.ShapeDtypeStruct((m, 6 * KV_WIDTH), F32),
                   jax.ShapeDtypeStruct((m, 2 * KV_WIDTH), BF16),
                   jax.ShapeDtypeStruct((2 * KV_WIDTH, m), BF16),
                   jax.ShapeDtypeStruct((m, X_WIDTH), F32),
                   jax.ShapeDtypeStruct((ng, m), F32)],
        compiler_params=_cparams(("parallel",)), name="nsa_proj",
    )(x, gamma.reshape(1, d), wq, wkv, wvt, wxq, wgt)


def _ffn_body(x_ref, g_ref, wg_ref, wu_ref, wd_ref, o_ref, u_sc):
    @pl.when(pl.program_id(1) == 0)
    def _():
        x = x_ref[...]
        u_sc[...] = _rms(x, g_ref[...]).astype(BF16)
        o_ref[...] = x

    u = u_sc[...]
    hg = _dot(u, wg_ref[...])
    hu = _dot(u, wu_ref[...])
    hm = (hg * jax.nn.sigmoid(hg) * hu).astype(BF16)
    o_ref[...] += _dot(hm, wd_ref[...])


def ffn(x, gamma, w_gu, w_d):
    m, d = x.shape
    f = w_d.shape[0]
    tm = min(1024, m)
    tf = f // 2 if (f // 2) % LANES == 0 else LANES * 2
    nf = f // tf
    return pl.pallas_call(
        _ffn_body, grid=(m // tm, nf),
        in_specs=[pl.BlockSpec((tm, d), lambda i, j: (i, 0)),
                  pl.BlockSpec((1, d), lambda i, j: (0, 0)),
                  pl.BlockSpec((d, tf), lambda i, j: (0, j)),
                  pl.BlockSpec((d, tf), lambda i, j: (0, nf + j)),
                  pl.BlockSpec((tf, d), lambda i, j: (j, 0))],
        out_specs=pl.BlockSpec((tm, d), lambda i, j: (i, 0)),
        out_shape=jax.ShapeDtypeStruct((m, d), F32),
        scratch_shapes=[pltpu.VMEM((tm, d), BF16)],
        compiler_params=_cparams(("parallel", "arbitrary")), name="ffn",
    )(x, gamma.reshape(1, d), w_gu, w_gu, w_d)


def _router_body(x_ref, g_ref, wr_ref, u_ref, comb_ref, rank_ref, cnt_ref):
    tm = x_ref.shape[0]
    u = _rms(x_ref[...], g_ref[...])
    u_ref[...] = u.astype(BF16)
    logits = jnp.dot(u, wr_ref[...], precision=HIGHEST, preferred_element_type=F32)
    lane = lax.broadcasted_iota(jnp.int32, (tm, 128), 1)
    logits = jnp.where(lane < N_EXPERTS, logits, -jnp.inf)
    m1 = jnp.max(logits, axis=-1, keepdims=True)
    i1 = jnp.min(jnp.where(logits == m1, lane, 128), axis=-1, keepdims=True)
    rest = jnp.where(lane == i1, -jnp.inf, logits)
    m2 = jnp.max(rest, axis=-1, keepdims=True)
    i2 = jnp.min(jnp.where(rest == m2, lane, 128), axis=-1, keepdims=True)
    e2 = jnp.exp(m2 - m1)
    g1 = 1.0 / (1.0 + e2)
    g2 = e2 / (1.0 + e2)
    comb_ref[...] = jnp.where(lane == i1, g1, 0.0) + jnp.where(lane == i2, g2, 0.0)
    sel = ((lane == i1) | (lane == i2)).astype(F32)
    r = lax.broadcasted_iota(jnp.int32, (tm, tm), 0)
    c = lax.broadcasted_iota(jnp.int32, (tm, tm), 1)
    tril = (c < r).astype(BF16)
    rank = _dot(tril, sel.astype(BF16))
    rank_ref[...] = jnp.where(sel > 0, rank, -1.0)
    cnt_ref[...] = jnp.sum(sel, axis=0, keepdims=True).astype(jnp.int32)


def router(x, gamma, w_router_pad, tm):
    m, d = x.shape
    nt = m // tm
    return pl.pallas_call(
        _router_body, grid=(nt,),
        in_specs=[pl.BlockSpec((tm, d), lambda i: (i, 0)),
                  pl.BlockSpec((1, d), lambda i: (0, 0)),
                  pl.BlockSpec((d, 128), lambda i: (0, 0))],
        out_specs=[pl.BlockSpec((tm, d), lambda i: (i, 0)),
                   pl.BlockSpec((tm, 128), lambda i: (i, 0)),
                   pl.BlockSpec((tm, 128), lambda i: (i, 0)),
                   pl.BlockSpec((None, 1, 128), lambda i: (i, 0, 0))],
        out_shape=[jax.ShapeDtypeStruct((m, d), BF16),
                   jax.ShapeDtypeStruct((m, 128), F32),
                   jax.ShapeDtypeStruct((m, 128), F32),
                   jax.ShapeDtypeStruct((nt, 1, 128), jnp.int32)],
        compiler_params=_cparams(("parallel",)), name="router",
    )(x, gamma.reshape(1, d), w_router_pad)


def _moe_body(cnt_ref, u_ref, rank_ref, rrow_ref, comb_ref, wg_ref, wu_ref, wd_ref, o_ref, rcol_sc, gcol_sc,
              *, cap, tm):
    i = pl.program_id(0)
    e = pl.program_id(1)
    n_sub = u_ref.shape[0] // tm

    @pl.when(e == 0)
    def _():
        o_ref[...] = jnp.zeros(o_ref.shape, F32)

    mine = lax.broadcasted_iota(jnp.int32, (n_sub * tm, 128), 1) == e
    rcol_sc[...] = jnp.sum(jnp.where(mine, rank_ref[...], 0.0), axis=1, keepdims=True)
    gcol_sc[...] = jnp.sum(jnp.where(mine, comb_ref[...], 0.0), axis=1, keepdims=True)
    for s in range(n_sub):
        rows = slice(s * tm, (s + 1) * tm)
        cnt = cnt_ref[(i * n_sub + s) * N_EXPERTS + e]
        n_chunks = (cnt + cap - 1) // cap

        def chunk(c, carry, rows=rows):
            base = (c * cap).astype(F32)
            slot_r = lax.broadcasted_iota(jnp.int32, (cap, tm), 0).astype(F32) + base
            gather = (rrow_ref[:, rows] == slot_r).astype(BF16)
            xg = _dot(gather, u_ref[rows, :]).astype(BF16)
            hg = _dot(xg, wg_ref[...])
            hu = _dot(xg, wu_ref[...])
            hm = (hg * jax.nn.sigmoid(hg) * hu).astype(BF16)
            y = _dot(hm, wd_ref[...]).astype(BF16)
            slot_c = lax.broadcasted_iota(jnp.int32, (tm, cap), 1).astype(F32) + base
            scatter = (rcol_sc[rows, :] == slot_c).astype(BF16)
            o_ref[rows, :] += gcol_sc[rows, :] * _dot(scatter, y)
            return carry

        lax.fori_loop(0, n_chunks, chunk, 0)


def moe(u, comb, rank, counts, w_gu, w_d, tm):
    m, d = u.shape
    nt = m // tm
    cap = min(tm, max(16, tm * TOP_K // N_EXPERTS))
    ef = w_d.shape[1]
    rrow = rank[:, :N_EXPERTS].T.reshape(N_EXPERTS, 1, m)
    cnt = counts[:, 0, :N_EXPERTS].reshape(nt * N_EXPERTS)
    n_sub = MOE_TILES_PER_STEP if nt % MOE_TILES_PER_STEP == 0 else 1
    ts = n_sub * tm
    grid_spec = pltpu.PrefetchScalarGridSpec(
        num_scalar_prefetch=1, grid=(nt // n_sub, N_EXPERTS),
        in_specs=[pl.BlockSpec((ts, d), lambda i, e, c: (i, 0)),
                  pl.BlockSpec((ts, 128), lambda i, e, c: (i, 0)),
                  pl.BlockSpec((None, 1, ts), lambda i, e, c: (e, 0, i)),
                  pl.BlockSpec((ts, 128), lambda i, e, c: (i, 0)),
                  pl.BlockSpec((None, d, ef), lambda i, e, c: (e, 0, 0)),
                  pl.BlockSpec((None, d, ef), lambda i, e, c: (e, 0, 1)),
                  pl.BlockSpec((None, ef, d), lambda i, e, c: (e, 0, 0))],
        out_specs=pl.BlockSpec((ts, d), lambda i, e, c: (i, 0)),
        scratch_shapes=[pltpu.VMEM((ts, 1), F32), pltpu.VMEM((ts, 1), F32)])
    return pl.pallas_call(
        functools.partial(_moe_body, cap=cap, tm=tm), grid_spec=grid_spec,
        out_shape=jax.ShapeDtypeStruct((m, d), F32),
        compiler_params=_cparams(("parallel", "arbitrary")), name="moe",
    )(cnt, u, rank, rrow, comb, w_gu, w_gu, w_d)


def _final_body(x_ref, y_ref, g_ref, o_ref):
    o_ref[...] = _rms(x_ref[...] + y_ref[...], g_ref[...])


def final_norm(x, y, gamma):
    m, d = x.shape
    tm = min(1024, m)
    return pl.pallas_call(
        _final_body, grid=(m // tm,),
        in_specs=[pl.BlockSpec((tm, d), lambda i: (i, 0)),
                  pl.BlockSpec((tm, d), lambda i: (i, 0)),
                  pl.BlockSpec((1, d), lambda i: (0, 0))],
        out_specs=pl.BlockSpec((tm, d), lambda i: (i, 0)),
        out_shape=jax.ShapeDtypeStruct((m, d), F32),
        compiler_params=_cparams(("parallel",)), name="final_norm",
    )(x, y, gamma.reshape(1, d))


def _log_sigmoid(x):
    return jnp.minimum(x, 0.0) - jnp.log(1.0 + jnp.exp(-jnp.abs(x)))


def _silu(x):
    return x * jax.nn.sigmoid(x)


def _mlstm_body(qp_ref, kp_ref, v_ref, o_ref, g_ref, gt_ref, cwq_ref, cwk_ref, bgr_ref, bgc_ref, hg_ref,
                h_ref, c_out, n_out, m_out, c_sc, n_sc, m_sc, tq_sc, tk_sc):
    j = pl.program_id(1)
    L = qp_ref.shape[0]
    P = M_HEAD_PAD

    @pl.when(j == 0)
    def _():
        c_sc[...] = jnp.zeros(c_sc.shape, F32)
        n_sc[...] = jnp.zeros(n_sc.shape, F32)
        m_sc[...] = jnp.zeros(m_sc.shape, F32)
        tq_sc[...] = jnp.zeros(tq_sc.shape, F32)
        tk_sc[...] = jnp.zeros(tk_sc.shape, F32)

    def conv(x_ref, tail_sc, cw_ref):
        x = x_ref[...]
        ext = jnp.concatenate([tail_sc[...], x], axis=0)
        y = cw_ref[3:4, :] * x
        for tap in range(CONV_W - 1):
            lo = 8 - (CONV_W - 1) + tap
            y = y + cw_ref[tap:tap + 1, :] * ext[lo:lo + L]
        tail_sc[...] = x[L - 8:L]
        return _silu(y)

    qc = conv(qp_ref, tq_sc, cwq_ref)
    kc = conv(kp_ref, tk_sc, cwk_ref) * (M_HEAD_DIM ** -0.5)

    g = g_ref[...] + bgr_ref[...]
    gt = gt_ref[...] + bgc_ref[...]
    ig_c, lf_c = g[:, :M_HEADS], _log_sigmoid(g[:, M_HEADS:])
    ig_r, lf_r = gt[:M_HEADS], _log_sigmoid(gt[M_HEADS:])
    row = lax.broadcasted_iota(jnp.int32, (L, L), 0)
    col = lax.broadcasted_iota(jnp.int32, (L, L), 1)
    causal = col <= row
    f_c = jnp.dot(causal.astype(F32), lf_c, precision=HIGHEST, preferred_element_type=F32)
    f_r = jnp.dot(lf_r, (row <= col).astype(F32), precision=HIGHEST, preferred_element_type=F32)

    for h in range(M_HEADS):
        sl = slice(h * P, (h + 1) * P)
        q, k, v = qc[:, sl], kc[:, sl], v_ref[:, sl]
        fc, fr = f_c[:, h:h + 1], f_r[h:h + 1, :]
        m_prev = m_sc[h:h + 1, 0:1]
        d = jnp.where(causal, fc - fr + ig_r[h:h + 1, :], NEG)
        a = fc + m_prev
        mt = jnp.maximum(a, jnp.max(d, axis=1, keepdims=True))
        wm = jnp.exp(d - mt)
        gcol = jnp.exp(a - mt)
        qb, kb, vb = q.astype(BF16), k.astype(BF16), v.astype(BF16)
        s = _dot_nt(qb, kb) * wm
        c_old = c_sc[h]
        n_old = n_sc[h:h + 1, :]
        num = _dot(s.astype(BF16), vb) + gcol * _dot_nt(qb, c_old.astype(BF16))
        den = jnp.sum(s, axis=1, keepdims=True) + gcol * jnp.sum(q * n_old, axis=1, keepdims=True)
        hc = num / jnp.maximum(jnp.abs(den), jnp.exp(-mt))
        og = jax.nn.sigmoid(o_ref[:, sl]) * hc
        ms = jnp.sum(og * og, axis=1, keepdims=True) * (1.0 / M_HEAD_DIM)
        h_ref[:, sl] = og * lax.rsqrt(ms + RMS_EPS) * hg_ref[:, sl]
        mt_last, gl = mt[L - 1:L], gcol[L - 1:L]
        wl = jnp.exp(fc[L - 1:L] - fc + ig_c[:, h:h + 1] - mt_last)
        c_sc[h] = gl * c_old + _dot_tn((v * wl).astype(BF16), kb)
        n_sc[h:h + 1, :] = gl * n_old + jnp.sum(wl * k, axis=0, keepdims=True)
        m_sc[h:h + 1, :] = jnp.broadcast_to(mt_last, (1, P))

    @pl.when(j == pl.num_programs(1) - 1)
    def _():
        for h in range(M_HEADS):
            c_out[h] = c_sc[h][:M_HEAD_DIM, :M_HEAD_DIM]
        n_out[...] = n_sc[:, :M_HEAD_DIM]
        m_out[...] = m_sc[...]


def mlstm_prompt(qp, kp, v, o, g, cwq, cwk, b_gate, head_g):
    b, t, w = qp.shape
    L = min(MLSTM_CHUNK, t)
    gt = jnp.swapaxes(g, 1, 2)
    seq = pl.BlockSpec((None, L, w), lambda i, j: (i, j, 0))
    const = lambda shape: pl.BlockSpec(shape, lambda i, j: (0,) * len(shape))
    return pl.pallas_call(
        _mlstm_body, grid=(b, t // L),
        in_specs=[seq, seq, seq, seq,
                  pl.BlockSpec((None, L, 2 * M_HEADS), lambda i, j: (i, j, 0)),
                  pl.BlockSpec((None, 2 * M_HEADS, L), lambda i, j: (i, 0, j)),
                  const((CONV_W, w)), const((CONV_W, w)), const((1, 2 * M_HEADS)), const((2 * M_HEADS, 1)),
                  const((1, w))],
        out_specs=[seq,
                   pl.BlockSpec((None, M_HEADS, M_HEAD_DIM, M_HEAD_DIM), lambda i, j: (i, 0, 0, 0)),
                   pl.BlockSpec((None, M_HEADS, M_HEAD_DIM), lambda i, j: (i, 0, 0)),
                   pl.BlockSpec((None, M_HEADS, M_HEAD_PAD), lambda i, j: (i, 0, 0))],
        out_shape=[jax.ShapeDtypeStruct((b, t, w), F32),
                   jax.ShapeDtypeStruct((b, M_HEADS, M_HEAD_DIM, M_HEAD_DIM), F32),
                   jax.ShapeDtypeStruct((b, M_HEADS, M_HEAD_DIM), F32),
                   jax.ShapeDtypeStruct((b, M_HEADS, M_HEAD_PAD), F32)],
        scratch_shapes=[pltpu.VMEM((M_HEADS, M_HEAD_PAD, M_HEAD_PAD), F32),
                        pltpu.VMEM((M_HEADS, M_HEAD_PAD), F32),
                        pltpu.VMEM((M_HEADS, M_HEAD_PAD), F32),
                        pltpu.VMEM((8, w), F32), pltpu.VMEM((8, w), F32)],
        compiler_params=_cparams(("parallel", "arbitrary")), name="mlstm_prompt",
    )(qp, kp, v, o, g, gt, cwq, cwk, b_gate.reshape(1, -1), b_gate.reshape(-1, 1), head_g)


def _mlstm_step_body(qp_ref, kp_ref, v_ref, o_ref, g_ref, hq_ref, hk_ref, c0_ref, n0_ref, m0_ref,
                     cwq_ref, cwk_ref, bgr_ref, hg_ref, h_ref, c_out, n_out, m_out):
    P, dh = M_HEAD_PAD, M_HEAD_DIM

    def conv(x_ref, hist_ref, cw_ref):
        y = cw_ref[3:4, :] * x_ref[...]
        for tap in range(CONV_W - 1):
            y = y + cw_ref[tap:tap + 1, :] * hist_ref[tap:tap + 1, :]
        return _silu(y)

    qc = conv(qp_ref, hq_ref, cwq_ref)
    kc = conv(kp_ref, hk_ref, cwk_ref) * (dh ** -0.5)
    g = g_ref[...] + bgr_ref[...]
    m0 = m0_ref[...]
    for h in range(M_HEADS):
        lo = h * P
        q, k, v = qc[:, lo:lo + dh], kc[:, lo:lo + dh], v_ref[:, lo:lo + dh]
        ig = g[:, h:h + 1]
        lf = _log_sigmoid(g[:, M_HEADS + h:M_HEADS + h + 1])
        a = lf + m0[h:h + 1, :]
        mt = jnp.maximum(a, ig)
        wv = jnp.exp(ig - mt)
        gv = jnp.exp(a - mt)
        qf, kf = q.astype(BF16).astype(F32), k.astype(BF16).astype(F32)
        qk = jnp.sum(qf * kf, axis=1, keepdims=True) * wv
        c_old, n_old = c0_ref[h], n0_ref[h:h + 1, :]
        q8 = jnp.broadcast_to(q, (8, dh)).astype(BF16)
        cq = _dot_nt(q8, c_old.astype(BF16))[0:1]
        num = qk * v + gv * cq
        den = qk + gv * jnp.sum(q * n_old, axis=1, keepdims=True)
        hc = num / jnp.maximum(jnp.abs(den), jnp.exp(-mt))
        og = jax.nn.sigmoid(o_ref[:, lo:lo + dh]) * hc
        ms = jnp.sum(og * og, axis=1, keepdims=True) * (1.0 / dh)
        hn = og * lax.rsqrt(ms + RMS_EPS) * hg_ref[:, lo:lo + dh]
        h_ref[:, lo:lo + P] = jnp.concatenate([hn, jnp.zeros((1, P - dh), F32)], axis=1)
        v_col = jnp.transpose(jnp.broadcast_to(v_ref[:, lo:lo + P], (8, P)))[:dh, 0:1]
        c_out[h] = gv * c_old + wv * (v_col * k)
        n_out[h:h + 1, :] = gv * n_old + wv * k
        m_out[h:h + 1, :] = jnp.broadcast_to(mt, (1, P))


def mlstm_step(qp, kp, v, o, g, hist_q, hist_k, c0, n0, m0, cwq, cwk, b_gate, head_g):
    b, _, w = qp.shape
    row = pl.BlockSpec((None, 1, w), lambda i: (i, 0, 0))
    const = lambda shape: pl.BlockSpec(shape, lambda i: (0,) * len(shape))
    cspec = pl.BlockSpec((None, M_HEADS, M_HEAD_DIM, M_HEAD_DIM), lambda i: (i, 0, 0, 0))
    nspec = pl.BlockSpec((None, M_HEADS, M_HEAD_DIM), lambda i: (i, 0, 0))
    return pl.pallas_call(
        _mlstm_step_body, grid=(b,),
        in_specs=[row, row, row, row,
                  pl.BlockSpec((None, 1, 2 * M_HEADS), lambda i: (i, 0, 0)),
                  pl.BlockSpec((None, CONV_W - 1, w), lambda i: (i, 0, 0)),
                  pl.BlockSpec((None, CONV_W - 1, w), lambda i: (i, 0, 0)),
                  cspec, nspec,
                  pl.BlockSpec((None, M_HEADS, 1), lambda i: (i, 0, 0)),
                  const((CONV_W, w)), const((CONV_W, w)), const((1, 2 * M_HEADS)), const((1, w))],
        out_specs=[row, cspec, nspec,
                   pl.BlockSpec((None, M_HEADS, M_HEAD_PAD), lambda i: (i, 0, 0))],
        out_shape=[jax.ShapeDtypeStruct((b, 1, w), F32),
                   jax.ShapeDtypeStruct(c0.shape, F32),
                   jax.ShapeDtypeStruct(n0.shape, F32),
                   jax.ShapeDtypeStruct((b, M_HEADS, M_HEAD_PAD), F32)],
        compiler_params=_cparams(("parallel",)), name="mlstm_step",
    )(qp, kp, v, o, g, hist_q, hist_k, c0, n0, m0.reshape(b, M_HEADS, 1), cwq, cwk,
      b_gate.reshape(1, -1), head_g)


def _pad_heads(a):
    lead = a.shape[:-1]
    a = a.reshape(lead + (M_HEADS, M_HEAD_DIM))
    a = jnp.pad(a, [(0, 0)] * len(lead) + [(0, 0), (0, M_HEAD_PAD - M_HEAD_DIM)])
    return a.reshape(lead + (M_PAD_WIDTH,))


def _unpad_heads(a):
    lead = a.shape[:-1]
    return a.reshape(lead + (M_HEADS, M_HEAD_PAD))[..., :M_HEAD_DIM].reshape(lead + (TOK_WIDTH,))


def _gelu_tanh(x):
    return x * (0.5 * (1.0 + jnp.tanh(0.7978845608028654 * (x + 0.044715 * (x * x * x)))))


def _compress_tail(u, v_next, pe_ref, w1_ref, w2_ref):
    bias = jnp.zeros((8, KV_WIDTH), F32)
    for p in range(CMP_LEN):
        pe8 = jnp.broadcast_to(pe_ref[p:p + 1, :], (8, KV_WIDTH)).astype(BF16)
        bias = bias + _dot(pe8, w1_ref[p])
    pre = u + v_next + bias[0:1]
    return _dot(_gelu_tanh(pre).astype(BF16), w2_ref[...])


PERM_ROWS = CMP_STRIDE * CMP_STRIDE


def _regroup_perm():
    r = lax.broadcasted_iota(jnp.int32, (PERM_ROWS, PERM_ROWS), 0)
    c = lax.broadcasted_iota(jnp.int32, (PERM_ROWS, PERM_ROWS), 1)
    return (c == (r % CMP_STRIDE) * CMP_STRIDE + r // CMP_STRIDE).astype(BF16)


def _regroup_rows(perm, x, ybuf, grp, transposed=False):
    y = (_dot_nt(perm, x) if transposed else _dot(perm, x)).astype(BF16)
    for p in range(CMP_STRIDE):
        ybuf[p, pl.ds(pl.multiple_of(grp * CMP_STRIDE, CMP_STRIDE), CMP_STRIDE), :] = (
            y[p * CMP_STRIDE:(p + 1) * CMP_STRIDE])


def _compress_from(rows_at, nb, w1_ref, w2_ref, pe_ref, o_ref):
    u = jnp.zeros((nb, KV_WIDTH), F32)
    v = jnp.zeros((nb, KV_WIDTH), F32)
    for p in range(CMP_STRIDE):
        x = rows_at(p)
        u = u + _dot(x, w1_ref[p])
        v = v + _dot(x, w1_ref[CMP_STRIDE + p])
    v_next = jnp.concatenate([v[1:], jnp.zeros((1, KV_WIDTH), F32)], axis=0)
    o_ref[...] = _compress_tail(u, v_next, pe_ref, w1_ref, w2_ref)


def _compress_body(rows_ref, w1_ref, w2_ref, pe_ref, o_ref, ybuf):
    perm = _regroup_perm()
    for grp in range(rows_ref.shape[0] // PERM_ROWS):
        _regroup_rows(perm, rows_ref[grp * PERM_ROWS:(grp + 1) * PERM_ROWS, :].astype(BF16), ybuf, grp)
    _compress_from(lambda p: ybuf[p], ybuf.shape[1], w1_ref, w2_ref, pe_ref, o_ref)


def compress_prompt(kv, w1bd, w2bd, pe4):
    b, t, _ = kv.shape
    nb = t // CMP_STRIDE
    return pl.pallas_call(
        _compress_body, grid=(b, 2),
        in_specs=[pl.BlockSpec((None, t, KV_WIDTH), lambda i, c: (i, 0, c)),
                  pl.BlockSpec((None, CMP_LEN, KV_WIDTH, KV_WIDTH), lambda i, c: (c, 0, 0, 0)),
                  pl.BlockSpec((None, KV_WIDTH, KV_WIDTH), lambda i, c: (c, 0, 0)),
                  pl.BlockSpec((None, CMP_LEN, KV_WIDTH), lambda i, c: (c, 0, 0))],
        out_specs=pl.BlockSpec((None, None, nb, KV_WIDTH), lambda i, c: (i, c, 0, 0)),
        out_shape=jax.ShapeDtypeStruct((b, 2, nb, KV_WIDTH), F32),
        scratch_shapes=[pltpu.VMEM((CMP_STRIDE, nb, KV_WIDTH), BF16)],
        compiler_params=_cparams(("parallel", "parallel")), name="compress_prompt",
    )(kv, w1bd, w2bd, pe4)


def _alibi_slopes():
    return [2.0 ** (-8.0 * (n + 1) / N_Q) for n in range(N_Q)]


SEL_CHUNK = 512


def _nsa_body(q_ref, ks_ref, kw_ref, vst_ref, vwt_ref, cmp_ref, gt_ref, bg_ref, o_ref,
              qe_sc, tot_sc, acc_sc, m_sc, l_sc, sel_sc, bias_sc, *, n_top, n_slc, kw_len):
    qb = pl.program_id(1)
    QB, CK = Q_BLOCK, SEL_CHUNK
    nq = GROUP * QB
    n_cmp = cmp_ref.shape[1]
    slopes = _alibi_slopes()
    lane_head = lax.broadcasted_iota(jnp.int32, (1, KV_WIDTH), 1) // HEAD_DIM
    t_row1 = qb * QB + lax.broadcasted_iota(jnp.int32, (1, QB), 1)
    t_row = jnp.concatenate([t_row1] * GROUP, axis=1)
    gates = jax.nn.sigmoid(gt_ref[...] + bg_ref[...])
    kc = cmp_ref[0].astype(BF16)
    vct = jnp.transpose(cmp_ref[1]).astype(BF16)
    key = lax.broadcasted_iota(jnp.int32, (CK, nq), 0)
    key_f = key.astype(F32)
    cmp_end = lax.broadcasted_iota(jnp.int32, (n_cmp, nq), 0) * CMP_STRIDE + (CMP_LEN - 1)
    mask_c = cmp_end <= t_row
    cmp_end_f = cmp_end.astype(F32)
    oj = lax.broadcasted_iota(jnp.int32, (128, n_cmp), 0)
    on = lax.broadcasted_iota(jnp.int32, (128, n_cmp), 1) * CMP_STRIDE
    overlap_t = ((on < oj * SLC_BLOCK + SLC_BLOCK) & (on + (CMP_LEN - 1) >= oj * SLC_BLOCK)
                 & (oj < n_slc)).astype(F32)
    blk = lax.broadcasted_iota(jnp.int32, (n_slc, 1), 0)
    cur = t_row1 // SLC_BLOCK
    forced = (blk == 0) | (blk == cur) | (blk == cur - 1)
    future = blk * SLC_BLOCK > t_row1

    def slope_row(g):
        return jnp.concatenate([jnp.full((1, QB), slopes[GROUP * g + r], F32) for r in range(GROUP)], axis=1)

    def gate_row(c, g):
        lo = c * N_Q + GROUP * g
        return jnp.concatenate([gates[lo + r:lo + r + 1, :] for r in range(GROUP)], axis=1)

    def head_rows(g):
        return slice(g * HEAD_DIM, (g + 1) * HEAD_DIM)


    for g in range(N_KV):
        gm = lane_head == g
        zero = jnp.zeros((QB, KV_WIDTH), BF16)
        qe = jnp.concatenate(
            [jnp.where(gm, q_ref[:, r * KV_WIDTH:(r + 1) * KV_WIDTH], zero) for r in range(GROUP)],
            axis=0) * (HEAD_DIM ** -0.5)
        qe_sc[g] = qe
        s = _dot_nt(kc, qe) + slope_row(g) * cmp_end_f
        s = jnp.where(mask_c, s, NEG)
        e = jnp.where(mask_c, jnp.exp(s - jnp.max(s, axis=0, keepdims=True)), 0.0)
        p_c = e / jnp.maximum(jnp.sum(e, axis=0, keepdims=True), 1e-30)
        tot_sc[g] = gate_row(0, g) * _dot(vct[head_rows(g)], p_c.astype(BF16))
        p_sum = p_c[:, 0:QB] + p_c[:, QB:2 * QB] + p_c[:, 2 * QB:3 * QB]
        imp = jnp.dot(overlap_t, p_sum, precision=HIGHEST, preferred_element_type=F32)[:n_slc]
        imp = jnp.where(forced, BIG, jnp.where(future, -BIG, imp))
        for j in range(n_slc):
            row_j = imp[j:j + 1, :]
            beats = (imp > row_j) | ((imp == row_j) & (blk < j))
            rank = jnp.sum(beats.astype(F32), axis=0, keepdims=True)
            sel_sc[g, j:j + 1, :] = jnp.where(rank < n_top, 0.0, NEG)
        m_sc[g] = jnp.full((1, nq), NEG, F32)
        l_sc[g] = jnp.zeros((1, nq), F32)
        acc_sc[g] = jnp.zeros((HEAD_DIM, nq), F32)
        bias_sc[g] = slope_row(g) * key_f

    def sel_chunk(c, diagonal):
        base = pl.multiple_of(c * CK, CK)
        k = ks_ref[pl.ds(base, CK), :]
        for g in range(N_KV):
            picked = jnp.concatenate(
                [jnp.broadcast_to(sel_sc[g, pl.ds(c * (CK // SLC_BLOCK) + i, 1), :], (SLC_BLOCK, QB))
                 for i in range(CK // SLC_BLOCK)], axis=0)
            picked = jnp.concatenate([picked] * GROUP, axis=1)
            s = _dot_nt(k, qe_sc[g]) + (bias_sc[g] + picked)
            if diagonal:
                s = jnp.where(key <= t_row - base, s, NEG)
            off = slope_row(g) * base.astype(F32)
            m_old = m_sc[g]
            m_new = jnp.maximum(m_old, jnp.max(s, axis=0, keepdims=True) + off)
            alpha = jnp.exp(m_old - m_new)
            p = jnp.exp(s - (m_new - off))
            l_sc[g] = alpha * l_sc[g] + jnp.sum(p, axis=0, keepdims=True)
            acc_sc[g] = alpha * acc_sc[g] + _dot(vst_ref[head_rows(g), pl.ds(base, CK)], p.astype(BF16))
            m_sc[g] = m_new

    def full_chunk(c, carry):
        sel_chunk(c, False)
        return carry

    n_full = (qb * QB) // CK
    lax.fori_loop(0, n_full, full_chunk, 0)
    sel_chunk(n_full, True)

    nkw = kw_len
    start = pl.multiple_of(jnp.maximum(qb * QB + QB - nkw, 0), QB)
    key_w = lax.broadcasted_iota(jnp.int32, (nkw, nq), 0)
    rel = t_row - start
    mask_w = (key_w <= rel) & (key_w >= rel - WINDOW)
    key_w_f = key_w.astype(F32)
    k_w = kw_ref[pl.ds(start, nkw), :]
    for g in range(N_KV):
        o_s = acc_sc[g] / jnp.maximum(l_sc[g], 1e-30)
        s = _dot_nt(k_w, qe_sc[g]) + slope_row(g) * key_w_f
        s = jnp.where(mask_w, s, NEG)
        p = jnp.exp(s - jnp.max(s, axis=0, keepdims=True))
        o_w = (_dot(vwt_ref[head_rows(g), pl.ds(start, nkw)], p.astype(BF16))
               / jnp.sum(p, axis=0, keepdims=True))
        tot = tot_sc[g] + gate_row(1, g) * o_s + gate_row(2, g) * o_w
        for r in range(GROUP):
            o_ref[r * KV_WIDTH + g * HEAD_DIM:r * KV_WIDTH + (g + 1) * HEAD_DIM, :] = (
                tot[:, r * QB:(r + 1) * QB].astype(o_ref.dtype))


def nsa_prompt(q, kb, vt, cmp, g_t, b_gate, b, t):
    n_slc = t // SLC_BLOCK
    n_top = min(N_SEL, n_slc)
    nqb = t // Q_BLOCK
    nq = GROUP * Q_BLOCK
    return pl.pallas_call(
        functools.partial(_nsa_body, n_top=n_top, n_slc=n_slc, kw_len=min(WINDOW + Q_BLOCK, t)),
        grid=(b, nqb),
        in_specs=[pl.BlockSpec((Q_BLOCK, TOK_WIDTH), lambda i, j: (i * nqb + j, 0)),
                  pl.BlockSpec((t, KV_WIDTH), lambda i, j: (i, 0)),
                  pl.BlockSpec((t, KV_WIDTH), lambda i, j: (i, 1)),
                  pl.BlockSpec((KV_WIDTH, t), lambda i, j: (0, i)),
                  pl.BlockSpec((KV_WIDTH, t), lambda i, j: (1, i)),
                  pl.BlockSpec((None, 2, t // CMP_STRIDE, KV_WIDTH), lambda i, j: (i, 0, 0, 0)),
                  pl.BlockSpec((3 * N_Q, Q_BLOCK), lambda i, j: (0, i * nqb + j)),
                  pl.BlockSpec((3 * N_Q, 1), lambda i, j: (0, 0))],
        out_specs=pl.BlockSpec((TOK_WIDTH, Q_BLOCK), lambda i, j: (0, i * nqb + j)),
        out_shape=jax.ShapeDtypeStruct((TOK_WIDTH, b * t), BF16),
        scratch_shapes=[pltpu.VMEM((N_KV, nq, KV_WIDTH), BF16), pltpu.VMEM((N_KV, HEAD_DIM, nq), F32),
                        pltpu.VMEM((N_KV, HEAD_DIM, nq), F32), pltpu.VMEM((N_KV, 1, nq), F32),
                        pltpu.VMEM((N_KV, 1, nq), F32), pltpu.VMEM((N_KV, n_slc, Q_BLOCK), F32),
                        pltpu.VMEM((N_KV, SEL_CHUNK, nq), F32)],
        compiler_params=_cparams(("parallel", "arbitrary")), name="nsa_prompt",
    )(q, kb, kb, vt, vt, cmp, g_t, b_gate.reshape(-1, 1))


def _compress_paged_body(pt_ref, pool_ref, w1_ref, w2_ref, pe_ref, o_ref, raw, ybuf, sems,
                         *, layer, n_pages):
    page = raw.shape[2]
    pages_per_grp = PERM_ROWS // page
    n_grp = n_pages // pages_per_grp
    step = pl.program_id(0) * 2 + pl.program_id(1)
    n_steps = pl.num_programs(0) * 2

    def page_copy(stp, p):
        seq, typ = stp // 2, stp % 2
        return pltpu.make_async_copy(
            pool_ref.at[layer, pt_ref[seq * n_pages + p], pl.ds(typ * KV_WIDTH, KV_WIDTH), :],
            raw.at[p], sems.at[p // pages_per_grp])

    def start_all(stp):
        def body(p, carry):
            page_copy(stp, p).start()
            return carry
        lax.fori_loop(0, n_pages, body, 0)

    @pl.when(step == 0)
    def _():
        start_all(step)

    perm = _regroup_perm()

    unroll = 4 if n_grp % 4 == 0 else 1

    def regroup(it, carry):
        for j in range(unroll * pages_per_grp):
            page_copy(step, it * unroll * pages_per_grp + j).wait()
        for j in range(unroll):
            grp = it * unroll + j
            x_t = jnp.concatenate([raw[grp * pages_per_grp + k] for k in range(pages_per_grp)], axis=1)
            _regroup_rows(perm, x_t.astype(BF16), ybuf, grp, transposed=True)
        return carry

    lax.fori_loop(0, n_grp // unroll, regroup, 0)

    @pl.when(step + 1 < n_steps)
    def _():
        start_all(step + 1)

    _compress_from(lambda p: ybuf[p], ybuf.shape[1], w1_ref, w2_ref, pe_ref, o_ref)


def compress_paged(pool, page_table, layer, w1bd, w2bd, pe4):
    b, n_pages = page_table.shape
    page = pool.shape[3]
    assert PERM_ROWS % page == 0 and n_pages % (PERM_ROWS // page) == 0
    nb = n_pages * page // CMP_STRIDE
    grid_spec = pltpu.PrefetchScalarGridSpec(
        num_scalar_prefetch=1, grid=(b, 2),
        in_specs=[pl.BlockSpec(memory_space=pl.ANY),
                  pl.BlockSpec((None, CMP_LEN, KV_WIDTH, KV_WIDTH), lambda i, c, pt: (c, 0, 0, 0)),
                  pl.BlockSpec((None, KV_WIDTH, KV_WIDTH), lambda i, c, pt: (c, 0, 0)),
                  pl.BlockSpec((None, CMP_LEN, KV_WIDTH), lambda i, c, pt: (c, 0, 0))],
        out_specs=pl.BlockSpec((None, None, nb, KV_WIDTH), lambda i, c, pt: (i, c, 0, 0)),
        scratch_shapes=[pltpu.VMEM((n_pages, KV_WIDTH, page), F32),
                        pltpu.VMEM((CMP_STRIDE, nb, KV_WIDTH), BF16),
                        pltpu.SemaphoreType.DMA((n_pages * page // PERM_ROWS,))])
    return pl.pallas_call(
        functools.partial(_compress_paged_body, layer=layer, n_pages=n_pages), grid_spec=grid_spec,
        out_shape=jax.ShapeDtypeStruct((b, 2, nb, KV_WIDTH), F32),
        compiler_params=_cparams(("arbitrary", "arbitrary")), name="compress_paged",
    )(page_table.reshape(-1), pool, w1bd, w2bd, pe4)


def _decode_queries(q_ref):
    lane_head = lax.broadcasted_iota(jnp.int32, (1, KV_WIDTH), 1) // HEAD_DIM
    row4 = lax.broadcasted_iota(jnp.int32, (N_KV, 1), 0)
    parts = []
    for r in range(GROUP):
        q_r = jnp.broadcast_to(q_ref[:, r * KV_WIDTH:(r + 1) * KV_WIDTH], (N_KV, KV_WIDTH))
        parts.append(jnp.where(lane_head == row4, q_r, 0.0))
    parts.append(jnp.zeros((16 - N_Q, KV_WIDTH), F32))
    qe = (jnp.concatenate(parts, axis=0) * (HEAD_DIM ** -0.5)).astype(BF16)
    row = lax.broadcasted_iota(jnp.int32, (16, 1), 0)
    slopes = _alibi_slopes()
    slope_col = jnp.zeros((16, 1), F32)
    for r in range(GROUP):
        for g in range(N_KV):
            slope_col = jnp.where(row == N_KV * r + g, slopes[GROUP * g + r], slope_col)
    return qe, slope_col, row % N_KV, lane_head


def _nsa_select_body(q_ref, cmp_ref, oc_ref, idx_ref, *, t_pos, n_slc, n_top):
    nc = cmp_ref.shape[1]
    nj = idx_ref.shape[1]
    qe, slope_col, _, _ = _decode_queries(q_ref)
    kc = cmp_ref[0].astype(BF16)
    vc = cmp_ref[1].astype(BF16)
    cmp_end = lax.broadcasted_iota(jnp.int32, (1, nc), 1) * CMP_STRIDE + (CMP_LEN - 1)
    mask = cmp_end <= t_pos
    s = _dot_nt(qe, kc) + slope_col * cmp_end.astype(F32)
    s = jnp.where(mask, s, NEG)
    e = jnp.where(mask, jnp.exp(s - jnp.max(s, axis=-1, keepdims=True)), 0.0)
    p = e / jnp.maximum(jnp.sum(e, axis=-1, keepdims=True), 1e-30)
    oc_ref[...] = _dot(p.astype(BF16), vc)
    p_sum = p[0:N_KV] + p[N_KV:2 * N_KV] + p[2 * N_KV:3 * N_KV]
    p_sum = jnp.concatenate([p_sum, jnp.zeros((8 - N_KV, nc), F32)], axis=0)
    oj = lax.broadcasted_iota(jnp.int32, (nj, nc), 0) * SLC_BLOCK
    on = lax.broadcasted_iota(jnp.int32, (nj, nc), 1) * CMP_STRIDE
    overlap_t = ((on < oj + SLC_BLOCK) & (on + (CMP_LEN - 1) >= oj)).astype(F32)
    imp = jnp.transpose(jnp.dot(overlap_t, jnp.transpose(p_sum), precision=HIGHEST,
                                preferred_element_type=F32))
    j = lax.broadcasted_iota(jnp.int32, (8, nj), 1)
    cur = t_pos // SLC_BLOCK
    forced = (j == 0) | (j == cur) | (j == cur - 1)
    imp = jnp.where(forced, BIG, jnp.where(j * SLC_BLOCK > t_pos, -BIG, imp))
    imp = jnp.where(j < n_slc, imp, -jnp.inf)
    picked = jnp.zeros((8, nj), jnp.int32)
    for i in range(n_top):
        mx = jnp.max(imp, axis=-1, keepdims=True)
        arg = jnp.min(jnp.where(imp == mx, j, nj), axis=-1, keepdims=True)
        picked = jnp.where(j == i, arg, picked)
        imp = jnp.where(j == arg, -jnp.inf, imp)
    idx_ref[...] = picked


def nsa_select(q, cmp, t_pos, n_slc, n_top):
    b = q.shape[0]
    nc = cmp.shape[2]
    nj = -(-n_slc // 128) * 128
    return pl.pallas_call(
        functools.partial(_nsa_select_body, t_pos=t_pos, n_slc=n_slc, n_top=n_top), grid=(b,),
        in_specs=[pl.BlockSpec((None, 1, TOK_WIDTH), lambda i: (i, 0, 0)),
                  pl.BlockSpec((None, 2, nc, KV_WIDTH), lambda i: (i, 0, 0, 0))],
        out_specs=[pl.BlockSpec((None, 16, KV_WIDTH), lambda i: (i, 0, 0)),
                   pl.BlockSpec((None, 8, nj), lambda i: (i, 0, 0))],
        out_shape=[jax.ShapeDtypeStruct((b, 16, KV_WIDTH), F32),
                   jax.ShapeDtypeStruct((b, 8, nj), jnp.int32)],
        compiler_params=_cparams(("parallel",)), name="nsa_select",
    )(q, cmp)


def _nsa_decode_body(pt_ref, idx_ref, q_ref, kvn_ref, win_ref, oc_ref, gp_ref, bg_ref, pool_ref, o_ref,
                     blkbuf, sem, *, layer, t_pos, n_top, n_pages, per_page):
    n_past_blocks = n_pages * per_page
    b = pl.program_id(0)

    slot = b % 2

    def blk_copy(seq, k):
        bid = jnp.minimum(idx_ref[seq * N_KV * n_top + k], n_past_blocks - 1)
        return pltpu.make_async_copy(
            pool_ref.at[layer, pt_ref[seq * n_pages + bid // per_page], pl.ds(2 * KV_WIDTH, 2 * KV_WIDTH), :],
            blkbuf.at[seq % 2, k], sem.at[seq % 2])

    def start_all(seq):
        for k in range(N_KV * n_top):
            blk_copy(seq, k).start()

    @pl.when(b == 0)
    def _():
        start_all(b)

    @pl.when(b + 1 < pl.num_programs(0))
    def _():
        start_all(b + 1)

    qe, slope_col, row_group, lane_head = _decode_queries(q_ref)
    qf = qe.astype(F32)
    kvn = kvn_ref[...]
    new = lambda c: kvn[:, c * KV_WIDTH:(c + 1) * KV_WIDTH]
    t_f = float(t_pos)

    def attend(k_t, v_t, mask, pos_f, k_new, v_new):
        s = _dot(qe, k_t.astype(BF16)) + slope_col * pos_f
        s = jnp.where(mask, s, NEG)
        s_new = (jnp.sum(qf * k_new.astype(BF16).astype(F32), axis=-1, keepdims=True) + slope_col * t_f)
        m = jnp.maximum(jnp.max(s, axis=-1, keepdims=True), s_new)
        p = jnp.where(mask, jnp.exp(s - m), 0.0)
        p_new = jnp.exp(s_new - m)
        l = jnp.sum(p, axis=-1, keepdims=True) + p_new
        return (_dot_nt(p.astype(BF16), v_t.astype(BF16)) + p_new * v_new) / l

    wbuf = win_ref.shape[1]
    pos_w = (t_pos - wbuf) + lax.broadcasted_iota(jnp.int32, (1, wbuf), 1)
    mask_w = (pos_w >= 0) & (t_pos - pos_w <= WINDOW)
    o_w = attend(win_ref[:KV_WIDTH, :], win_ref[KV_WIDTH:, :], mask_w, pos_w.astype(F32), new(4), new(5))

    for k in range(N_KV * n_top):
        blk_copy(b, k).wait()

    page = blkbuf.shape[3]
    o_s = jnp.zeros((16, KV_WIDTH), F32)
    key = lax.broadcasted_iota(jnp.int32, (1, n_top * page), 1)
    for g in range(N_KV):
        pos = jnp.full((1, n_top * page), t_pos + 1, jnp.int32)
        for i in range(n_top):
            bid = idx_ref[(b * N_KV + g) * n_top + i]
            base = jnp.where(bid < n_past_blocks, (bid // per_page) * page, t_pos + 1)
            in_blk = (key // page == i) & ((key % page) // SLC_BLOCK == bid % per_page)
            pos = jnp.where(in_blk, base + key % page, pos)
        k_t = jnp.concatenate([blkbuf[slot, g * n_top + i, :KV_WIDTH, :] for i in range(n_top)], axis=1)
        v_t = jnp.concatenate([blkbuf[slot, g * n_top + i, KV_WIDTH:, :] for i in range(n_top)], axis=1)
        o_g = attend(k_t, v_t, pos <= t_pos, pos.astype(F32), new(2), new(3))
        o_s = jnp.where(row_group == g, o_g, o_s)

    gates = jax.nn.sigmoid(gp_ref[...] + bg_ref[...])
    row = lax.broadcasted_iota(jnp.int32, (16, 1), 0)
    tot = jnp.zeros((16, KV_WIDTH), F32)
    for c, o in enumerate((oc_ref[...], o_s, o_w)):
        gcol = jnp.zeros((16, 1), F32)
        for r in range(GROUP):
            for g in range(N_KV):
                lane = c * N_Q + GROUP * g + r
                gcol = jnp.where(row == N_KV * r + g, gates[:, lane:lane + 1], gcol)
        tot = tot + gcol * o
    out = []
    for r in range(GROUP):
        h_r = jnp.zeros((1, KV_WIDTH), F32)
        for g in range(N_KV):
            h_r = jnp.where(lane_head == g, tot[N_KV * r + g:N_KV * r + g + 1], h_r)
        out.append(h_r)
    o_ref[...] = jnp.concatenate(out, axis=1)


def nsa_decode(pool, page_table, layer, idx, q, kv_new, win, o_c, g_pre, b_gate, t_pos, n_top):
    b, n_pages = page_table.shape
    page = pool.shape[3]
    per_page = page // SLC_BLOCK
    wbuf = win.shape[2]
    row = lambda w: pl.BlockSpec((None, 1, w), lambda s, pt, ix: (s, 0, 0))
    grid_spec = pltpu.PrefetchScalarGridSpec(
        num_scalar_prefetch=2, grid=(b,),
        in_specs=[row(TOK_WIDTH), row(6 * KV_WIDTH),
                  pl.BlockSpec((None, 2 * KV_WIDTH, wbuf), lambda s, pt, ix: (s, 0, 0)),
                  pl.BlockSpec((None, 16, KV_WIDTH), lambda s, pt, ix: (s, 0, 0)),
                  row(3 * N_Q),
                  pl.BlockSpec((1, 3 * N_Q), lambda s, pt, ix: (0, 0)),
                  pl.BlockSpec(memory_space=pl.ANY)],
        out_specs=row(TOK_WIDTH),
        scratch_shapes=[pltpu.VMEM((2, N_KV * n_top, 2 * KV_WIDTH, page), F32),
                        pltpu.SemaphoreType.DMA((2,))])
    idx_flat = idx[:, :N_KV, :n_top].reshape(-1)
    return pl.pallas_call(
        functools.partial(_nsa_decode_body, layer=layer, t_pos=t_pos, n_top=n_top, n_pages=n_pages,
                          per_page=per_page),
        grid_spec=grid_spec, out_shape=jax.ShapeDtypeStruct((b, 1, TOK_WIDTH), F32),
        compiler_params=_cparams(("arbitrary",)), name="nsa_decode",
    )(page_table.reshape(-1), idx_flat, q, kv_new, win, o_c, g_pre, b_gate.reshape(1, -1), pool)


def _rgd_perm():
    idx = []
    for r in range(GROUP):
        for g in range(N_KV):
            n = GROUP * g + r
            idx.extend(range(n * HEAD_DIM, (n + 1) * HEAD_DIM))
    return jnp.array(idx, jnp.int32)


def _cols_to_rgd(w):
    lead = w.shape[:-1]
    w = w.reshape(lead + (N_KV, GROUP, HEAD_DIM))
    return jnp.swapaxes(w, -3, -2).reshape(lead + (TOK_WIDTH,))


def _block_diag4(w):
    eye = jnp.eye(N_KV, dtype=w.dtype)
    out = jnp.einsum('ab,...ij->...aibj', eye, w)
    return out.reshape(w.shape[:-2] + (KV_WIDTH, KV_WIDTH))


MOE_TILE = 1024
MOE_TILES_PER_STEP = 2


def kernel(x_prompt, x_sample, cache_mem_k, cache_mem_v, state_mlstm_C, state_mlstm_n, state_mlstm_m,
           state_mlstm_conv, cache_nsa, cache_nsa_win, page_table, mem_prompt, norm_mix, norm_mem, norm_ffn,
           norm_final, w_mem_kv, w_in_a, b_gate_a, conv_a, head_norm_a, w_in_b, b_gate_b, w_cmp1, w_cmp2,
           pe_cmp, w_out, w_ffn_gu, w_ffn_d, w_router, w_moe_gu, w_moe_d):
    bp, seq, d = x_prompt.shape
    bs, dseq, _ = x_sample.shape
    assert dseq == 1 and norm_mix.shape[0] == 2 and w_in_a.shape[0] == 1 and w_in_b.shape[0] == 1
    n_mem = mem_prompt.shape[1]
    bf = lambda a: a.astype(BF16)
    xp = x_prompt.reshape(bp * seq, d).astype(F32)
    xs = x_sample.reshape(bs, d).astype(F32)
    mem = mem_prompt.reshape(bp * n_mem, d).astype(F32)

    def mem_kv(i):
        w = w_mem_kv[i]
        mk, mv = norm_proj(mem, norm_mem[i], [bf(w[:, :X_WIDTH]), bf(w[:, X_WIDTH:])], [False, False])
        return mk.reshape(bp, n_mem, X_WIDTH), mv.reshape(bp, n_mem, X_WIDTH)

    def mix_out(i, x, h, a, wh):
        return out_proj(x, h.reshape(x.shape[0], -1), a.reshape(x.shape[0], -1), wh, bf(w_out[i][TOK_WIDTH:]))

    wa = w_in_a[0]
    t3, t4 = 3 * TOK_WIDTH, 4 * TOK_WIDTH
    w0 = [bf(_pad_heads(wa[:, :TOK_WIDTH])), bf(_pad_heads(wa[:, TOK_WIDTH:2 * TOK_WIDTH])),
          bf(_pad_heads(wa[:, 2 * TOK_WIDTH:t3])), bf(_pad_heads(wa[:, t3:t4])),
          bf(wa[:, t4 + 2 * M_HEADS:]), wa[:, t4:t4 + 2 * M_HEADS]]
    hi0 = [False] * 5 + [True]
    cwq = _pad_heads(conv_a[0][:, :TOK_WIDTH])
    cwk = _pad_heads(conv_a[0][:, TOK_WIDTH:])
    head_g = _pad_heads(head_norm_a[0]).reshape(1, M_PAD_WIDTH)
    wh0 = w_out[0][:TOK_WIDTH].reshape(M_HEADS, M_HEAD_DIM, d)
    wh0 = bf(jnp.pad(wh0, ((0, 0), (0, M_HEAD_PAD - M_HEAD_DIM), (0, 0))).reshape(M_PAD_WIDTH, d))
    mk0, mv0 = mem_kv(0)

    qp, kp, vp, op, xqp, gp = norm_proj(xp, norm_mix[0], w0, hi0)
    r3 = lambda a: a.reshape(bp, seq, a.shape[-1])
    hp, c_p, n_p, m_p = mlstm_prompt(r3(qp), r3(kp), r3(vp), r3(op), r3(gp), cwq, cwk, b_gate_a[0], head_g)
    conv_p = jnp.concatenate([_unpad_heads(r3(qp)[:, seq - (CONV_W - 1):]),
                              _unpad_heads(r3(kp)[:, seq - (CONV_W - 1):])], axis=-1)
    ap = mem_attention(r3(xqp), mk0, mv0)
    xp = mix_out(0, xp, hp, ap, wh0)
    xp = ffn(xp, norm_ffn[0], bf(w_ffn_gu[0]), bf(w_ffn_d[0]))

    qs, ks_, vs, os_, xqs, gs = norm_proj(xs, norm_mix[0], w0, hi0)
    s3 = lambda a: a.reshape(bs, 1, a.shape[-1])
    hist = state_mlstm_conv[0].astype(F32)
    hs, c_s, n_s, m_s = mlstm_step(
        s3(qs), s3(ks_), s3(vs), s3(os_), s3(gs), _pad_heads(hist[..., :TOK_WIDTH]),
        _pad_heads(hist[..., TOK_WIDTH:]), state_mlstm_C[0].astype(F32), state_mlstm_n[0].astype(F32),
        state_mlstm_m[0].astype(F32), cwq, cwk, b_gate_a[0], head_g)
    qk_new = jnp.concatenate([_unpad_heads(s3(qs)), _unpad_heads(s3(ks_))], axis=-1)
    conv_s = jnp.concatenate([hist, qk_new], axis=1)[:, 1:]
    as_ = mem_attention(s3(xqs), cache_mem_k[0].reshape(bs, n_mem, X_WIDTH).astype(F32),
                        cache_mem_v[0].reshape(bs, n_mem, X_WIDTH).astype(F32))
    xs = mix_out(0, xs, hs, as_, wh0)
    xs = ffn(xs, norm_ffn[0], bf(w_ffn_gu[0]), bf(w_ffn_d[0]))

    wb = w_in_b[0]
    kv_end = TOK_WIDTH + 6 * KV_WIDTH
    w1 = [bf(_cols_to_rgd(wb[:, :TOK_WIDTH])), bf(wb[:, TOK_WIDTH:kv_end]), bf(wb[:, kv_end + 3 * N_Q:]),
          wb[:, kv_end:kv_end + 3 * N_Q]]
    hi1 = [False] * 3 + [True]
    wh1 = w_out[1][:TOK_WIDTH].reshape(N_KV, GROUP, HEAD_DIM, d)
    wh1 = bf(jnp.swapaxes(wh1, 0, 1).reshape(TOK_WIDTH, d))
    w1bd = bf(_block_diag4(w_cmp1[0].reshape(2, CMP_LEN, HEAD_DIM, HEAD_DIM)))
    w2bd = bf(_block_diag4(w_cmp2[0]))
    pe4 = jnp.tile(pe_cmp[0], (1, 1, N_KV))
    mk1, mv1 = mem_kv(1)

    v_cols = lambda c: wb[:, TOK_WIDTH + c * KV_WIDTH:TOK_WIDTH + (c + 1) * KV_WIDTH]
    wvt = bf(jnp.concatenate([v_cols(3), v_cols(5)], axis=1).T)
    q1, kv1, kb1, vt1, xq1, gt1 = nsa_proj(xp, norm_mix[1], w1[0], w1[1], wvt, w1[2], w1[3].T)
    cmp_p = compress_prompt(r3(kv1), w1bd, w2bd, pe4)
    hp1t = nsa_prompt(q1, kb1, vt1, cmp_p, gt1, b_gate_b[0], bp, seq)
    rows_p = r3(kv1)[:, :, :4 * KV_WIDTH].reshape(bp, seq, 4, N_KV, HEAD_DIM)
    wlen = min(WINDOW, seq)
    win_p = r3(kv1)[:, seq - wlen:, 4 * KV_WIDTH:].reshape(bp, wlen, 2, N_KV, HEAD_DIM)
    ap1 = mem_attention(r3(xq1), mk1, mv1)
    xp = out_proj_t(xp, hp1t, ap1.reshape(bp * seq, X_WIDTH), wh1, bf(w_out[1][TOK_WIDTH:]))

    n_pool, page = cache_nsa.shape[1], cache_nsa.shape[2]
    past_len = page_table.shape[1] * page
    wbuf = cache_nsa_win.shape[2]
    n_slc = -(-(past_len + 1) // SLC_BLOCK)
    n_top = min(N_SEL, n_slc)
    pool = jnp.transpose(cache_nsa.astype(F32), (0, 1, 3, 4, 5, 2)).reshape(
        cache_nsa.shape[0], n_pool, 4 * KV_WIDTH, page)
    q1s, kv1s, xq1s, g1s = norm_proj(xs, norm_mix[1], w1, hi1)
    cmp_s = compress_paged(pool, page_table, 0, w1bd, w2bd, pe4)
    oc_s, idx_s = nsa_select(s3(q1s), cmp_s, past_len, n_slc, n_top)
    win_old = cache_nsa_win[0].astype(F32)
    win_t = jnp.transpose(win_old, (0, 2, 3, 4, 1)).reshape(bs, 2 * KV_WIDTH, wbuf)
    hs1 = nsa_decode(pool, page_table, 0, idx_s, s3(q1s), s3(kv1s), win_t,
                     oc_s, s3(g1s), b_gate_b[0], past_len, n_top)
    rows_s = kv1s[:, :4 * KV_WIDTH].reshape(bs, 1, 4, N_KV, HEAD_DIM)
    win_new = kv1s[:, 4 * KV_WIDTH:].reshape(bs, 1, 2, N_KV, HEAD_DIM)
    win_s = jnp.concatenate([win_old, win_new], axis=1)[:, -wbuf:]
    as1 = mem_attention(s3(xq1s), cache_mem_k[1].reshape(bs, n_mem, X_WIDTH).astype(F32),
                        cache_mem_v[1].reshape(bs, n_mem, X_WIDTH).astype(F32))
    xs = mix_out(1, xs, hs1, as1, wh1)

    w_r = jnp.pad(w_router[0], ((0, 0), (0, 128 - N_EXPERTS)))
    w_gu, w_dn = bf(w_moe_gu[0]), bf(w_moe_d[0])

    def experts(x, tm):
        u, comb, rank, counts = router(x, norm_ffn[1], w_r, tm)
        return final_norm(x, moe(u, comb, rank, counts, w_gu, w_dn, tm), norm_final)

    y_prompt = experts(xp, min(MOE_TILE, bp * seq)).reshape(bp, seq, d)
    y_sample = experts(xs, bs).reshape(bs, 1, d)

    unflat = lambda a: a.reshape(bp, n_mem, X_HEADS, HEAD_DIM)
    return (y_prompt, y_sample,
            jnp.stack([unflat(mk0), unflat(mk1)]), jnp.stack([unflat(mv0), unflat(mv1)]),
            c_p[None], n_p[None], m_p[None, :, :, 0], conv_p[None], rows_p[None], win_p[None],
            c_s[None], n_s[None], m_s[None, :, :, 0], conv_s[None], rows_s[None], win_s[None])
```

```python
import functools

import jax
import jax.numpy as jnp
from jax import lax
from jax.experimental import pallas as pl
from jax.experimental.pallas import tpu as pltpu

F32 = jnp.float32
BF16 = jnp.bfloat16
HIGHEST = lax.Precision.HIGHEST

D_MODEL = 1024
HEAD_DIM = 64
X_HEADS = 4
X_WIDTH = X_HEADS * HEAD_DIM
TOK_WIDTH = D_MODEL - X_WIDTH
M_HEADS = 8
M_HEAD_DIM = TOK_WIDTH // M_HEADS
M_HEAD_PAD = 128
M_PAD_WIDTH = M_HEADS * M_HEAD_PAD
CONV_W = 4
N_Q = TOK_WIDTH // HEAD_DIM
N_KV = 4
GROUP = N_Q // N_KV
KV_WIDTH = N_KV * HEAD_DIM
CMP_STRIDE = 16
CMP_LEN = 2 * CMP_STRIDE
SLC_BLOCK = 64
N_SEL = 16
WINDOW = 512
Q_BLOCK = 128
D_FF = 2816
N_EXPERTS = 8
TOP_K = 2
E_FF = 7 * D_MODEL // 4
RMS_EPS = 1e-6
NEG = -1e30
BIG = 1e9

LANES = 128
MLSTM_CHUNK = 128
VMEM_LIMIT = 56 * 1024 * 1024


def _cparams(sem):
    return pltpu.CompilerParams(dimension_semantics=sem, vmem_limit_bytes=VMEM_LIMIT)


def _rms(x, g):
    return x * lax.rsqrt(jnp.mean(x * x, axis=-1, keepdims=True) + RMS_EPS) * g


def _dot(a, b):
    return jnp.dot(a, b, preferred_element_type=F32)


def _dot_nt(a, b):
    return lax.dot_general(a, b, (((1,), (1,)), ((), ())), preferred_element_type=F32)


def _dot_tn(a, b):
    return lax.dot_general(a, b, (((0,), (0,)), ((), ())), preferred_element_type=F32)


def _norm_proj_body(x_ref, g_ref, *refs, n_out, hi):
    w_refs, o_refs = refs[:n_out], refs[n_out:]
    u = _rms(x_ref[...], g_ref[...])
    ub = u.astype(BF16)
    for w_ref, o_ref, h in zip(w_refs, o_refs, hi):
        if h:
            o_ref[...] = jnp.dot(u, w_ref[...], precision=HIGHEST, preferred_element_type=F32)
        else:
            o_ref[...] = _dot(ub, w_ref[...])


def norm_proj(x, gamma, weights, hi):
    m, d = x.shape
    tm = min(512, m)
    n_out = len(weights)
    in_specs = [pl.BlockSpec((tm, d), lambda i: (i, 0)), pl.BlockSpec((1, d), lambda i: (0, 0))]
    in_specs += [pl.BlockSpec(w.shape, lambda i: (0, 0)) for w in weights]
    out_specs = [pl.BlockSpec((tm, w.shape[1]), lambda i: (i, 0)) for w in weights]
    out_shape = [jax.ShapeDtypeStruct((m, w.shape[1]), F32) for w in weights]
    return pl.pallas_call(
        functools.partial(_norm_proj_body, n_out=n_out, hi=tuple(hi)),
        grid=(m // tm,), in_specs=in_specs, out_specs=out_specs, out_shape=out_shape,
        compiler_params=_cparams(("parallel",)), name="norm_proj",
    )(x, gamma.reshape(1, d), *weights)


def _mem_attn_body(q_ref, k_ref, v_ref, o_ref):
    q = q_ref[...] * (HEAD_DIM ** -0.5)
    rows = q.shape[0]
    if rows < 8:
        q = jnp.broadcast_to(q[0:1], (8, X_WIDTH))
    kb = k_ref[...].astype(BF16)
    vb = v_ref[...].astype(BF16)
    head = lax.broadcasted_iota(jnp.int32, (1, X_WIDTH), 1) // HEAD_DIM
    out = jnp.zeros(q.shape, F32)
    for h in range(X_HEADS):
        qh = jnp.where(head == h, q, 0.0).astype(BF16)
        s = _dot_nt(qh, kb)
        p = jnp.exp(s - jnp.max(s, axis=-1, keepdims=True))
        o = _dot(p.astype(BF16), vb) / jnp.sum(p, axis=-1, keepdims=True)
        out = jnp.where(head == h, o, out)
    o_ref[...] = out[:rows]


def mem_attention(q, mk, mv):
    b, t, w = q.shape
    n = mk.shape[1]
    tq = min(512, t)
    return pl.pallas_call(
        _mem_attn_body, grid=(b, t // tq),
        in_specs=[pl.BlockSpec((None, tq, w), lambda i, j: (i, j, 0)),
                  pl.BlockSpec((None, n, w), lambda i, j: (i, 0, 0)),
                  pl.BlockSpec((None, n, w), lambda i, j: (i, 0, 0))],
        out_specs=pl.BlockSpec((None, tq, w), lambda i, j: (i, j, 0)),
        out_shape=jax.ShapeDtypeStruct((b, t, w), F32),
        compiler_params=_cparams(("parallel", "parallel")), name="mem_attention",
    )(q, mk, mv)


def _out_proj_body(x_ref, h_ref, a_ref, wh_ref, wa_ref, o_ref):
    o_ref[...] = (x_ref[...] + _dot(h_ref[...].astype(BF16), wh_ref[...])
                  + _dot(a_ref[...].astype(BF16), wa_ref[...]))


def out_proj(x, h, a, wh, wa):
    m, d = x.shape
    tm = min(512, m)
    kh, ka = h.shape[1], a.shape[1]
    return pl.pallas_call(
        _out_proj_body, grid=(m // tm,),
        in_specs=[pl.BlockSpec((tm, d), lambda i: (i, 0)),
                  pl.BlockSpec((tm, kh), lambda i: (i, 0)),
                  pl.BlockSpec((tm, ka), lambda i: (i, 0)),
                  pl.BlockSpec((kh, d), lambda i: (0, 0)),
                  pl.BlockSpec((ka, d), lambda i: (0, 0))],
        out_specs=pl.BlockSpec((tm, d), lambda i: (i, 0)),
        out_shape=jax.ShapeDtypeStruct((m, d), F32),
        compiler_params=_cparams(("parallel",)), name="out_proj",
    )(x, h, a, wh, wa)


def _out_proj_t_body(x_ref, ht_ref, a_ref, wh_ref, wa_ref, o_ref):
    o_ref[...] = (x_ref[...] + _dot_tn(ht_ref[...], wh_ref[...])
                  + _dot(a_ref[...].astype(BF16), wa_ref[...]))


def out_proj_t(x, ht, a, wh, wa):
    m, d = x.shape
    tm = min(512, m)
    kh, ka = ht.shape[0], a.shape[1]
    return pl.pallas_call(
        _out_proj_t_body, grid=(m // tm,),
        in_specs=[pl.BlockSpec((tm, d), lambda i: (i, 0)),
                  pl.BlockSpec((kh, tm), lambda i: (0, i)),
                  pl.BlockSpec((tm, ka), lambda i: (i, 0)),
                  pl.BlockSpec((kh, d), lambda i: (0, 0)),
                  pl.BlockSpec((ka, d), lambda i: (0, 0))],
        out_specs=pl.BlockSpec((tm, d), lambda i: (i, 0)),
        out_shape=jax.ShapeDtypeStruct((m, d), F32),
        compiler_params=_cparams(("parallel",)), name="out_proj_t",
    )(x, ht, a, wh, wa)


def _nsa_proj_body(x_ref, g_ref, wq_ref, wkv_ref, wvt_ref, wxq_ref, wgt_ref,
                   q_ref, kv_ref, kb_ref, vt_ref, xq_ref, gt_ref):
    u = _rms(x_ref[...], g_ref[...])
    ub = u.astype(BF16)
    q_ref[...] = _dot(ub, wq_ref[...]).astype(BF16)
    kv = _dot(ub, wkv_ref[...])
    kv_ref[...] = kv
    kb_ref[...] = jnp.concatenate([kv[:, 2 * KV_WIDTH:3 * KV_WIDTH], kv[:, 4 * KV_WIDTH:5 * KV_WIDTH]],
                                  axis=1).astype(BF16)
    vt_ref[...] = _dot_nt(wvt_ref[...], ub).astype(BF16)
    xq_ref[...] = _dot(ub, wxq_ref[...])
    gt_ref[...] = lax.dot_general(wgt_ref[...], u, (((1,), (1,)), ((), ())), precision=HIGHEST,
                                  preferred_element_type=F32)


def nsa_proj(x, gamma, wq, wkv, wvt, wxq, wgt):
    m, d = x.shape
    tm = min(512, m)
    const = lambda a: pl.BlockSpec(a.shape, lambda i: (0, 0))
    rows = lambda n: pl.BlockSpec((tm, n), lambda i: (i, 0))
    cols = lambda n: pl.BlockSpec((n, tm), lambda i: (0, i))
    ng = wgt.shape[0]
    return pl.pallas_call(
        _nsa_proj_body, grid=(m // tm,),
        in_specs=[rows(d), pl.BlockSpec((1, d), lambda i: (0, 0)),
                  const(wq), const(wkv), const(wvt), const(wxq), const(wgt)],
        out_specs=[rows(TOK_WIDTH), rows(6 * KV_WIDTH), rows(2 * KV_WIDTH), cols(2 * KV_WIDTH),
                   rows(X_WIDTH), cols(ng)],
        out_shape=[jax.ShapeDtypeStruct((m, TOK_WIDTH), BF16),
                   jax.ShapeDtypeStruct((m, 6 * KV_WIDTH), F32),
                   jax.ShapeDtypeStruct((m, 2 * KV_WIDTH), BF16),
                   jax.ShapeDtypeStruct((2 * KV_WIDTH, m), BF16),
                   jax.ShapeDtypeStruct((m, X_WIDTH), F32),
                   jax.ShapeDtypeStruct((ng, m), F32)],
        compiler_params=_cparams(("parallel",)), name="nsa_proj",
    )(x, gamma.reshape(1, d), wq, wkv, wvt, wxq, wgt)


def _ffn_body(x_ref, g_ref, wg_ref, wu_ref, wd_ref, o_ref, u_sc):
    @pl.when(pl.program_id(1) == 0)
    def _():
        x = x_ref[...]
        u_sc[...] = _rms(x, g_ref[...]).astype(BF16)
        o_ref[...] = x

    u = u_sc[...]
    hg = _dot(u, wg_ref[...])
    hu = _dot(u, wu_ref[...])
    hm = (hg * jax.nn.sigmoid(hg) * hu).astype(BF16)
    o_ref[...] += _dot(hm, wd_ref[...])


def ffn(x, gamma, w_gu, w_d):
    m, d = x.shape
    f = w_d.shape[0]
    tm = min(1024, m)
    tf = f // 2 if (f // 2) % LANES == 0 else LANES * 2
    nf = f // tf
    return pl.pallas_call(
        _ffn_body, grid=(m // tm, nf),
        in_specs=[pl.BlockSpec((tm, d), lambda i, j: (i, 0)),
                  pl.BlockSpec((1, d), lambda i, j: (0, 0)),
                  pl.BlockSpec((d, tf), lambda i, j: (0, j)),
                  pl.BlockSpec((d, tf), lambda i, j: (0, nf + j)),
                  pl.BlockSpec((tf, d), lambda i, j: (j, 0))],
        out_specs=pl.BlockSpec((tm, d), lambda i, j: (i, 0)),
        out_shape=jax.ShapeDtypeStruct((m, d), F32),
        scratch_shapes=[pltpu.VMEM((tm, d), BF16)],
        compiler_params=_cparams(("parallel", "arbitrary")), name="ffn",
    )(x, gamma.reshape(1, d), w_gu, w_gu, w_d)


def _router_body(x_ref, g_ref, wr_ref, u_ref, comb_ref, rank_ref, cnt_ref):
    tm = x_ref.shape[0]
    u = _rms(x_ref[...], g_ref[...])
    u_ref[...] = u.astype(BF16)
    logits = jnp.dot(u, wr_ref[...], precision=HIGHEST, preferred_element_type=F32)
    lane = lax.broadcasted_iota(jnp.int32, (tm, 128), 1)
    logits = jnp.where(lane < N_EXPERTS, logits, -jnp.inf)
    m1 = jnp.max(logits, axis=-1, keepdims=True)
    i1 = jnp.min(jnp.where(logits == m1, lane, 128), axis=-1, keepdims=True)
    rest = jnp.where(lane == i1, -jnp.inf, logits)
    m2 = jnp.max(rest, axis=-1, keepdims=True)
    i2 = jnp.min(jnp.where(rest == m2, lane, 128), axis=-1, keepdims=True)
    e2 = jnp.exp(m2 - m1)
    g1 = 1.0 / (1.0 + e2)
    g2 = e2 / (1.0 + e2)
    comb_ref[...] = jnp.where(lane == i1, g1, 0.0) + jnp.where(lane == i2, g2, 0.0)
    sel = ((lane == i1) | (lane == i2)).astype(F32)
    r = lax.broadcasted_iota(jnp.int32, (tm, tm), 0)
    c = lax.broadcasted_iota(jnp.int32, (tm, tm), 1)
    tril = (c < r).astype(BF16)
    rank = _dot(tril, sel.astype(BF16))
    rank_ref[...] = jnp.where(sel > 0, rank, -1.0)
    cnt_ref[...] = jnp.sum(sel, axis=0, keepdims=True).astype(jnp.int32)


def router(x, gamma, w_router_pad, tm):
    m, d = x.shape
    nt = m // tm
    return pl.pallas_call(
        _router_body, grid=(nt,),
        in_specs=[pl.BlockSpec((tm, d), lambda i: (i, 0)),
                  pl.BlockSpec((1, d), lambda i: (0, 0)),
                  pl.BlockSpec((d, 128), lambda i: (0, 0))],
        out_specs=[pl.BlockSpec((tm, d), lambda i: (i, 0)),
                   pl.BlockSpec((tm, 128), lambda i: (i, 0)),
                   pl.BlockSpec((tm, 128), lambda i: (i, 0)),
                   pl.BlockSpec((None, 1, 128), lambda i: (i, 0, 0))],
        out_shape=[jax.ShapeDtypeStruct((m, d), BF16),
                   jax.ShapeDtypeStruct((m, 128), F32),
                   jax.ShapeDtypeStruct((m, 128), F32),
                   jax.ShapeDtypeStruct((nt, 1, 128), jnp.int32)],
        compiler_params=_cparams(("parallel",)), name="router",
    )(x, gamma.reshape(1, d), w_router_pad)


def _moe_body(cnt_ref, u_ref, rank_ref, rrow_ref, comb_ref, x_ref, gn_ref, wg_ref, wu_ref, wd_ref, o_ref,
              rcol_sc, gcol_sc, *, cap, tm):
    i = pl.program_id(0)
    e = pl.program_id(1)
    n_sub = u_ref.shape[0] // tm

    @pl.when(e == 0)
    def _():
        o_ref[...] = jnp.zeros(o_ref.shape, F32)

    mine = lax.broadcasted_iota(jnp.int32, (n_sub * tm, 128), 1) == e
    rcol_sc[...] = jnp.sum(jnp.where(mine, rank_ref[...], 0.0), axis=1, keepdims=True)
    gcol_sc[...] = jnp.sum(jnp.where(mine, comb_ref[...], 0.0), axis=1, keepdims=True)
    for s in range(n_sub):
        rows = slice(s * tm, (s + 1) * tm)
        cnt = cnt_ref[(i * n_sub + s) * N_EXPERTS + e]
        n_chunks = (cnt + cap - 1) // cap

        def chunk(c, carry, rows=rows):
            base = (c * cap).astype(F32)
            slot_r = lax.broadcasted_iota(jnp.int32, (cap, tm), 0).astype(F32) + base
            gather = (rrow_ref[:, rows] == slot_r).astype(BF16)
            xg = _dot(gather, u_ref[rows, :]).astype(BF16)
            hg = _dot(xg, wg_ref[...])
            hu = _dot(xg, wu_ref[...])
            hm = (hg * jax.nn.sigmoid(hg) * hu).astype(BF16)
            y = _dot(hm, wd_ref[...]).astype(BF16)
            slot_c = lax.broadcasted_iota(jnp.int32, (tm, cap), 1).astype(F32) + base
            scatter = (rcol_sc[rows, :] == slot_c).astype(BF16)
            o_ref[rows, :] += gcol_sc[rows, :] * _dot(scatter, y)
            return carry

        lax.fori_loop(0, n_chunks, chunk, 0)

    @pl.when(e == N_EXPERTS - 1)
    def _():
        o_ref[...] = _rms(x_ref[...] + o_ref[...], gn_ref[...])


def moe(u, comb, rank, counts, w_gu, w_d, tm, x, gamma):
    m, d = u.shape
    nt = m // tm
    cap = min(tm, max(16, tm * TOP_K // N_EXPERTS))
    ef = w_d.shape[1]
    rrow = rank[:, :N_EXPERTS].T.reshape(N_EXPERTS, 1, m)
    cnt = counts[:, 0, :N_EXPERTS].reshape(nt * N_EXPERTS)
    n_sub = MOE_TILES_PER_STEP if nt % MOE_TILES_PER_STEP == 0 else 1
    ts = n_sub * tm
    grid_spec = pltpu.PrefetchScalarGridSpec(
        num_scalar_prefetch=1, grid=(nt // n_sub, N_EXPERTS),
        in_specs=[pl.BlockSpec((ts, d), lambda i, e, c: (i, 0)),
                  pl.BlockSpec((ts, 128), lambda i, e, c: (i, 0)),
                  pl.BlockSpec((None, 1, ts), lambda i, e, c: (e, 0, i)),
                  pl.BlockSpec((ts, 128), lambda i, e, c: (i, 0)),
                  pl.BlockSpec((ts, d), lambda i, e, c: (i, 0)),
                  pl.BlockSpec((1, d), lambda i, e, c: (0, 0)),
                  pl.BlockSpec((None, d, ef), lambda i, e, c: (e, 0, 0)),
                  pl.BlockSpec((None, d, ef), lambda i, e, c: (e, 0, 1)),
                  pl.BlockSpec((None, ef, d), lambda i, e, c: (e, 0, 0))],
        out_specs=pl.BlockSpec((ts, d), lambda i, e, c: (i, 0)),
        scratch_shapes=[pltpu.VMEM((ts, 1), F32), pltpu.VMEM((ts, 1), F32)])
    return pl.pallas_call(
        functools.partial(_moe_body, cap=cap, tm=tm), grid_spec=grid_spec,
        out_shape=jax.ShapeDtypeStruct((m, d), F32),
        compiler_params=_cparams(("parallel", "arbitrary")), name="moe",
    )(cnt, u, rank, rrow, comb, x, gamma.reshape(1, d), w_gu, w_gu, w_d)


def _final_body(x_ref, y_ref, g_ref, o_ref):
    o_ref[...] = _rms(x_ref[...] + y_ref[...], g_ref[...])


def final_norm(x, y, gamma):
    m, d = x.shape
    tm = min(1024, m)
    return pl.pallas_call(
        _final_body, grid=(m // tm,),
        in_specs=[pl.BlockSpec((tm, d), lambda i: (i, 0)),
                  pl.BlockSpec((tm, d), lambda i: (i, 0)),
                  pl.BlockSpec((1, d), lambda i: (0, 0))],
        out_specs=pl.BlockSpec((tm, d), lambda i: (i, 0)),
        out_shape=jax.ShapeDtypeStruct((m, d), F32),
        compiler_params=_cparams(("parallel",)), name="final_norm",
    )(x, y, gamma.reshape(1, d))


def _log_sigmoid(x):
    return jnp.minimum(x, 0.0) - jnp.log(1.0 + jnp.exp(-jnp.abs(x)))


def _silu(x):
    return x * jax.nn.sigmoid(x)


def _mlstm_body(qp_ref, kp_ref, v_ref, o_ref, g_ref, gt_ref, cwq_ref, cwk_ref, bgr_ref, bgc_ref, hg_ref,
                h_ref, c_out, n_out, m_out, c_sc, n_sc, m_sc, tq_sc, tk_sc):
    j = pl.program_id(1)
    L = qp_ref.shape[0]
    P = M_HEAD_PAD

    @pl.when(j == 0)
    def _():
        c_sc[...] = jnp.zeros(c_sc.shape, F32)
        n_sc[...] = jnp.zeros(n_sc.shape, F32)
        m_sc[...] = jnp.zeros(m_sc.shape, F32)
        tq_sc[...] = jnp.zeros(tq_sc.shape, F32)
        tk_sc[...] = jnp.zeros(tk_sc.shape, F32)

    def conv(x_ref, tail_sc, cw_ref):
        x = x_ref[...]
        ext = jnp.concatenate([tail_sc[...], x], axis=0)
        y = cw_ref[3:4, :] * x
        for tap in range(CONV_W - 1):
            lo = 8 - (CONV_W - 1) + tap
            y = y + cw_ref[tap:tap + 1, :] * ext[lo:lo + L]
        tail_sc[...] = x[L - 8:L]
        return _silu(y)

    qc = conv(qp_ref, tq_sc, cwq_ref)
    kc = conv(kp_ref, tk_sc, cwk_ref) * (M_HEAD_DIM ** -0.5)

    g = g_ref[...] + bgr_ref[...]
    gt = gt_ref[...] + bgc_ref[...]
    ig_c, lf_c = g[:, :M_HEADS], _log_sigmoid(g[:, M_HEADS:])
    ig_r, lf_r = gt[:M_HEADS], _log_sigmoid(gt[M_HEADS:])
    row = lax.broadcasted_iota(jnp.int32, (L, L), 0)
    col = lax.broadcasted_iota(jnp.int32, (L, L), 1)
    causal = col <= row
    f_c = jnp.dot(causal.astype(F32), lf_c, precision=HIGHEST, preferred_element_type=F32)
    f_r = jnp.dot(lf_r, (row <= col).astype(F32), precision=HIGHEST, preferred_element_type=F32)

    for h in range(M_HEADS):
        sl = slice(h * P, (h + 1) * P)
        q, k, v = qc[:, sl], kc[:, sl], v_ref[:, sl]
        fc, fr = f_c[:, h:h + 1], f_r[h:h + 1, :]
        m_prev = m_sc[h:h + 1, 0:1]
        d = jnp.where(causal, fc - fr + ig_r[h:h + 1, :], NEG)
        a = fc + m_prev
        mt = jnp.maximum(a, jnp.max(d, axis=1, keepdims=True))
        wm = jnp.exp(d - mt)
        gcol = jnp.exp(a - mt)
        qb, kb, vb = q.astype(BF16), k.astype(BF16), v.astype(BF16)
        s = _dot_nt(qb, kb) * wm
        c_old = c_sc[h]
        n_old = n_sc[h:h + 1, :]
        num = _dot(s.astype(BF16), vb) + gcol * _dot_nt(qb, c_old.astype(BF16))
        den = jnp.sum(s, axis=1, keepdims=True) + gcol * jnp.sum(q * n_old, axis=1, keepdims=True)
        hc = num / jnp.maximum(jnp.abs(den), jnp.exp(-mt))
        og = jax.nn.sigmoid(o_ref[:, sl]) * hc
        ms = jnp.sum(og * og, axis=1, keepdims=True) * (1.0 / M_HEAD_DIM)
        h_ref[:, sl] = og * lax.rsqrt(ms + RMS_EPS) * hg_ref[:, sl]
        mt_last, gl = mt[L - 1:L], gcol[L - 1:L]
        wl = jnp.exp(fc[L - 1:L] - fc + ig_c[:, h:h + 1] - mt_last)
        c_sc[h] = gl * c_old + _dot_tn((v * wl).astype(BF16), kb)
        n_sc[h:h + 1, :] = gl * n_old + jnp.sum(wl * k, axis=0, keepdims=True)
        m_sc[h:h + 1, :] = jnp.broadcast_to(mt_last, (1, P))

    @pl.when(j == pl.num_programs(1) - 1)
    def _():
        for h in range(M_HEADS):
            c_out[h] = c_sc[h][:M_HEAD_DIM, :M_HEAD_DIM]
        n_out[...] = n_sc[:, :M_HEAD_DIM]
        m_out[...] = m_sc[...]


def mlstm_prompt(qp, kp, v, o, g, cwq, cwk, b_gate, head_g):
    b, t, w = qp.shape
    L = min(MLSTM_CHUNK, t)
    gt = jnp.swapaxes(g, 1, 2)
    seq = pl.BlockSpec((None, L, w), lambda i, j: (i, j, 0))
    const = lambda shape: pl.BlockSpec(shape, lambda i, j: (0,) * len(shape))
    return pl.pallas_call(
        _mlstm_body, grid=(b, t // L),
        in_specs=[seq, seq, seq, seq,
                  pl.BlockSpec((None, L, 2 * M_HEADS), lambda i, j: (i, j, 0)),
                  pl.BlockSpec((None, 2 * M_HEADS, L), lambda i, j: (i, 0, j)),
                  const((CONV_W, w)), const((CONV_W, w)), const((1, 2 * M_HEADS)), const((2 * M_HEADS, 1)),
                  const((1, w))],
        out_specs=[seq,
                   pl.BlockSpec((None, M_HEADS, M_HEAD_DIM, M_HEAD_DIM), lambda i, j: (i, 0, 0, 0)),
                   pl.BlockSpec((None, M_HEADS, M_HEAD_DIM), lambda i, j: (i, 0, 0)),
                   pl.BlockSpec((None, M_HEADS, M_HEAD_PAD), lambda i, j: (i, 0, 0))],
        out_shape=[jax.ShapeDtypeStruct((b, t, w), F32),
                   jax.ShapeDtypeStruct((b, M_HEADS, M_HEAD_DIM, M_HEAD_DIM), F32),
                   jax.ShapeDtypeStruct((b, M_HEADS, M_HEAD_DIM), F32),
                   jax.ShapeDtypeStruct((b, M_HEADS, M_HEAD_PAD), F32)],
        scratch_shapes=[pltpu.VMEM((M_HEADS, M_HEAD_PAD, M_HEAD_PAD), F32),
                        pltpu.VMEM((M_HEADS, M_HEAD_PAD), F32),
                        pltpu.VMEM((M_HEADS, M_HEAD_PAD), F32),
                        pltpu.VMEM((8, w), F32), pltpu.VMEM((8, w), F32)],
        compiler_params=_cparams(("parallel", "arbitrary")), name="mlstm_prompt",
    )(qp, kp, v, o, g, gt, cwq, cwk, b_gate.reshape(1, -1), b_gate.reshape(-1, 1), head_g)


def _mlstm_step_body(qp_ref, kp_ref, v_ref, o_ref, g_ref, hq_ref, hk_ref, c0_ref, n0_ref, m0_ref,
                     cwq_ref, cwk_ref, bgr_ref, hg_ref, h_ref, c_out, n_out, m_out):
    P, dh = M_HEAD_PAD, M_HEAD_DIM

    def conv(x_ref, hist_ref, cw_ref):
        y = cw_ref[3:4, :] * x_ref[...]
        for tap in range(CONV_W - 1):
            y = y + cw_ref[tap:tap + 1, :] * hist_ref[tap:tap + 1, :]
        return _silu(y)

    qc = conv(qp_ref, hq_ref, cwq_ref)
    kc = conv(kp_ref, hk_ref, cwk_ref) * (dh ** -0.5)
    g = g_ref[...] + bgr_ref[...]
    m0 = m0_ref[...]
    for h in range(M_HEADS):
        lo = h * P
        q, k, v = qc[:, lo:lo + dh], kc[:, lo:lo + dh], v_ref[:, lo:lo + dh]
        ig = g[:, h:h + 1]
        lf = _log_sigmoid(g[:, M_HEADS + h:M_HEADS + h + 1])
        a = lf + m0[h:h + 1, :]
        mt = jnp.maximum(a, ig)
        wv = jnp.exp(ig - mt)
        gv = jnp.exp(a - mt)
        qf, kf = q.astype(BF16).astype(F32), k.astype(BF16).astype(F32)
        qk = jnp.sum(qf * kf, axis=1, keepdims=True) * wv
        c_old, n_old = c0_ref[h], n0_ref[h:h + 1, :]
        q8 = jnp.broadcast_to(q, (8, dh)).astype(BF16)
        cq = _dot_nt(q8, c_old.astype(BF16))[0:1]
        num = qk * v + gv * cq
        den = qk + gv * jnp.sum(q * n_old, axis=1, keepdims=True)
        hc = num / jnp.maximum(jnp.abs(den), jnp.exp(-mt))
        og = jax.nn.sigmoid(o_ref[:, lo:lo + dh]) * hc
        ms = jnp.sum(og * og, axis=1, keepdims=True) * (1.0 / dh)
        hn = og * lax.rsqrt(ms + RMS_EPS) * hg_ref[:, lo:lo + dh]
        h_ref[:, lo:lo + P] = jnp.concatenate([hn, jnp.zeros((1, P - dh), F32)], axis=1)
        v_col = jnp.transpose(jnp.broadcast_to(v_ref[:, lo:lo + P], (8, P)))[:dh, 0:1]
        c_out[h] = gv * c_old + wv * (v_col * k)
        n_out[h:h + 1, :] = gv * n_old + wv * k
        m_out[h:h + 1, :] = jnp.broadcast_to(mt, (1, P))


def mlstm_step(qp, kp, v, o, g, hist_q, hist_k, c0, n0, m0, cwq, cwk, b_gate, head_g):
    b, _, w = qp.shape
    row = pl.BlockSpec((None, 1, w), lambda i: (i, 0, 0))
    const = lambda shape: pl.BlockSpec(shape, lambda i: (0,) * len(shape))
    cspec = pl.BlockSpec((None, M_HEADS, M_HEAD_DIM, M_HEAD_DIM), lambda i: (i, 0, 0, 0))
    nspec = pl.BlockSpec((None, M_HEADS, M_HEAD_DIM), lambda i: (i, 0, 0))
    return pl.pallas_call(
        _mlstm_step_body, grid=(b,),
        in_specs=[row, row, row, row,
                  pl.BlockSpec((None, 1, 2 * M_HEADS), lambda i: (i, 0, 0)),
                  pl.BlockSpec((None, CONV_W - 1, w), lambda i: (i, 0, 0)),
                  pl.BlockSpec((None, CONV_W - 1, w), lambda i: (i, 0, 0)),
                  cspec, nspec,
                  pl.BlockSpec((None, M_HEADS, 1), lambda i: (i, 0, 0)),
                  const((CONV_W, w)), const((CONV_W, w)), const((1, 2 * M_HEADS)), const((1, w))],
        out_specs=[row, cspec, nspec,
                   pl.BlockSpec((None, M_HEADS, M_HEAD_PAD), lambda i: (i, 0, 0))],
        out_shape=[jax.ShapeDtypeStruct((b, 1, w), F32),
                   jax.ShapeDtypeStruct(c0.shape, F32),
                   jax.ShapeDtypeStruct(n0.shape, F32),
                   jax.ShapeDtypeStruct((b, M_HEADS, M_HEAD_PAD), F32)],
        compiler_params=_cparams(("parallel",)), name="mlstm_step",
    )(qp, kp, v, o, g, hist_q, hist_k, c0, n0, m0.reshape(b, M_HEADS, 1), cwq, cwk,
      b_gate.reshape(1, -1), head_g)


def _pad_heads(a):
    lead = a.shape[:-1]
    a = a.reshape(lead + (M_HEADS, M_HEAD_DIM))
    a = jnp.pad(a, [(0, 0)] * len(lead) + [(0, 0), (0, M_HEAD_PAD - M_HEAD_DIM)])
    return a.reshape(lead + (M_PAD_WIDTH,))


def _unpad_heads(a):
    lead = a.shape[:-1]
    return a.reshape(lead + (M_HEADS, M_HEAD_PAD))[..., :M_HEAD_DIM].reshape(lead + (TOK_WIDTH,))


def _gelu_tanh(x):
    return x * (0.5 * (1.0 + jnp.tanh(0.7978845608028654 * (x + 0.044715 * (x * x * x)))))


def _compress_tail(u, v_next, pe_ref, w1_ref, w2_ref):
    bias = jnp.zeros((8, KV_WIDTH), F32)
    for p in range(CMP_LEN):
        pe8 = jnp.broadcast_to(pe_ref[p:p + 1, :], (8, KV_WIDTH)).astype(BF16)
        bias = bias + _dot(pe8, w1_ref[p])
    pre = u + v_next + bias[0:1]
    return _dot(_gelu_tanh(pre).astype(BF16), w2_ref[...])


PERM_ROWS = CMP_STRIDE * CMP_STRIDE


def _regroup_perm():
    r = lax.broadcasted_iota(jnp.int32, (PERM_ROWS, PERM_ROWS), 0)
    c = lax.broadcasted_iota(jnp.int32, (PERM_ROWS, PERM_ROWS), 1)
    return (c == (r % CMP_STRIDE) * CMP_STRIDE + r // CMP_STRIDE).astype(BF16)


def _regroup_rows(perm, x, ybuf, grp, transposed=False):
    y = (_dot_nt(perm, x) if transposed else _dot(perm, x)).astype(BF16)
    for p in range(CMP_STRIDE):
        ybuf[p, pl.ds(pl.multiple_of(grp * CMP_STRIDE, CMP_STRIDE), CMP_STRIDE), :] = (
            y[p * CMP_STRIDE:(p + 1) * CMP_STRIDE])


def _compress_from(rows_at, nb, w1_ref, w2_ref, pe_ref, o_ref):
    u = jnp.zeros((nb, KV_WIDTH), F32)
    v = jnp.zeros((nb, KV_WIDTH), F32)
    for p in range(CMP_STRIDE):
        x = rows_at(p)
        u = u + _dot(x, w1_ref[p])
        v = v + _dot(x, w1_ref[CMP_STRIDE + p])
    v_next = jnp.concatenate([v[1:], jnp.zeros((1, KV_WIDTH), F32)], axis=0)
    o_ref[...] = _compress_tail(u, v_next, pe_ref, w1_ref, w2_ref)


def _compress_body(rows_ref, w1_ref, w2_ref, pe_ref, o_ref, ybuf):
    perm = _regroup_perm()
    for grp in range(rows_ref.shape[0] // PERM_ROWS):
        _regroup_rows(perm, rows_ref[grp * PERM_ROWS:(grp + 1) * PERM_ROWS, :].astype(BF16), ybuf, grp)
    _compress_from(lambda p: ybuf[p], ybuf.shape[1], w1_ref, w2_ref, pe_ref, o_ref)


def compress_prompt(kv, w1bd, w2bd, pe4):
    b, t, _ = kv.shape
    nb = t // CMP_STRIDE
    return pl.pallas_call(
        _compress_body, grid=(b, 2),
        in_specs=[pl.BlockSpec((None, t, KV_WIDTH), lambda i, c: (i, 0, c)),
                  pl.BlockSpec((None, CMP_LEN, KV_WIDTH, KV_WIDTH), lambda i, c: (c, 0, 0, 0)),
                  pl.BlockSpec((None, KV_WIDTH, KV_WIDTH), lambda i, c: (c, 0, 0)),
                  pl.BlockSpec((None, CMP_LEN, KV_WIDTH), lambda i, c: (c, 0, 0))],
        out_specs=pl.BlockSpec((None, None, nb, KV_WIDTH), lambda i, c: (i, c, 0, 0)),
        out_shape=jax.ShapeDtypeStruct((b, 2, nb, KV_WIDTH), F32),
        scratch_shapes=[pltpu.VMEM((CMP_STRIDE, nb, KV_WIDTH), BF16)],
        compiler_params=_cparams(("parallel", "parallel")), name="compress_prompt",
    )(kv, w1bd, w2bd, pe4)


def _alibi_slopes():
    return [2.0 ** (-8.0 * (n + 1) / N_Q) for n in range(N_Q)]


SEL_CHUNK = 512


def _nsa_body(q_ref, ks_ref, kw_ref, vst_ref, vwt_ref, cmp_ref, gt_ref, bg_ref, o_ref,
              qe_sc, tot_sc, acc_sc, m_sc, l_sc, sel_sc, bias_sc, *, n_top, n_slc, kw_len):
    qb = pl.program_id(1)
    QB, CK = Q_BLOCK, SEL_CHUNK
    nq = GROUP * QB
    n_cmp = cmp_ref.shape[1]
    slopes = _alibi_slopes()
    lane_head = lax.broadcasted_iota(jnp.int32, (1, KV_WIDTH), 1) // HEAD_DIM
    t_row1 = qb * QB + lax.broadcasted_iota(jnp.int32, (1, QB), 1)
    t_row = jnp.concatenate([t_row1] * GROUP, axis=1)
    gates = jax.nn.sigmoid(gt_ref[...] + bg_ref[...])
    kc = cmp_ref[0].astype(BF16)
    vct = jnp.transpose(cmp_ref[1]).astype(BF16)
    key = lax.broadcasted_iota(jnp.int32, (CK, nq), 0)
    key_f = key.astype(F32)
    cmp_end = lax.broadcasted_iota(jnp.int32, (n_cmp, nq), 0) * CMP_STRIDE + (CMP_LEN - 1)
    mask_c = cmp_end <= t_row
    cmp_end_f = cmp_end.astype(F32)
    oj = lax.broadcasted_iota(jnp.int32, (128, n_cmp), 0)
    on = lax.broadcasted_iota(jnp.int32, (128, n_cmp), 1) * CMP_STRIDE
    overlap_t = ((on < oj * SLC_BLOCK + SLC_BLOCK) & (on + (CMP_LEN - 1) >= oj * SLC_BLOCK)
                 & (oj < n_slc)).astype(F32)
    blk = lax.broadcasted_iota(jnp.int32, (n_slc, 1), 0)
    cur = t_row1 // SLC_BLOCK
    forced = (blk == 0) | (blk == cur) | (blk == cur - 1)
    future = blk * SLC_BLOCK > t_row1

    def slope_row(g):
        return jnp.concatenate([jnp.full((1, QB), slopes[GROUP * g + r], F32) for r in range(GROUP)], axis=1)

    def gate_row(c, g):
        lo = c * N_Q + GROUP * g
        return jnp.concatenate([gates[lo + r:lo + r + 1, :] for r in range(GROUP)], axis=1)

    def head_rows(g):
        return slice(g * HEAD_DIM, (g + 1) * HEAD_DIM)


    for g in range(N_KV):
        gm = lane_head == g
        zero = jnp.zeros((QB, KV_WIDTH), BF16)
        qe = jnp.concatenate(
            [jnp.where(gm, q_ref[:, r * KV_WIDTH:(r + 1) * KV_WIDTH], zero) for r in range(GROUP)],
            axis=0) * (HEAD_DIM ** -0.5)
        qe_sc[g] = qe
        s = _dot_nt(kc, qe) + slope_row(g) * cmp_end_f
        s = jnp.where(mask_c, s, NEG)
        e = jnp.where(mask_c, jnp.exp(s - jnp.max(s, axis=0, keepdims=True)), 0.0)
        p_c = e / jnp.maximum(jnp.sum(e, axis=0, keepdims=True), 1e-30)
        tot_sc[g] = gate_row(0, g) * _dot(vct[head_rows(g)], p_c.astype(BF16))
        p_sum = p_c[:, 0:QB] + p_c[:, QB:2 * QB] + p_c[:, 2 * QB:3 * QB]
        imp = jnp.dot(overlap_t, p_sum, precision=HIGHEST, preferred_element_type=F32)[:n_slc]
        imp = jnp.where(forced, BIG, jnp.where(future, -BIG, imp))
        for j in range(n_slc):
            row_j = imp[j:j + 1, :]
            beats = (imp > row_j) | ((imp == row_j) & (blk < j))
            rank = jnp.sum(beats.astype(F32), axis=0, keepdims=True)
            sel_sc[g, j:j + 1, :] = jnp.where(rank < n_top, 0.0, NEG)
        m_sc[g] = jnp.full((1, nq), NEG, F32)
        l_sc[g] = jnp.zeros((1, nq), F32)
        acc_sc[g] = jnp.zeros((HEAD_DIM, nq), F32)
        bias_sc[g] = slope_row(g) * key_f

    def sel_chunk(c, diagonal):
        base = pl.multiple_of(c * CK, CK)
        k = ks_ref[pl.ds(base, CK), :]
        for g in range(N_KV):
            picked = jnp.concatenate(
                [jnp.broadcast_to(sel_sc[g, pl.ds(c * (CK // SLC_BLOCK) + i, 1), :], (SLC_BLOCK, QB))
                 for i in range(CK // SLC_BLOCK)], axis=0)
            picked = jnp.concatenate([picked] * GROUP, axis=1)
            s = _dot_nt(k, qe_sc[g]) + (bias_sc[g] + picked)
            if diagonal:
                s = jnp.where(key <= t_row - base, s, NEG)
            off = slope_row(g) * base.astype(F32)
            m_old = m_sc[g]
            m_new = jnp.maximum(m_old, jnp.max(s, axis=0, keepdims=True) + off)
            alpha = jnp.exp(m_old - m_new)
            p = jnp.exp(s - (m_new - off))
            l_sc[g] = alpha * l_sc[g] + jnp.sum(p, axis=0, keepdims=True)
            acc_sc[g] = alpha * acc_sc[g] + _dot(vst_ref[head_rows(g), pl.ds(base, CK)], p.astype(BF16))
            m_sc[g] = m_new

    def full_chunk(c, carry):
        sel_chunk(c, False)
        return carry

    n_full = (qb * QB) // CK
    lax.fori_loop(0, n_full, full_chunk, 0)
    sel_chunk(n_full, True)

    nkw = kw_len
    start = pl.multiple_of(jnp.maximum(qb * QB + QB - nkw, 0), QB)
    key_w = lax.broadcasted_iota(jnp.int32, (nkw, nq), 0)
    rel = t_row - start
    mask_w = (key_w <= rel) & (key_w >= rel - WINDOW)
    key_w_f = key_w.astype(F32)
    k_w = kw_ref[pl.ds(start, nkw), :]
    for g in range(N_KV):
        o_s = acc_sc[g] / jnp.maximum(l_sc[g], 1e-30)
        s = _dot_nt(k_w, qe_sc[g]) + slope_row(g) * key_w_f
        s = jnp.where(mask_w, s, NEG)
        p = jnp.exp(s - jnp.max(s, axis=0, keepdims=True))
        o_w = (_dot(vwt_ref[head_rows(g), pl.ds(start, nkw)], p.astype(BF16))
               / jnp.sum(p, axis=0, keepdims=True))
        tot = tot_sc[g] + gate_row(1, g) * o_s + gate_row(2, g) * o_w
        for r in range(GROUP):
            o_ref[r * KV_WIDTH + g * HEAD_DIM:r * KV_WIDTH + (g + 1) * HEAD_DIM, :] = (
                tot[:, r * QB:(r + 1) * QB].astype(o_ref.dtype))


def nsa_prompt(q, kb, vt, cmp, g_t, b_gate, b, t):
    n_slc = t // SLC_BLOCK
    n_top = min(N_SEL, n_slc)
    nqb = t // Q_BLOCK
    nq = GROUP * Q_BLOCK
    return pl.pallas_call(
        functools.partial(_nsa_body, n_top=n_top, n_slc=n_slc, kw_len=min(WINDOW + Q_BLOCK, t)),
        grid=(b, nqb),
        in_specs=[pl.BlockSpec((Q_BLOCK, TOK_WIDTH), lambda i, j: (i * nqb + j, 0)),
                  pl.BlockSpec((t, KV_WIDTH), lambda i, j: (i, 0)),
                  pl.BlockSpec((t, KV_WIDTH), lambda i, j: (i, 1)),
                  pl.BlockSpec((KV_WIDTH, t), lambda i, j: (0, i)),
                  pl.BlockSpec((KV_WIDTH, t), lambda i, j: (1, i)),
                  pl.BlockSpec((None, 2, t // CMP_STRIDE, KV_WIDTH), lambda i, j: (i, 0, 0, 0)),
                  pl.BlockSpec((3 * N_Q, Q_BLOCK), lambda i, j: (0, i * nqb + j)),
                  pl.BlockSpec((3 * N_Q, 1), lambda i, j: (0, 0))],
        out_specs=pl.BlockSpec((TOK_WIDTH, Q_BLOCK), lambda i, j: (0, i * nqb + j)),
        out_shape=jax.ShapeDtypeStruct((TOK_WIDTH, b * t), BF16),
        scratch_shapes=[pltpu.VMEM((N_KV, nq, KV_WIDTH), BF16), pltpu.VMEM((N_KV, HEAD_DIM, nq), F32),
                        pltpu.VMEM((N_KV, HEAD_DIM, nq), F32), pltpu.VMEM((N_KV, 1, nq), F32),
                        pltpu.VMEM((N_KV, 1, nq), F32), pltpu.VMEM((N_KV, n_slc, Q_BLOCK), F32),
                        pltpu.VMEM((N_KV, SEL_CHUNK, nq), F32)],
        compiler_params=_cparams(("parallel", "arbitrary")), name="nsa_prompt",
    )(q, kb, kb, vt, vt, cmp, g_t, b_gate.reshape(-1, 1))


def _compress_paged_body(pt_ref, pool_ref, w1_ref, w2_ref, pe_ref, o_ref, raw, ybuf, sems,
                         *, layer, n_pages):
    page = raw.shape[2]
    pages_per_grp = PERM_ROWS // page
    n_grp = n_pages // pages_per_grp
    step = pl.program_id(0) * 2 + pl.program_id(1)
    n_steps = pl.num_programs(0) * 2

    def page_copy(stp, p):
        seq, typ = stp // 2, stp % 2
        return pltpu.make_async_copy(
            pool_ref.at[layer, pt_ref[seq * n_pages + p], pl.ds(typ * KV_WIDTH, KV_WIDTH), :],
            raw.at[p], sems.at[p // pages_per_grp])

    def start_all(stp):
        def body(p, carry):
            page_copy(stp, p).start()
            return carry
        lax.fori_loop(0, n_pages, body, 0)

    @pl.when(step == 0)
    def _():
        start_all(step)

    perm = _regroup_perm()

    unroll = 4 if n_grp % 4 == 0 else 1

    def regroup(it, carry):
        for j in range(unroll * pages_per_grp):
            page_copy(step, it * unroll * pages_per_grp + j).wait()
        for j in range(unroll):
            grp = it * unroll + j
            x_t = jnp.concatenate([raw[grp * pages_per_grp + k] for k in range(pages_per_grp)], axis=1)
            _regroup_rows(perm, x_t.astype(BF16), ybuf, grp, transposed=True)
        return carry

    lax.fori_loop(0, n_grp // unroll, regroup, 0)

    @pl.when(step + 1 < n_steps)
    def _():
        start_all(step + 1)

    _compress_from(lambda p: ybuf[p], ybuf.shape[1], w1_ref, w2_ref, pe_ref, o_ref)


def compress_paged(pool, page_table, layer, w1bd, w2bd, pe4):
    b, n_pages = page_table.shape
    page = pool.shape[3]
    assert PERM_ROWS % page == 0 and n_pages % (PERM_ROWS // page) == 0
    nb = n_pages * page // CMP_STRIDE
    grid_spec = pltpu.PrefetchScalarGridSpec(
        num_scalar_prefetch=1, grid=(b, 2),
        in_specs=[pl.BlockSpec(memory_space=pl.ANY),
                  pl.BlockSpec((None, CMP_LEN, KV_WIDTH, KV_WIDTH), lambda i, c, pt: (c, 0, 0, 0)),
                  pl.BlockSpec((None, KV_WIDTH, KV_WIDTH), lambda i, c, pt: (c, 0, 0)),
                  pl.BlockSpec((None, CMP_LEN, KV_WIDTH), lambda i, c, pt: (c, 0, 0))],
        out_specs=pl.BlockSpec((None, None, nb, KV_WIDTH), lambda i, c, pt: (i, c, 0, 0)),
        scratch_shapes=[pltpu.VMEM((n_pages, KV_WIDTH, page), F32),
                        pltpu.VMEM((CMP_STRIDE, nb, KV_WIDTH), BF16),
                        pltpu.SemaphoreType.DMA((n_pages * page // PERM_ROWS,))])
    return pl.pallas_call(
        functools.partial(_compress_paged_body, layer=layer, n_pages=n_pages), grid_spec=grid_spec,
        out_shape=jax.ShapeDtypeStruct((b, 2, nb, KV_WIDTH), F32),
        compiler_params=_cparams(("arbitrary", "arbitrary")), name="compress_paged",
    )(page_table.reshape(-1), pool, w1bd, w2bd, pe4)


def _decode_queries(q_ref):
    lane_head = lax.broadcasted_iota(jnp.int32, (1, KV_WIDTH), 1) // HEAD_DIM
    row4 = lax.broadcasted_iota(jnp.int32, (N_KV, 1), 0)
    parts = []
    for r in range(GROUP):
        q_r = jnp.broadcast_to(q_ref[:, r * KV_WIDTH:(r + 1) * KV_WIDTH], (N_KV, KV_WIDTH))
        parts.append(jnp.where(lane_head == row4, q_r, 0.0))
    parts.append(jnp.zeros((16 - N_Q, KV_WIDTH), F32))
    qe = (jnp.concatenate(parts, axis=0) * (HEAD_DIM ** -0.5)).astype(BF16)
    row = lax.broadcasted_iota(jnp.int32, (16, 1), 0)
    slopes = _alibi_slopes()
    slope_col = jnp.zeros((16, 1), F32)
    for r in range(GROUP):
        for g in range(N_KV):
            slope_col = jnp.where(row == N_KV * r + g, slopes[GROUP * g + r], slope_col)
    return qe, slope_col, row % N_KV, lane_head


def _nsa_select_body(q_ref, cmp_ref, oc_ref, idx_ref, *, t_pos, n_slc, n_top):
    nc = cmp_ref.shape[1]
    nj = idx_ref.shape[1]
    qe, slope_col, _, _ = _decode_queries(q_ref)
    kc = cmp_ref[0].astype(BF16)
    vc = cmp_ref[1].astype(BF16)
    cmp_end = lax.broadcasted_iota(jnp.int32, (1, nc), 1) * CMP_STRIDE + (CMP_LEN - 1)
    mask = cmp_end <= t_pos
    s = _dot_nt(qe, kc) + slope_col * cmp_end.astype(F32)
    s = jnp.where(mask, s, NEG)
    e = jnp.where(mask, jnp.exp(s - jnp.max(s, axis=-1, keepdims=True)), 0.0)
    p = e / jnp.maximum(jnp.sum(e, axis=-1, keepdims=True), 1e-30)
    oc_ref[...] = _dot(p.astype(BF16), vc)
    p_sum = p[0:N_KV] + p[N_KV:2 * N_KV] + p[2 * N_KV:3 * N_KV]
    p_sum = jnp.concatenate([p_sum, jnp.zeros((8 - N_KV, nc), F32)], axis=0)
    oj = lax.broadcasted_iota(jnp.int32, (nj, nc), 0) * SLC_BLOCK
    on = lax.broadcasted_iota(jnp.int32, (nj, nc), 1) * CMP_STRIDE
    overlap_t = ((on < oj + SLC_BLOCK) & (on + (CMP_LEN - 1) >= oj)).astype(F32)
    imp = jnp.transpose(jnp.dot(overlap_t, jnp.transpose(p_sum), precision=HIGHEST,
                                preferred_element_type=F32))
    j = lax.broadcasted_iota(jnp.int32, (8, nj), 1)
    cur = t_pos // SLC_BLOCK
    forced = (j == 0) | (j == cur) | (j == cur - 1)
    imp = jnp.where(forced, BIG, jnp.where(j * SLC_BLOCK > t_pos, -BIG, imp))
    imp = jnp.where(j < n_slc, imp, -jnp.inf)
    picked = jnp.zeros((8, nj), jnp.int32)
    for i in range(n_top):
        mx = jnp.max(imp, axis=-1, keepdims=True)
        arg = jnp.min(jnp.where(imp == mx, j, nj), axis=-1, keepdims=True)
        picked = jnp.where(j == i, arg, picked)
        imp = jnp.where(j == arg, -jnp.inf, imp)
    idx_ref[...] = picked


def nsa_select(q, cmp, t_pos, n_slc, n_top):
    b = q.shape[0]
    nc = cmp.shape[2]
    nj = -(-n_slc // 128) * 128
    return pl.pallas_call(
        functools.partial(_nsa_select_body, t_pos=t_pos, n_slc=n_slc, n_top=n_top), grid=(b,),
        in_specs=[pl.BlockSpec((None, 1, TOK_WIDTH), lambda i: (i, 0, 0)),
                  pl.BlockSpec((None, 2, nc, KV_WIDTH), lambda i: (i, 0, 0, 0))],
        out_specs=[pl.BlockSpec((None, 16, KV_WIDTH), lambda i: (i, 0, 0)),
                   pl.BlockSpec((None, 8, nj), lambda i: (i, 0, 0))],
        out_shape=[jax.ShapeDtypeStruct((b, 16, KV_WIDTH), F32),
                   jax.ShapeDtypeStruct((b, 8, nj), jnp.int32)],
        compiler_params=_cparams(("parallel",)), name="nsa_select",
    )(q, cmp)


def _nsa_decode_body(pt_ref, idx_ref, q_ref, kvn_ref, win_ref, oc_ref, gp_ref, bg_ref, pool_ref, o_ref,
                     blkbuf, sem, *, layer, t_pos, n_top, n_pages, per_page):
    n_past_blocks = n_pages * per_page
    b = pl.program_id(0)

    slot = b % 2

    def blk_copy(seq, k):
        bid = jnp.minimum(idx_ref[seq * N_KV * n_top + k], n_past_blocks - 1)
        return pltpu.make_async_copy(
            pool_ref.at[layer, pt_ref[seq * n_pages + bid // per_page], pl.ds(2 * KV_WIDTH, 2 * KV_WIDTH), :],
            blkbuf.at[seq % 2, k], sem.at[seq % 2])

    def start_all(seq):
        for k in range(N_KV * n_top):
            blk_copy(seq, k).start()

    @pl.when(b == 0)
    def _():
        start_all(b)

    @pl.when(b + 1 < pl.num_programs(0))
    def _():
        start_all(b + 1)

    qe, slope_col, row_group, lane_head = _decode_queries(q_ref)
    qf = qe.astype(F32)
    kvn = kvn_ref[...]
    new = lambda c: kvn[:, c * KV_WIDTH:(c + 1) * KV_WIDTH]
    t_f = float(t_pos)

    def attend(k_t, v_t, mask, pos_f, k_new, v_new):
        s = _dot(qe, k_t.astype(BF16)) + slope_col * pos_f
        s = jnp.where(mask, s, NEG)
        s_new = (jnp.sum(qf * k_new.astype(BF16).astype(F32), axis=-1, keepdims=True) + slope_col * t_f)
        m = jnp.maximum(jnp.max(s, axis=-1, keepdims=True), s_new)
        p = jnp.where(mask, jnp.exp(s - m), 0.0)
        p_new = jnp.exp(s_new - m)
        l = jnp.sum(p, axis=-1, keepdims=True) + p_new
        return (_dot_nt(p.astype(BF16), v_t.astype(BF16)) + p_new * v_new) / l

    wbuf = win_ref.shape[1]
    pos_w = (t_pos - wbuf) + lax.broadcasted_iota(jnp.int32, (1, wbuf), 1)
    mask_w = (pos_w >= 0) & (t_pos - pos_w <= WINDOW)
    o_w = attend(win_ref[:KV_WIDTH, :], win_ref[KV_WIDTH:, :], mask_w, pos_w.astype(F32), new(4), new(5))

    for k in range(N_KV * n_top):
        blk_copy(b, k).wait()

    page = blkbuf.shape[3]
    o_s = jnp.zeros((16, KV_WIDTH), F32)
    key = lax.broadcasted_iota(jnp.int32, (1, n_top * page), 1)
    for g in range(N_KV):
        pos = jnp.full((1, n_top * page), t_pos + 1, jnp.int32)
        for i in range(n_top):
            bid = idx_ref[(b * N_KV + g) * n_top + i]
            base = jnp.where(bid < n_past_blocks, (bid // per_page) * page, t_pos + 1)
            in_blk = (key // page == i) & ((key % page) // SLC_BLOCK == bid % per_page)
            pos = jnp.where(in_blk, base + key % page, pos)
        k_t = jnp.concatenate([blkbuf[slot, g * n_top + i, :KV_WIDTH, :] for i in range(n_top)], axis=1)
        v_t = jnp.concatenate([blkbuf[slot, g * n_top + i, KV_WIDTH:, :] for i in range(n_top)], axis=1)
        o_g = attend(k_t, v_t, pos <= t_pos, pos.astype(F32), new(2), new(3))
        o_s = jnp.where(row_group == g, o_g, o_s)

    gates = jax.nn.sigmoid(gp_ref[...] + bg_ref[...])
    row = lax.broadcasted_iota(jnp.int32, (16, 1), 0)
    tot = jnp.zeros((16, KV_WIDTH), F32)
    for c, o in enumerate((oc_ref[...], o_s, o_w)):
        gcol = jnp.zeros((16, 1), F32)
        for r in range(GROUP):
            for g in range(N_KV):
                lane = c * N_Q + GROUP * g + r
                gcol = jnp.where(row == N_KV * r + g, gates[:, lane:lane + 1], gcol)
        tot = tot + gcol * o
    out = []
    for r in range(GROUP):
        h_r = jnp.zeros((1, KV_WIDTH), F32)
        for g in range(N_KV):
            h_r = jnp.where(lane_head == g, tot[N_KV * r + g:N_KV * r + g + 1], h_r)
        out.append(h_r)
    o_ref[...] = jnp.concatenate(out, axis=1)


def nsa_decode(pool, page_table, layer, idx, q, kv_new, win, o_c, g_pre, b_gate, t_pos, n_top):
    b, n_pages = page_table.shape
    page = pool.shape[3]
    per_page = page // SLC_BLOCK
    wbuf = win.shape[2]
    row = lambda w: pl.BlockSpec((None, 1, w), lambda s, pt, ix: (s, 0, 0))
    grid_spec = pltpu.PrefetchScalarGridSpec(
        num_scalar_prefetch=2, grid=(b,),
        in_specs=[row(TOK_WIDTH), row(6 * KV_WIDTH),
                  pl.BlockSpec((None, 2 * KV_WIDTH, wbuf), lambda s, pt, ix: (s, 0, 0)),
                  pl.BlockSpec((None, 16, KV_WIDTH), lambda s, pt, ix: (s, 0, 0)),
                  row(3 * N_Q),
                  pl.BlockSpec((1, 3 * N_Q), lambda s, pt, ix: (0, 0)),
                  pl.BlockSpec(memory_space=pl.ANY)],
        out_specs=row(TOK_WIDTH),
        scratch_shapes=[pltpu.VMEM((2, N_KV * n_top, 2 * KV_WIDTH, page), F32),
                        pltpu.SemaphoreType.DMA((2,))])
    idx_flat = idx[:, :N_KV, :n_top].reshape(-1)
    return pl.pallas_call(
        functools.partial(_nsa_decode_body, layer=layer, t_pos=t_pos, n_top=n_top, n_pages=n_pages,
                          per_page=per_page),
        grid_spec=grid_spec, out_shape=jax.ShapeDtypeStruct((b, 1, TOK_WIDTH), F32),
        compiler_params=_cparams(("arbitrary",)), name="nsa_decode",
    )(page_table.reshape(-1), idx_flat, q, kv_new, win, o_c, g_pre, b_gate.reshape(1, -1), pool)


def _rgd_perm():
    idx = []
    for r in range(GROUP):
        for g in range(N_KV):
            n = GROUP * g + r
            idx.extend(range(n * HEAD_DIM, (n + 1) * HEAD_DIM))
    return jnp.array(idx, jnp.int32)


def _cols_to_rgd(w):
    lead = w.shape[:-1]
    w = w.reshape(lead + (N_KV, GROUP, HEAD_DIM))
    return jnp.swapaxes(w, -3, -2).reshape(lead + (TOK_WIDTH,))


def _block_diag4(w):
    eye = jnp.eye(N_KV, dtype=w.dtype)
    out = jnp.einsum('ab,...ij->...aibj', eye, w)
    return out.reshape(w.shape[:-2] + (KV_WIDTH, KV_WIDTH))


MOE_TILE = 1024
MOE_TILES_PER_STEP = 1


def kernel(x_prompt, x_sample, cache_mem_k, cache_mem_v, state_mlstm_C, state_mlstm_n, state_mlstm_m,
           state_mlstm_conv, cache_nsa, cache_nsa_win, page_table, mem_prompt, norm_mix, norm_mem, norm_ffn,
           norm_final, w_mem_kv, w_in_a, b_gate_a, conv_a, head_norm_a, w_in_b, b_gate_b, w_cmp1, w_cmp2,
           pe_cmp, w_out, w_ffn_gu, w_ffn_d, w_router, w_moe_gu, w_moe_d):
    bp, seq, d = x_prompt.shape
    bs, dseq, _ = x_sample.shape
    assert dseq == 1 and norm_mix.shape[0] == 2 and w_in_a.shape[0] == 1 and w_in_b.shape[0] == 1
    n_mem = mem_prompt.shape[1]
    bf = lambda a: a.astype(BF16)
    xp = x_prompt.reshape(bp * seq, d).astype(F32)
    xs = x_sample.reshape(bs, d).astype(F32)
    mem = mem_prompt.reshape(bp * n_mem, d).astype(F32)

    def mem_kv(i):
        w = w_mem_kv[i]
        mk, mv = norm_proj(mem, norm_mem[i], [bf(w[:, :X_WIDTH]), bf(w[:, X_WIDTH:])], [False, False])
        return mk.reshape(bp, n_mem, X_WIDTH), mv.reshape(bp, n_mem, X_WIDTH)

    def mix_out(i, x, h, a, wh):
        return out_proj(x, h.reshape(x.shape[0], -1), a.reshape(x.shape[0], -1), wh, bf(w_out[i][TOK_WIDTH:]))

    wa = w_in_a[0]
    t3, t4 = 3 * TOK_WIDTH, 4 * TOK_WIDTH
    w0 = [bf(_pad_heads(wa[:, :TOK_WIDTH])), bf(_pad_heads(wa[:, TOK_WIDTH:2 * TOK_WIDTH])),
          bf(_pad_heads(wa[:, 2 * TOK_WIDTH:t3])), bf(_pad_heads(wa[:, t3:t4])),
          bf(wa[:, t4 + 2 * M_HEADS:]), wa[:, t4:t4 + 2 * M_HEADS]]
    hi0 = [False] * 5 + [True]
    cwq = _pad_heads(conv_a[0][:, :TOK_WIDTH])
    cwk = _pad_heads(conv_a[0][:, TOK_WIDTH:])
    head_g = _pad_heads(head_norm_a[0]).reshape(1, M_PAD_WIDTH)
    wh0 = w_out[0][:TOK_WIDTH].reshape(M_HEADS, M_HEAD_DIM, d)
    wh0 = bf(jnp.pad(wh0, ((0, 0), (0, M_HEAD_PAD - M_HEAD_DIM), (0, 0))).reshape(M_PAD_WIDTH, d))
    mk0, mv0 = mem_kv(0)

    qp, kp, vp, op, xqp, gp = norm_proj(xp, norm_mix[0], w0, hi0)
    r3 = lambda a: a.reshape(bp, seq, a.shape[-1])
    hp, c_p, n_p, m_p = mlstm_prompt(r3(qp), r3(kp), r3(vp), r3(op), r3(gp), cwq, cwk, b_gate_a[0], head_g)
    conv_p = jnp.concatenate([_unpad_heads(r3(qp)[:, seq - (CONV_W - 1):]),
                              _unpad_heads(r3(kp)[:, seq - (CONV_W - 1):])], axis=-1)
    ap = mem_attention(r3(xqp), mk0, mv0)
    xp = mix_out(0, xp, hp, ap, wh0)
    xp = ffn(xp, norm_ffn[0], bf(w_ffn_gu[0]), bf(w_ffn_d[0]))

    qs, ks_, vs, os_, xqs, gs = norm_proj(xs, norm_mix[0], w0, hi0)
    s3 = lambda a: a.reshape(bs, 1, a.shape[-1])
    hist = state_mlstm_conv[0].astype(F32)
    hs, c_s, n_s, m_s = mlstm_step(
        s3(qs), s3(ks_), s3(vs), s3(os_), s3(gs), _pad_heads(hist[..., :TOK_WIDTH]),
        _pad_heads(hist[..., TOK_WIDTH:]), state_mlstm_C[0].astype(F32), state_mlstm_n[0].astype(F32),
        state_mlstm_m[0].astype(F32), cwq, cwk, b_gate_a[0], head_g)
    qk_new = jnp.concatenate([_unpad_heads(s3(qs)), _unpad_heads(s3(ks_))], axis=-1)
    conv_s = jnp.concatenate([hist, qk_new], axis=1)[:, 1:]
    as_ = mem_attention(s3(xqs), cache_mem_k[0].reshape(bs, n_mem, X_WIDTH).astype(F32),
                        cache_mem_v[0].reshape(bs, n_mem, X_WIDTH).astype(F32))
    xs = mix_out(0, xs, hs, as_, wh0)
    xs = ffn(xs, norm_ffn[0], bf(w_ffn_gu[0]), bf(w_ffn_d[0]))

    wb = w_in_b[0]
    kv_end = TOK_WIDTH + 6 * KV_WIDTH
    w1 = [bf(_cols_to_rgd(wb[:, :TOK_WIDTH])), bf(wb[:, TOK_WIDTH:kv_end]), bf(wb[:, kv_end + 3 * N_Q:]),
          wb[:, kv_end:kv_end + 3 * N_Q]]
    hi1 = [False] * 3 + [True]
    wh1 = w_out[1][:TOK_WIDTH].reshape(N_KV, GROUP, HEAD_DIM, d)
    wh1 = bf(jnp.swapaxes(wh1, 0, 1).reshape(TOK_WIDTH, d))
    w1bd = bf(_block_diag4(w_cmp1[0].reshape(2, CMP_LEN, HEAD_DIM, HEAD_DIM)))
    w2bd = bf(_block_diag4(w_cmp2[0]))
    pe4 = jnp.tile(pe_cmp[0], (1, 1, N_KV))
    mk1, mv1 = mem_kv(1)

    v_cols = lambda c: wb[:, TOK_WIDTH + c * KV_WIDTH:TOK_WIDTH + (c + 1) * KV_WIDTH]
    wvt = bf(jnp.concatenate([v_cols(3), v_cols(5)], axis=1).T)
    q1, kv1, kb1, vt1, xq1, gt1 = nsa_proj(xp, norm_mix[1], w1[0], w1[1], wvt, w1[2], w1[3].T)
    cmp_p = compress_prompt(r3(kv1), w1bd, w2bd, pe4)
    hp1t = nsa_prompt(q1, kb1, vt1, cmp_p, gt1, b_gate_b[0], bp, seq)
    rows_p = r3(kv1)[:, :, :4 * KV_WIDTH].reshape(bp, seq, 4, N_KV, HEAD_DIM)
    wlen = min(WINDOW, seq)
    win_p = r3(kv1)[:, seq - wlen:, 4 * KV_WIDTH:].reshape(bp, wlen, 2, N_KV, HEAD_DIM)
    ap1 = mem_attention(r3(xq1), mk1, mv1)
    xp = out_proj_t(xp, hp1t, ap1.reshape(bp * seq, X_WIDTH), wh1, bf(w_out[1][TOK_WIDTH:]))

    n_pool, page = cache_nsa.shape[1], cache_nsa.shape[2]
    past_len = page_table.shape[1] * page
    wbuf = cache_nsa_win.shape[2]
    n_slc = -(-(past_len + 1) // SLC_BLOCK)
    n_top = min(N_SEL, n_slc)
    pool = jnp.transpose(cache_nsa.astype(F32), (0, 1, 3, 4, 5, 2)).reshape(
        cache_nsa.shape[0], n_pool, 4 * KV_WIDTH, page)
    q1s, kv1s, xq1s, g1s = norm_proj(xs, norm_mix[1], w1, hi1)
    cmp_s = compress_paged(pool, page_table, 0, w1bd, w2bd, pe4)
    oc_s, idx_s = nsa_select(s3(q1s), cmp_s, past_len, n_slc, n_top)
    win_old = cache_nsa_win[0].astype(F32)
    win_t = jnp.transpose(win_old, (0, 2, 3, 4, 1)).reshape(bs, 2 * KV_WIDTH, wbuf)
    hs1 = nsa_decode(pool, page_table, 0, idx_s, s3(q1s), s3(kv1s), win_t,
                     oc_s, s3(g1s), b_gate_b[0], past_len, n_top)
    rows_s = kv1s[:, :4 * KV_WIDTH].reshape(bs, 1, 4, N_KV, HEAD_DIM)
    win_new = kv1s[:, 4 * KV_WIDTH:].reshape(bs, 1, 2, N_KV, HEAD_DIM)
    win_s = jnp.concatenate([win_old, win_new], axis=1)[:, -wbuf:]
    as1 = mem_attention(s3(xq1s), cache_mem_k[1].reshape(bs, n_mem, X_WIDTH).astype(F32),
                        cache_mem_v[1].reshape(bs, n_mem, X_WIDTH).astype(F32))
    xs = mix_out(1, xs, hs1, as1, wh1)

    w_r = jnp.pad(w_router[0], ((0, 0), (0, 128 - N_EXPERTS)))
    w_gu, w_dn = bf(w_moe_gu[0]), bf(w_moe_d[0])

    def experts(x, tm):
        u, comb, rank, counts = router(x, norm_ffn[1], w_r, tm)
        return moe(u, comb, rank, counts, w_gu, w_dn, tm, x, norm_final)

    y_prompt = experts(xp, min(MOE_TILE, bp * seq)).reshape(bp, seq, d)
    y_sample = experts(xs, bs).reshape(bs, 1, d)

    unflat = lambda a: a.reshape(bp, n_mem, X_HEADS, HEAD_DIM)
    return (y_prompt, y_sample,
            jnp.stack([unflat(mk0), unflat(mk1)]), jnp.stack([unflat(mv0), unflat(mv1)]),
            c_p[None], n_p[None], m_p[None, :, :, 0], conv_p[None], rows_p[None], win_p[None],
            c_s[None], n_s[None], m_s[None, :, :, 0], conv_s[None], rows_s[None], win_s[None])
```
